```python
import math
import jax, jax.numpy as jnp
from jax import lax
import numpy as np

D_MODEL = 2048
BATCH = 4
SEQ = 4096
DEPTH = 1

N_META = 16
GRID_W = 64
ROW_WIN = 8
COL_WIN = 16
NA_HEAD_DIM = 64
NA_WIDTH = D_MODEL // 2
NA_HEADS = NA_WIDTH // NA_HEAD_DIM
SSM_WIDTH = D_MODEL // 2
SSM_GROUP = 16
SSM_GROUPS = SSM_WIDTH // SSM_GROUP
SSM_STATE = 64
DT_MIN = 1e-3
DT_MAX = 1e-1
D_FF = ((8 * D_MODEL // 3 + 255) // 256) * 256
CONV_W = 3
RMS_EPS = 1e-6
IN_SPLITS = [NA_WIDTH, 2 * NA_WIDTH, 3 * NA_WIDTH, 3 * NA_WIDTH + SSM_WIDTH,
             3 * NA_WIDTH + SSM_WIDTH + D_MODEL]
IN_COLS = 3 * NA_WIDTH + SSM_WIDTH + 2 * D_MODEL

kernel_name = "hybrid_natten_s5_convffn_encoder"


def rms_norm(x, g):
    xf = x.astype(jnp.float32)
    y = xf * lax.rsqrt(jnp.mean(xf * xf, axis=-1, keepdims=True) + RMS_EPS)
    return (y * g.astype(jnp.float32)).astype(x.dtype)


def neighbourhood_attention(q, k, v, rpb):
    b, L, h, dh = q.shape
    rows = (L - N_META) // GRID_W
    kr = min(ROW_WIN, rows)
    scale = dh ** -0.5
    qm, km, vm = q[:, :N_META], k[:, :N_META], v[:, :N_META]
    qg = q[:, N_META:].reshape(b, rows, GRID_W, h, dh)
    kg = k[:, N_META:].reshape(b, rows, GRID_W, h, dh)
    vg = v[:, N_META:].reshape(b, rows, GRID_W, h, dh)

    s_m = jnp.einsum('bqhd,bkhd->bhqk', qm, km).astype(jnp.float32) * scale
    p_m = jax.nn.softmax(s_m, axis=-1).astype(v.dtype)
    o_m = jnp.einsum('bhqk,bkhd->bqhd', p_m, vm)

    r_idx = jnp.arange(rows)
    row_start = jnp.clip(r_idx - kr // 2, 0, rows - kr)
    dr = row_start[:, None] + jnp.arange(kr)[None] - r_idx[:, None] + (ROW_WIN - 1)
    c_idx = jnp.arange(GRID_W)
    col_start = jnp.clip(c_idx - COL_WIN // 2, 0, GRID_W - COL_WIN)
    col_keys = col_start[:, None] + jnp.arange(COL_WIN)[None]
    dc = col_keys - c_idx[:, None] + (COL_WIN - 1)

    def row_block(args):
        q_r, r0, dr_r = args
        k_rows = lax.dynamic_slice_in_dim(kg, r0, kr, axis=1)
        v_rows = lax.dynamic_slice_in_dim(vg, r0, kr, axis=1)
        k_win = k_rows[:, :, col_keys]
        v_win = v_rows[:, :, col_keys]
        bias = rpb[:, dr_r[:, None, None], dc[None]]
        s_loc = (jnp.einsum('bchd,bicjhd->bhcij', q_r, k_win).astype(jnp.float32) * scale
                 + jnp.transpose(bias, (0, 2, 1, 3))[None].astype(jnp.float32))
        s_loc = s_loc.reshape(b, h, GRID_W, kr * COL_WIN)
        s_meta = jnp.einsum('bchd,bmhd->bhcm', q_r, km).astype(jnp.float32) * scale
        p = jax.nn.softmax(jnp.concatenate([s_loc, s_meta], axis=-1), axis=-1).astype(v.dtype)
        p_loc = p[..., :kr * COL_WIN].reshape(b, h, GRID_W, kr, COL_WIN)
        p_meta = p[..., kr * COL_WIN:]
        return (jnp.einsum('bhcij,bicjhd->bchd', p_loc, v_win)
                + jnp.einsum('bhcm,bmhd->bchd', p_meta, vm))

    o_g = lax.map(row_block, (jnp.moveaxis(qg, 1, 0), row_start, dr))
    o_g = jnp.moveaxis(o_g, 0, 1).reshape(b, rows * GRID_W, h, dh)
    return jnp.concatenate([o_m, o_g], axis=1)


def _ssm_combine(left, right):
    a_l, b_l = left
    a_r, b_r = right
    return a_r * a_l, a_r * b_l + b_r


def s5_direction(u, lam_re, lam_im, log_step, b_re, b_im, c_re, c_im, reverse):
    f32 = jnp.float32
    lam = lax.complex(lam_re.astype(f32), lam_im.astype(f32))
    step = jnp.exp(log_step.astype(f32))[:, None]
    lam_bar = jnp.exp(lam * step)
    b_mat = lax.complex(b_re.astype(f32), b_im.astype(f32))
    b_bar = ((lam_bar - 1.0) / lam)[..., None] * b_mat
    bu = jnp.einsum('blgc,gnc->blgn', u.astype(jnp.complex64), b_bar)
    a = jnp.broadcast_to(lam_bar, bu.shape)
    _, states = lax.associative_scan(_ssm_combine, (a, bu), axis=1, reverse=reverse)
    c_mat = lax.complex(c_re.astype(f32), c_im.astype(f32))
    return jnp.real(jnp.einsum('blgn,gcn->blgc', states, c_mat))


def s5_branch(u, lam_re, lam_im, log_step, b_re, b_im, c_re, c_im, d_skip, w_glu):
    b, L, _ = u.shape
    ug = u.astype(jnp.float32).reshape(b, L, SSM_GROUPS, SSM_GROUP)
    y = (s5_direction(ug, lam_re[0], lam_im[0], log_step[0], b_re[0], b_im[0], c_re[0], c_im[0], False)
         + s5_direction(ug, lam_re[1], lam_im[1], log_step[1], b_re[1], b_im[1], c_re[1], c_im[1], True)
         + ug * d_skip.astype(jnp.float32).reshape(SSM_GROUPS, SSM_GROUP))
    y = jax.nn.gelu(y.reshape(b, L, SSM_WIDTH).astype(u.dtype))
    return y * jax.nn.sigmoid(y @ w_glu)


def conv_gated_ffn(h, w_up, conv_w, conv_b, w_down):
    a, g = jnp.split(h @ w_up, 2, axis=-1)
    ap = jnp.pad(a, ((0, 0), (1, 1), (0, 0)))
    a = ap[:, :-2] * conv_w[0] + ap[:, 1:-1] * conv_w[1] + ap[:, 2:] * conv_w[2] + conv_b
    return (jax.nn.gelu(a) * g) @ w_down


def setup_inputs(seed: int = 0) -> dict:
    key = jax.random.key(seed)
    ks = jax.random.split(key, 24)
    f32 = jnp.float32
    G, N, C = SSM_GROUPS, SSM_STATE, SSM_GROUP

    def nrm(k, shape, scale):
        return jax.random.normal(k, shape, f32) * scale

    n_idx = jnp.arange(N, dtype=f32)
    lam_re = -0.5 + 0.01 * jax.random.normal(ks[5], (DEPTH, 2, G, N), f32)
    lam_im = math.pi * n_idx + 0.01 * jax.random.normal(ks[6], (DEPTH, 2, G, N), f32)
    log_step = jax.random.uniform(ks[7], (DEPTH, 2, G), f32, math.log(DT_MIN), math.log(DT_MAX))
    b_scale = (2.0 * C) ** -0.5
    c_scale = (2.0 * N) ** -0.5
    return {
        "x": nrm(ks[0], (BATCH, SEQ, D_MODEL), 1.0),
        "meta_tokens": nrm(ks[1], (N_META, D_MODEL), 1.0),
        "norm1_g": 1.0 + nrm(ks[2], (DEPTH, D_MODEL), 0.02),
        "w_in": nrm(ks[3], (DEPTH, D_MODEL, IN_COLS), D_MODEL ** -0.5),
        "na_rpb": nrm(ks[4], (DEPTH, NA_HEADS, 2 * ROW_WIN - 1, 2 * COL_WIN - 1), 0.02),
        "ssm_lam_re": lam_re,
        "ssm_lam_im": lam_im,
        "ssm_log_step": log_step,
        "ssm_b_re": nrm(ks[8], (DEPTH, 2, G, N, C), b_scale),
        "ssm_b_im": nrm(ks[9], (DEPTH, 2, G, N, C), b_scale),
        "ssm_c_re": nrm(ks[10], (DEPTH, 2, G, C, N), c_scale),
        "ssm_c_im": nrm(ks[11], (DEPTH, 2, G, C, N), c_scale),
        "ssm_d": nrm(ks[12], (DEPTH, SSM_WIDTH), 1.0),
        "w_glu": nrm(ks[13], (DEPTH, SSM_WIDTH, SSM_WIDTH), SSM_WIDTH ** -0.5),
        "w_proj_na": nrm(ks[14], (DEPTH, NA_WIDTH, D_MODEL), NA_WIDTH ** -0.5),
        "w_proj_ssm": nrm(ks[15], (DEPTH, SSM_WIDTH, D_MODEL), SSM_WIDTH ** -0.5),
        "w_out": nrm(ks[16], (DEPTH, D_MODEL, D_MODEL), D_MODEL ** -0.5),
        "norm2_g": 1.0 + nrm(ks[17], (DEPTH, D_MODEL), 0.02),
        "w_up": nrm(ks[18], (DEPTH, D_MODEL, 2 * D_FF), D_MODEL ** -0.5),
        "conv_w": nrm(ks[19], (DEPTH, CONV_W, D_FF), CONV_W ** -0.5),
        "conv_b": nrm(ks[20], (DEPTH, D_FF), 0.01),
        "w_down": nrm(ks[21], (DEPTH, D_FF, D_MODEL), D_FF ** -0.5),
        "final_g": 1.0 + nrm(ks[22], (D_MODEL,), 0.02),
    }


def reference(x, meta_tokens, norm1_g, w_in, na_rpb, ssm_lam_re, ssm_lam_im, ssm_log_step,
              ssm_b_re, ssm_b_im, ssm_c_re, ssm_c_im, ssm_d, w_glu, w_proj_na, w_proj_ssm,
              w_out, norm2_g, w_up, conv_w, conv_b, w_down, final_g):
    b = x.shape[0]
    meta = jnp.broadcast_to(meta_tokens.astype(x.dtype)[None], (b, N_META, D_MODEL))
    h = jnp.concatenate([meta, x], axis=1)
    L = h.shape[1]
    for l in range(DEPTH):
        hn = rms_norm(h, norm1_g[l])
        q, k, v, u, g_na, g_ssm = jnp.split(hn @ w_in[l], IN_SPLITS, axis=-1)
        q = q.reshape(b, L, NA_HEADS, NA_HEAD_DIM)
        k = k.reshape(b, L, NA_HEADS, NA_HEAD_DIM)
        v = v.reshape(b, L, NA_HEADS, NA_HEAD_DIM)
        y_na = neighbourhood_attention(q, k, v, na_rpb[l]).reshape(b, L, NA_WIDTH) @ w_proj_na[l]
        y_ssm = s5_branch(u, ssm_lam_re[l], ssm_lam_im[l], ssm_log_step[l], ssm_b_re[l],
                          ssm_b_im[l], ssm_c_re[l], ssm_c_im[l], ssm_d[l], w_glu[l]) @ w_proj_ssm[l]
        mixed = jax.nn.sigmoid(g_na) * y_na + jax.nn.sigmoid(g_ssm) * y_ssm
        h = h + mixed @ w_out[l]
        h = h + conv_gated_ffn(rms_norm(h, norm2_g[l]), w_up[l], conv_w[l], conv_b[l], w_down[l])
    h = rms_norm(h, final_g)
    return h[:, N_META:]
```

```python
import functools
import math

import jax
import jax.numpy as jnp
from jax import lax
from jax.experimental import pallas as pl
from jax.experimental.pallas import tpu as pltpu

F32 = jnp.float32
BF16 = jnp.bfloat16

N_META = 16
GRID_W = 64
ROW_WIN = 8
COL_WIN = 16
RMS_EPS = 1e-6
SSM_CHUNK = 16
MASK_BIAS = -1e30

VMEM_LIMIT_BYTES = 56 * 1024 * 1024
F32_SUBLANES = 8
BF16_SUBLANES = 16

ROW_TILES = (1024, 512, 256, 128, 64)
COL_TILES = (512, 256, 128)


def _pick(dim, prefs):
    for t in prefs:
        if dim % t == 0:
            return t
    return dim


def _params(*sem):
    return pltpu.CompilerParams(dimension_semantics=sem, vmem_limit_bytes=VMEM_LIMIT_BYTES)


def _rms(x, g):
    ms = jnp.mean(x * x, axis=-1, keepdims=True)
    return x * lax.rsqrt(ms + RMS_EPS) * g


def _gelu(x):
    c = math.sqrt(2.0 / math.pi)
    return 0.5 * x * (1.0 + jnp.tanh(c * (x + 0.044715 * (x * x * x))))


def _sigmoid(x):
    return 1.0 / (1.0 + jnp.exp(-x))


def _norm_matmul_kernel(x_ref, g_ref, w_ref, o_ref, hn_ref):
    @pl.when(pl.program_id(1) == 0)
    def _():
        hn_ref[...] = _rms(x_ref[...], g_ref[...]).astype(BF16)

    o_ref[...] = jnp.dot(hn_ref[...], w_ref[...], preferred_element_type=F32).astype(o_ref.dtype)


def norm_matmul(x, g, w):
    rows, d = x.shape
    n = w.shape[1]
    tm = _pick(rows, ROW_TILES)
    tn = _pick(n, COL_TILES)
    return pl.pallas_call(
        _norm_matmul_kernel,
        grid=(rows // tm, n // tn),
        in_specs=[
            pl.BlockSpec((tm, d), lambda i, j: (i, 0)),
            pl.BlockSpec((1, d), lambda i, j: (0, 0)),
            pl.BlockSpec((d, tn), lambda i, j: (0, j)),
        ],
        out_specs=pl.BlockSpec((tm, tn), lambda i, j: (i, j)),
        out_shape=jax.ShapeDtypeStruct((rows, n), BF16),
        scratch_shapes=[pltpu.VMEM((tm, d), BF16)],
        compiler_params=_params("parallel", "arbitrary"),
        name="norm_in_proj",
    )(x, g.reshape(1, d), w)


def _na_kernel(q_ref, k_ref, v_ref, qm_ref, km_ref, vm_ref, bias_ref, o_ref, om_ref, *, rows, dh):
    lane = lax.broadcasted_iota(jnp.int32, (1, 2 * dh), 1)
    head_masks = (lane < dh, lane >= dh)
    scale = dh ** -0.5
    km = km_ref[...]
    vm = vm_ref[...]
    nt = (((1,), (1,)), ((), ()))
    kwin = ROW_WIN * GRID_W

    def attend(qs, kk, vv, bias_of_head):
        out = None
        for hh in range(2):
            qh = jnp.where(head_masks[hh], qs, jnp.zeros_like(qs))
            s_m = lax.dot_general(qh, km, nt, preferred_element_type=F32)
            m = jnp.max(s_m, axis=-1, keepdims=True)
            if kk is not None:
                s = lax.dot_general(qh, kk, nt, preferred_element_type=F32) + bias_of_head(hh)
                m = jnp.maximum(m, jnp.max(s, axis=-1, keepdims=True))
                p = jnp.exp(s - m)
            p_m = jnp.exp(s_m - m)
            den = jnp.sum(p_m, axis=-1, keepdims=True)
            acc = jnp.dot(p_m.astype(BF16), vm, preferred_element_type=F32)
            if kk is not None:
                den = den + jnp.sum(p, axis=-1, keepdims=True)
                acc = acc + jnp.dot(p.astype(BF16), vv, preferred_element_type=F32)
            o_h = acc / den
            out = o_h if out is None else jnp.where(head_masks[1], o_h, out)
        return out

    qm = qm_ref[...] * scale
    om_ref[...] = attend(qm.astype(BF16), None, None, None).astype(om_ref.dtype)

    def body(r, carry):
        r0 = jnp.clip(r - ROW_WIN // 2, 0, rows - ROW_WIN)
        d = r0 - r + (ROW_WIN - 1)
        qs = (q_ref[pl.ds(pl.multiple_of(r * GRID_W, GRID_W), GRID_W), :] * scale).astype(BF16)
        kstart = pl.multiple_of(r0 * GRID_W, GRID_W)
        kk = k_ref[pl.ds(kstart, kwin), :]
        vv = v_ref[pl.ds(kstart, kwin), :]
        o = attend(qs, kk, vv, lambda hh: bias_ref[hh, d])
        o_ref[pl.ds(pl.multiple_of(r * GRID_W, GRID_W), GRID_W), :] = o.astype(o_ref.dtype)
        return carry

    lax.fori_loop(0, rows, body, 0)


def _na_bias_table(rpb, rows):
    kr = min(ROW_WIN, rows)
    d = jnp.arange(ROW_WIN)[:, None]
    i = jnp.arange(kr)[None, :]
    dr = jnp.clip(d + i, 0, 2 * ROW_WIN - 2)
    c = jnp.arange(GRID_W)[:, None]
    kc = jnp.arange(GRID_W)[None, :]
    col_start = jnp.clip(c - COL_WIN // 2, 0, GRID_W - COL_WIN)
    valid = (kc >= col_start) & (kc < col_start + COL_WIN)
    dc = jnp.clip(kc - c + (COL_WIN - 1), 0, 2 * COL_WIN - 2)
    tab = rpb[:, dr[:, :, None, None], dc[None, None]]
    tab = jnp.where(valid[None, None, None], tab.astype(F32), MASK_BIAS)
    tab = jnp.transpose(tab, (0, 1, 3, 2, 4))
    return tab.reshape(rpb.shape[0], ROW_WIN, GRID_W, kr * GRID_W)


def neighbourhood_attention(proj_main, proj_meta, rpb, batch, seq, na_width):
    heads = rpb.shape[0]
    dh = na_width // heads
    rows = seq // GRID_W
    assert rows >= ROW_WIN and heads % 2 == 0 and 2 * dh == 128
    pairs = heads // 2
    bias = _na_bias_table(rpb, rows)
    blk = (seq, 2 * dh)
    mblk = (N_META, 2 * dh)
    kernel = functools.partial(_na_kernel, rows=rows, dh=dh)
    return pl.pallas_call(
        kernel,
        grid=(batch, pairs),
        in_specs=[
            pl.BlockSpec(blk, lambda b, h: (b, h)),
            pl.BlockSpec(blk, lambda b, h: (b, pairs + h)),
            pl.BlockSpec(blk, lambda b, h: (b, 2 * pairs + h)),
            pl.BlockSpec(mblk, lambda b, h: (b, h)),
            pl.BlockSpec(mblk, lambda b, h: (b, pairs + h)),
            pl.BlockSpec(mblk, lambda b, h: (b, 2 * pairs + h)),
            pl.BlockSpec((2, ROW_WIN, GRID_W, ROW_WIN * GRID_W), lambda b, h: (h, 0, 0, 0)),
        ],
        out_specs=[
            pl.BlockSpec(blk, lambda b, h: (b, h)),
            pl.BlockSpec(mblk, lambda b, h: (b, h)),
        ],
        out_shape=[
            jax.ShapeDtypeStruct((batch * seq, na_width), BF16),
            jax.ShapeDtypeStruct((batch * N_META, na_width), BF16),
        ],
        compiler_params=_params("parallel", "parallel"),
        name="neighbourhood_attention",
    )(proj_main, proj_main, proj_main, proj_meta, proj_meta, proj_meta, bias)


def _ssm_operators(lam_re, lam_im, log_step, b_re, b_im, c_re, c_im, d_skip):
    t = SSM_CHUNK
    lam = lax.complex(lam_re.astype(F32), lam_im.astype(F32))
    step = jnp.exp(log_step.astype(F32))[..., None]
    lam_dt = lam * step
    lam_bar = jnp.exp(lam_dt)
    b_bar = ((lam_bar - 1.0) / lam)[..., None] * lax.complex(b_re.astype(F32), b_im.astype(F32))
    c_mat = lax.complex(c_re.astype(F32), c_im.astype(F32))
    g, n = lam.shape[1], lam.shape[2]
    c = b_re.shape[-1]
    k_idx = jnp.arange(t + 1, dtype=F32)
    pows = jnp.exp(lam_dt[:, :, None, :] * k_idx[None, None, :, None])

    kern = jnp.real(jnp.einsum('dgcn,dgln,dgnk->dglck', c_mat, pows[:, :, :t], b_bar))
    s_i = jnp.arange(t)[:, None]
    t_i = jnp.arange(t)[None, :]
    lag_f = jnp.clip(t_i - s_i, 0, t - 1)
    lag_r = jnp.clip(s_i - t_i, 0, t - 1)
    m_f = jnp.where((t_i >= s_i)[None, :, :, None, None], kern[0][:, lag_f], 0.0)
    m_r = jnp.where((s_i >= t_i)[None, :, :, None, None], kern[1][:, lag_r], 0.0)
    skip = (jnp.eye(t, dtype=F32)[None, :, :, None, None]
            * (jnp.eye(c, dtype=F32)[None] * d_skip.astype(F32).reshape(g, c, 1))[:, None, None])
    m = jnp.transpose(m_f + m_r + skip, (0, 1, 4, 2, 3)).reshape(g, t * c, t * c)

    e_f = pows[0][:, ::-1][:, 1:, :, None] * b_bar[0][:, None]
    e_r = pows[1][:, :t, :, None] * b_bar[1][:, None]
    e_f = jnp.transpose(e_f, (0, 1, 3, 2)).reshape(g, t * c, n)
    e_r = jnp.transpose(e_r, (0, 1, 3, 2)).reshape(g, t * c, n)
    e = jnp.concatenate([jnp.real(e_f), jnp.real(e_r), jnp.imag(e_f), jnp.imag(e_r)], axis=-1)

    g_f = c_mat[0][:, None] * pows[0][:, 1:, None, :]
    g_r = c_mat[1][:, None] * pows[1][:, ::-1][:, :t, None, :]
    g_f = jnp.transpose(g_f, (0, 3, 1, 2)).reshape(g, n, t * c)
    g_r = jnp.transpose(g_r, (0, 3, 1, 2)).reshape(g, n, t * c)
    f = jnp.concatenate([jnp.real(g_f), jnp.real(g_r), -jnp.imag(g_f), -jnp.imag(g_r)], axis=1)

    a1 = jnp.concatenate([pows[0][:, t], pows[1][:, t]], axis=-1)
    a2 = a1 * a1
    half = F32_SUBLANES // 2
    rows_lo = (jnp.arange(F32_SUBLANES) < half)[None, :, None]
    a_fwd = jnp.where(rows_lo, a1[:, None], a2[:, None])
    a_rev = jnp.where(rows_lo, a2[:, None], a1[:, None])
    a_one = jnp.broadcast_to(a1[:, None], a_fwd.shape)
    a = jnp.stack([jnp.real(a_one), jnp.imag(a_one), jnp.real(a_fwd), jnp.imag(a_fwd),
                   jnp.real(a_rev), jnp.imag(a_rev)], axis=1)
    return m.astype(BF16), e.astype(BF16), f.astype(BF16), a.astype(F32)


def _ssm_kernel(v_ref, m_ref, e_ref, f_ref, a_ref, y_ref, ee_ref, pf_ref, pr_ref, *, groups, tiles, nstate):
    sub = F32_SUBLANES
    half = sub // 2
    w = 2 * nstate
    row = lax.broadcasted_iota(jnp.int32, (sub, w), 0)
    lo = row < half

    for gi in range(groups):
        ee_ref[gi] = jnp.dot(v_ref[gi], e_ref[gi], preferred_element_type=F32)

    def cmul(ar, ai, xr, xi):
        return ar * xr - ai * xi, ar * xi + ai * xr

    def swap(x):
        return pltpu.roll(x, half, 0)

    def body(j, carry):
        new = []
        for gi in range(groups):
            sr, si, rr, ri = carry[gi]
            a1r, a1i = a_ref[gi, 0], a_ref[gi, 1]
            fs = pl.ds(pl.multiple_of(j * sub, sub), sub)
            er, ei = ee_ref[gi, fs, 0:w], ee_ref[gi, fs, w:2 * w]
            aer, aei = cmul(a1r, a1i, er, ei)
            tr = er + jnp.where(lo, 0.0, swap(aer))
            ti = ei + jnp.where(lo, 0.0, swap(aei))
            lr = jnp.where(lo, swap(sr), sr)
            li = jnp.where(lo, swap(si), si)
            dr_, di_ = cmul(a_ref[gi, 2], a_ref[gi, 3], lr, li)
            nsr, nsi = tr + dr_, ti + di_
            pf_ref[gi, fs, 0:w] = swap(jnp.where(lo, nsr, sr))
            pf_ref[gi, fs, w:2 * w] = swap(jnp.where(lo, nsi, si))
            rs = pl.ds(pl.multiple_of((tiles - 1 - j) * sub, sub), sub)
            er, ei = ee_ref[gi, rs, 0:w], ee_ref[gi, rs, w:2 * w]
            aer, aei = cmul(a1r, a1i, er, ei)
            tr = er + jnp.where(lo, swap(aer), 0.0)
            ti = ei + jnp.where(lo, swap(aei), 0.0)
            fr = jnp.where(lo, rr, swap(rr))
            fi = jnp.where(lo, ri, swap(ri))
            dr_, di_ = cmul(a_ref[gi, 4], a_ref[gi, 5], fr, fi)
            nrr, nri = tr + dr_, ti + di_
            pr_ref[gi, rs, 0:w] = swap(jnp.where(lo, rr, nrr))
            pr_ref[gi, rs, w:2 * w] = swap(jnp.where(lo, ri, nri))
            new.append((nsr, nsi, nrr, nri))
        return tuple(new)

    zero = jnp.zeros((sub, w), F32)
    lax.fori_loop(0, tiles, body, tuple((zero, zero, zero, zero) for _ in range(groups)))

    lane = lax.broadcasted_iota(jnp.int32, (1, 2 * w), 1)
    fwd_lane = (lane % w) < nstate
    for gi in range(groups):
        prev = jnp.where(fwd_lane, pf_ref[gi], pr_ref[gi]).astype(BF16)
        y = jnp.dot(v_ref[gi], m_ref[gi], preferred_element_type=F32)
        y = y + jnp.dot(prev, f_ref[gi], preferred_element_type=F32)
        y_ref[gi] = y.astype(y_ref.dtype)


def s5_scan(u_main, u_meta, ops, batch, seq):
    m, e, f, a = ops
    g = m.shape[0]
    t = SSM_CHUNK
    width = u_main.shape[1]
    c = width // g
    nstate = e.shape[2] // 4
    assert batch * 2 == F32_SUBLANES and t * c == 256 and 4 * nstate == 256
    assert N_META % t == 0 and seq % t == 0
    chunks = (N_META + seq) // t
    tile_chunks = 2 * BF16_SUBLANES // F32_SUBLANES
    chunks_p = -(-chunks // tile_chunks) * tile_chunks
    rows = chunks_p * batch
    tiles = rows // F32_SUBLANES

    useq = jnp.concatenate([u_meta.reshape(batch, N_META, width), u_main.reshape(batch, seq, width)], axis=1)
    useq = useq.reshape(batch, chunks, t, g, c)
    useq = jnp.pad(useq, ((0, 0), (0, chunks_p - chunks), (0, 0), (0, 0), (0, 0)))
    v = jnp.transpose(useq, (3, 1, 0, 2, 4)).reshape(g, rows, t * c)

    gb = _pick(g, (4, 2, 1))
    kernel = functools.partial(_ssm_kernel, groups=gb, tiles=tiles, nstate=nstate)
    op_spec = pl.BlockSpec((gb, t * c, t * c), lambda i: (i, 0, 0))
    y = pl.pallas_call(
        kernel,
        grid=(g // gb,),
        in_specs=[
            pl.BlockSpec((gb, rows, t * c), lambda i: (i, 0, 0)),
            op_spec, op_spec, op_spec,
            pl.BlockSpec((gb, 6, F32_SUBLANES, 2 * nstate), lambda i: (i, 0, 0, 0)),
        ],
        out_specs=pl.BlockSpec((gb, rows, t * c), lambda i: (i, 0, 0)),
        out_shape=jax.ShapeDtypeStruct((g, rows, t * c), BF16),
        scratch_shapes=[pltpu.VMEM((gb, rows, 4 * nstate), F32)] * 3,
        compiler_params=_params("parallel"),
        name="s5_scan",
    )(v, m, e, f, a)

    y = y.reshape(g, chunks_p, batch, t, c)[:, :chunks]
    y = jnp.transpose(y, (2, 1, 3, 0, 4)).reshape(batch, N_META + seq, width)
    return y[:, N_META:].reshape(batch * seq, width), y[:, :N_META].reshape(batch * N_META, width)


def _mix_kernel(o_ref, y_ref, gna_ref, gssm_ref, wna_ref, wglu_ref, wssm_ref, out_ref):
    y_na = jnp.dot(o_ref[...], wna_ref[...], preferred_element_type=F32)
    gl = _gelu(y_ref[...].astype(F32))
    z = jnp.dot(gl.astype(BF16), wglu_ref[...], preferred_element_type=F32)
    t = (gl * _sigmoid(z)).astype(BF16)
    y_ssm = jnp.dot(t, wssm_ref[...], preferred_element_type=F32)
    mixed = _sigmoid(gna_ref[...].astype(F32)) * y_na + _sigmoid(gssm_ref[...].astype(F32)) * y_ssm
    out_ref[...] = mixed.astype(out_ref.dtype)


def mix_branches(o_na, y, proj, w_na, w_glu, w_ssm, gate_col):
    rows, na_width = o_na.shape
    ssm_width = y.shape[1]
    d = w_na.shape[1]
    tm = _pick(rows, (512, 256, 128, 64))
    assert gate_col % d == 0
    gblk = gate_col // d
    resident = dict(pipeline_mode=pl.Buffered(1))
    return pl.pallas_call(
        _mix_kernel,
        grid=(rows // tm,),
        in_specs=[
            pl.BlockSpec((tm, na_width), lambda i: (i, 0)),
            pl.BlockSpec((tm, ssm_width), lambda i: (i, 0)),
            pl.BlockSpec((tm, d), lambda i: (i, gblk)),
            pl.BlockSpec((tm, d), lambda i: (i, gblk + 1)),
            pl.BlockSpec(w_na.shape, lambda i: (0, 0), **resident),
            pl.BlockSpec(w_glu.shape, lambda i: (0, 0), **resident),
            pl.BlockSpec(w_ssm.shape, lambda i: (0, 0), **resident),
        ],
        out_specs=pl.BlockSpec((tm, d), lambda i: (i, 0)),
        out_shape=jax.ShapeDtypeStruct((rows, d), BF16),
        compiler_params=_params("parallel"),
        name="mix_branches",
    )(o_na, y, proj, proj, w_na, w_glu, w_ssm)


def _residual_matmul_kernel(h_ref, a_ref, w_ref, o_ref):
    o_ref[...] = h_ref[...] + jnp.dot(a_ref[...], w_ref[...], preferred_element_type=F32)


def residual_matmul(h, a, w):
    rows, d = h.shape
    tm = _pick(rows, (512, 256, 128, 64))
    return pl.pallas_call(
        _residual_matmul_kernel,
        grid=(rows // tm,),
        in_specs=[
            pl.BlockSpec((tm, d), lambda i: (i, 0)),
            pl.BlockSpec((tm, a.shape[1]), lambda i: (i, 0)),
            pl.BlockSpec(w.shape, lambda i: (0, 0), pipeline_mode=pl.Buffered(1)),
        ],
        out_specs=pl.BlockSpec((tm, d), lambda i: (i, 0)),
        out_shape=jax.ShapeDtypeStruct((rows, d), F32),
        compiler_params=_params("parallel"),
        name="residual_out_proj",
    )(h, a, w)


HALO = BF16_SUBLANES


def _ffn_up_kernel(h_ref, prev_ref, next_ref, g_ref, wa_ref, wg_ref, cw_ref, cb_ref, o_ref, hn_ref, *, tm):
    @pl.when(pl.program_id(1) == 0)
    def _():
        g = g_ref[...]
        hn_ref[0:HALO, :] = _rms(prev_ref[...], g).astype(BF16)
        hn_ref[HALO:HALO + tm, :] = _rms(h_ref[...], g).astype(BF16)
        hn_ref[HALO + tm:, :] = _rms(next_ref[...], g).astype(BF16)

    ext = tm + 2 * HALO
    a = jnp.dot(hn_ref[...], wa_ref[...], preferred_element_type=F32)
    gate = jnp.dot(hn_ref[HALO:HALO + tm, :], wg_ref[...], preferred_element_type=F32)
    a_prev = pltpu.roll(a, 1, 0)[HALO:HALO + tm]
    a_next = pltpu.roll(a, ext - 1, 0)[HALO:HALO + tm]
    conv = a_prev * cw_ref[0:1, :] + a[HALO:HALO + tm] * cw_ref[1:2, :] + a_next * cw_ref[2:3, :] + cb_ref[...]
    o_ref[...] = (_gelu(conv) * gate).astype(o_ref.dtype)


def ffn_up(h, halo_prev, halo_next, g, w_up, conv_w, conv_b, tm):
    rows, d = h.shape
    dff = conv_b.shape[0]
    tn = _pick(dff, COL_TILES)
    nj = dff // tn
    kernel = functools.partial(_ffn_up_kernel, tm=tm)
    return pl.pallas_call(
        kernel,
        grid=(rows // tm, nj),
        in_specs=[
            pl.BlockSpec((tm, d), lambda i, j: (i, 0)),
            pl.BlockSpec((HALO, d), lambda i, j: (i, 0)),
            pl.BlockSpec((HALO, d), lambda i, j: (i, 0)),
            pl.BlockSpec((1, d), lambda i, j: (0, 0)),
            pl.BlockSpec((d, tn), lambda i, j: (0, j)),
            pl.BlockSpec((d, tn), lambda i, j: (0, nj + j)),
            pl.BlockSpec((conv_w.shape[0], tn), lambda i, j: (0, j)),
            pl.BlockSpec((1, tn), lambda i, j: (0, j)),
        ],
        out_specs=pl.BlockSpec((tm, tn), lambda i, j: (i, j)),
        out_shape=jax.ShapeDtypeStruct((rows, dff), BF16),
        scratch_shapes=[pltpu.VMEM((tm + 2 * HALO, d), BF16)],
        compiler_params=_params("parallel", "arbitrary"),
        name="ffn_up_conv_gate",
    )(h, halo_prev, halo_next, g.reshape(1, d), w_up, w_up, conv_w, conv_b.reshape(1, dff))


def _ffn_down_kernel(h_ref, a_ref, w_ref, g_ref, o_ref, acc_ref):
    k = pl.program_id(1)

    @pl.when(k == 0)
    def _():
        acc_ref[...] = h_ref[...]

    acc_ref[...] += jnp.dot(a_ref[...], w_ref[...], preferred_element_type=F32)

    @pl.when(k == pl.num_programs(1) - 1)
    def _():
        o_ref[...] = _rms(acc_ref[...], g_ref[...])


def ffn_down(h, act, w_down, g):
    rows, d = h.shape
    dff = act.shape[1]
    tm = _pick(rows, (512, 256, 128, 64))
    tk = _pick(dff, COL_TILES)
    return pl.pallas_call(
        _ffn_down_kernel,
        grid=(rows // tm, dff // tk),
        in_specs=[
            pl.BlockSpec((tm, d), lambda i, k: (i, 0)),
            pl.BlockSpec((tm, tk), lambda i, k: (i, k)),
            pl.BlockSpec((tk, d), lambda i, k: (k, 0)),
            pl.BlockSpec((1, d), lambda i, k: (0, 0)),
        ],
        out_specs=pl.BlockSpec((tm, d), lambda i, k: (i, 0)),
        out_shape=jax.ShapeDtypeStruct((rows, d), F32),
        scratch_shapes=[pltpu.VMEM((tm, d), F32)],
        compiler_params=_params("parallel", "arbitrary"),
        name="ffn_down_final_norm",
    )(h, act, w_down, g.reshape(1, d))


def _conv_halos(h1_main, h1_meta, batch, seq, tm):
    d = h1_main.shape[1]
    per_seq = seq // tm
    hm = h1_main.reshape(batch, per_seq, tm, d)
    meta_tail = h1_meta.reshape(batch, 1, N_META, d)[:, :, N_META - HALO:]
    prev = jnp.concatenate([meta_tail, hm[:, :-1, tm - HALO:]], axis=1)
    nxt = jnp.concatenate([hm[:, 1:, :HALO], jnp.zeros((batch, 1, HALO, d), h1_main.dtype)], axis=1)
    return prev.reshape(batch * per_seq * HALO, d), nxt.reshape(batch * per_seq * HALO, d)


def kernel(x, meta_tokens, norm1_g, w_in, na_rpb, ssm_lam_re, ssm_lam_im, ssm_log_step, ssm_b_re, ssm_b_im,
           ssm_c_re, ssm_c_im, ssm_d, w_glu, w_proj_na, w_proj_ssm, w_out, norm2_g, w_up, conv_w, conv_b,
           w_down, final_g):
    batch, seq, d = x.shape
    depth = w_in.shape[0]
    na_width = w_proj_na.shape[1]
    ssm_width = w_proj_ssm.shape[1]
    assert depth == 1 and N_META >= HALO
    l = 0
    h_main = x.reshape(batch * seq, d)
    h_meta = jnp.broadcast_to(meta_tokens.astype(x.dtype)[None], (batch, N_META, d)).reshape(batch * N_META, d)

    w_in_b = w_in[l].astype(BF16)
    proj_main = norm_matmul(h_main, norm1_g[l], w_in_b)
    proj_meta = norm_matmul(h_meta, norm1_g[l], w_in_b)

    o_main, o_meta = neighbourhood_attention(proj_main, proj_meta, na_rpb[l], batch, seq, na_width)

    u_col = 3 * na_width
    ops = _ssm_operators(ssm_lam_re[l], ssm_lam_im[l], ssm_log_step[l], ssm_b_re[l], ssm_b_im[l],
                         ssm_c_re[l], ssm_c_im[l], ssm_d[l])
    y_main, y_meta = s5_scan(proj_main[:, u_col:u_col + ssm_width], proj_meta[:, u_col:u_col + ssm_width],
                             ops, batch, seq)

    w_na_b, w_glu_b, w_ssm_b = w_proj_na[l].astype(BF16), w_glu[l].astype(BF16), w_proj_ssm[l].astype(BF16)
    w_out_b = w_out[l].astype(BF16)
    gate_col = u_col + ssm_width
    h1 = []
    for h, o, y, proj in ((h_main, o_main, y_main, proj_main), (h_meta, o_meta, y_meta, proj_meta)):
        mixed = mix_branches(o, y, proj, w_na_b, w_glu_b, w_ssm_b, gate_col)
        h1.append(residual_matmul(h, mixed, w_out_b))
    h1_main, h1_meta = h1

    tm = _pick(seq, ROW_TILES)
    halo_prev, halo_next = _conv_halos(h1_main, h1_meta, batch, seq, tm)
    act = ffn_up(h1_main, halo_prev, halo_next, norm2_g[l], w_up[l].astype(BF16), conv_w[l], conv_b[l], tm)
    out = ffn_down(h1_main, act, w_down[l].astype(BF16), final_g)
    return out.reshape(batch, seq, d)
```

```python
import functools
import math

import jax
import jax.numpy as jnp
from jax import lax
from jax.experimental import pallas as pl
from jax.experimental.pallas import tpu as pltpu

F32 = jnp.float32
BF16 = jnp.bfloat16

N_META = 16
GRID_W = 64
ROW_WIN = 8
COL_WIN = 16
RMS_EPS = 1e-6
SSM_CHUNK = 16
MASK_BIAS = -1e30
NA_UNROLL = 4

VMEM_LIMIT_BYTES = 56 * 1024 * 1024
F32_SUBLANES = 8
BF16_SUBLANES = 16

ROW_TILES = (1024, 512, 256, 128, 64)
COL_TILES = (512, 256, 128)


def _pick(dim, prefs):
    for t in prefs:
        if dim % t == 0:
            return t
    return dim


def _params(*sem):
    return pltpu.CompilerParams(dimension_semantics=sem, vmem_limit_bytes=VMEM_LIMIT_BYTES)


def _rms(x, g):
    ms = jnp.mean(x * x, axis=-1, keepdims=True)
    return x * lax.rsqrt(ms + RMS_EPS) * g


def _gelu(x):
    c = math.sqrt(2.0 / math.pi)
    return 0.5 * x * (1.0 + jnp.tanh(c * (x + 0.044715 * (x * x * x))))


def _sigmoid(x):
    return 1.0 / (1.0 + jnp.exp(-x))


def _norm_matmul_kernel(x_ref, g_ref, w_ref, o_ref, hn_ref):
    @pl.when(pl.program_id(1) == 0)
    def _():
        hn_ref[...] = _rms(x_ref[...], g_ref[...]).astype(BF16)

    o_ref[...] = jnp.dot(hn_ref[...], w_ref[...], preferred_element_type=F32).astype(o_ref.dtype)


def norm_matmul(x, g, w):
    rows, d = x.shape
    n = w.shape[1]
    tm = _pick(rows, ROW_TILES)
    tn = _pick(n, COL_TILES)
    return pl.pallas_call(
        _norm_matmul_kernel,
        grid=(rows // tm, n // tn),
        in_specs=[
            pl.BlockSpec((tm, d), lambda i, j: (i, 0)),
            pl.BlockSpec((1, d), lambda i, j: (0, 0)),
            pl.BlockSpec((d, tn), lambda i, j: (0, j)),
        ],
        out_specs=pl.BlockSpec((tm, tn), lambda i, j: (i, j)),
        out_shape=jax.ShapeDtypeStruct((rows, n), BF16),
        scratch_shapes=[pltpu.VMEM((tm, d), BF16)],
        compiler_params=_params("parallel", "arbitrary"),
        name="norm_in_proj",
    )(x, g.reshape(1, d), w)


def _na_kernel(q_ref, k_ref, v_ref, qm_ref, km_ref, vm_ref, tab_ref, o_ref, om_ref, bias_ref, sm_ref, acc_ref,
               *, rows, dh):
    lane = lax.broadcasted_iota(jnp.int32, (1, 2 * dh), 1)
    head_masks = (lane < dh, lane >= dh)
    scale = dh ** -0.5
    km = km_ref[...]
    vm = vm_ref[...]
    nt = (((1,), (1,)), ((), ()))
    kwin = ROW_WIN * GRID_W

    first_row = lane < GRID_W
    for hh in range(2):
        for d in range(ROW_WIN):
            for i2 in range(ROW_WIN // 2):
                bias_ref[d, hh * GRID_W:(hh + 1) * GRID_W, i2 * 2 * GRID_W:(i2 + 1) * 2 * GRID_W] = jnp.where(
                    first_row, tab_ref[hh, d + 2 * i2], tab_ref[hh, d + 2 * i2 + 1])

    def one_head(x, hh):
        return jnp.where(head_masks[hh], x, jnp.zeros_like(x))

    qm = (qm_ref[...] * scale).astype(BF16)
    om = None
    for hh in range(2):
        s = lax.dot_general(one_head(qm, hh), km, nt, preferred_element_type=F32)
        p = jnp.exp(s - jnp.max(s, axis=-1, keepdims=True))
        o_h = jnp.dot(p.astype(BF16), vm, preferred_element_type=F32) / jnp.sum(p, axis=-1, keepdims=True)
        om = o_h if om is None else jnp.where(head_masks[1], o_h, om)
    om_ref[...] = om.astype(om_ref.dtype)

    q_all = (q_ref[...] * scale).astype(BF16)
    for hh in range(2):
        sm_ref[hh] = lax.dot_general(one_head(q_all, hh), km, nt, preferred_element_type=F32)

    def body(r, carry):
        r0 = jnp.clip(r - ROW_WIN // 2, 0, rows - ROW_WIN)
        d = r0 - r + (ROW_WIN - 1)
        qrows = pl.ds(pl.multiple_of(r * GRID_W, GRID_W), GRID_W)
        qs = (q_ref[qrows, :] * scale).astype(BF16)
        q2 = jnp.concatenate([one_head(qs, 0), one_head(qs, 1)], axis=0)
        kstart = pl.multiple_of(r0 * GRID_W, GRID_W)
        s = lax.dot_general(q2, k_ref[pl.ds(kstart, kwin), :], nt, preferred_element_type=F32) + bias_ref[d]
        s_m = jnp.concatenate([sm_ref[0, qrows, :], sm_ref[1, qrows, :]], axis=0)
        m = jnp.maximum(jnp.max(s, axis=-1, keepdims=True), jnp.max(s_m, axis=-1, keepdims=True))
        p = jnp.exp(s - m)
        p_m = jnp.exp(s_m - m)
        inv = 1.0 / (jnp.sum(p, axis=-1, keepdims=True) + jnp.sum(p_m, axis=-1, keepdims=True))
        acc = jnp.dot(p.astype(BF16), v_ref[pl.ds(kstart, kwin), :], preferred_element_type=F32) * inv
        acc_ref[qrows, :] = jnp.where(head_masks[1], acc[GRID_W:], acc[:GRID_W])
        p_m = p_m * inv
        sm_ref[0, qrows, :] = p_m[:GRID_W]
        sm_ref[1, qrows, :] = p_m[GRID_W:]
        return carry

    lax.fori_loop(0, rows, body, 0, unroll=NA_UNROLL)

    o_meta = [jnp.dot(sm_ref[hh].astype(BF16), vm, preferred_element_type=F32) for hh in range(2)]
    o_ref[...] = (acc_ref[...] + jnp.where(head_masks[1], o_meta[1], o_meta[0])).astype(o_ref.dtype)


def _na_bias_table(rpb):
    c = jnp.arange(GRID_W)[:, None]
    kc = jnp.arange(GRID_W)[None, :]
    col_start = jnp.clip(c - COL_WIN // 2, 0, GRID_W - COL_WIN)
    valid = (kc >= col_start) & (kc < col_start + COL_WIN)
    dc = kc - c + (COL_WIN - 1)
    pick = dc[None] == jnp.arange(2 * COL_WIN - 1)[:, None, None]
    tab = jnp.sum(jnp.where(pick, rpb.astype(F32)[..., None, None], 0.0), axis=2)
    tab = jnp.where(valid, tab, MASK_BIAS)
    return jnp.concatenate([tab, tab], axis=-1)


def neighbourhood_attention(proj_main, proj_meta, rpb, batch, seq, na_width):
    heads = rpb.shape[0]
    dh = na_width // heads
    rows = seq // GRID_W
    assert rows >= ROW_WIN and heads % 2 == 0 and 2 * dh == 128
    pairs = heads // 2
    bias = _na_bias_table(rpb)
    blk = (seq, 2 * dh)
    mblk = (N_META, 2 * dh)
    kernel = functools.partial(_na_kernel, rows=rows, dh=dh)
    return pl.pallas_call(
        kernel,
        grid=(batch, pairs),
        in_specs=[
            pl.BlockSpec(blk, lambda b, h: (b, h)),
            pl.BlockSpec(blk, lambda b, h: (b, pairs + h)),
            pl.BlockSpec(blk, lambda b, h: (b, 2 * pairs + h)),
            pl.BlockSpec(mblk, lambda b, h: (b, h)),
            pl.BlockSpec(mblk, lambda b, h: (b, pairs + h)),
            pl.BlockSpec(mblk, lambda b, h: (b, 2 * pairs + h)),
            pl.BlockSpec((2, 2 * ROW_WIN - 1, GRID_W, 2 * GRID_W), lambda b, h: (h, 0, 0, 0)),
        ],
        out_specs=[
            pl.BlockSpec(blk, lambda b, h: (b, h)),
            pl.BlockSpec(mblk, lambda b, h: (b, h)),
        ],
        out_shape=[
            jax.ShapeDtypeStruct((batch * seq, na_width), BF16),
            jax.ShapeDtypeStruct((batch * N_META, na_width), BF16),
        ],
        scratch_shapes=[pltpu.VMEM((ROW_WIN, 2 * GRID_W, ROW_WIN * GRID_W), F32),
                        pltpu.VMEM((2, seq, N_META), F32),
                        pltpu.VMEM((seq, 2 * dh), F32)],
        compiler_params=_params("parallel", "parallel"),
        name="neighbourhood_attention",
    )(proj_main, proj_main, proj_main, proj_meta, proj_meta, proj_meta, bias)


def _ssm_operators(lam_re, lam_im, log_step, b_re, b_im, c_re, c_im, d_skip):
    t = SSM_CHUNK
    lam = lax.complex(lam_re.astype(F32), lam_im.astype(F32))
    step = jnp.exp(log_step.astype(F32))[..., None]
    lam_dt = lam * step
    lam_bar = jnp.exp(lam_dt)
    b_bar = ((lam_bar - 1.0) / lam)[..., None] * lax.complex(b_re.astype(F32), b_im.astype(F32))
    c_mat = lax.complex(c_re.astype(F32), c_im.astype(F32))
    g, n = lam.shape[1], lam.shape[2]
    c = b_re.shape[-1]
    k_idx = jnp.arange(t + 1, dtype=F32)
    pows = jnp.exp(lam_dt[:, :, None, :] * k_idx[None, None, :, None])

    kern = jnp.real(jnp.einsum('dgcn,dgln,dgnk->dglck', c_mat, pows[:, :, :t], b_bar))
    s_i = jnp.arange(t)[:, None]
    t_i = jnp.arange(t)[None, :]
    lag_f = jnp.clip(t_i - s_i, 0, t - 1)
    lag_r = jnp.clip(s_i - t_i, 0, t - 1)
    m_f = jnp.where((t_i >= s_i)[None, :, :, None, None], kern[0][:, lag_f], 0.0)
    m_r = jnp.where((s_i >= t_i)[None, :, :, None, None], kern[1][:, lag_r], 0.0)
    skip = (jnp.eye(t, dtype=F32)[None, :, :, None, None]
            * (jnp.eye(c, dtype=F32)[None] * d_skip.astype(F32).reshape(g, c, 1))[:, None, None])
    m = jnp.transpose(m_f + m_r + skip, (0, 1, 4, 2, 3)).reshape(g, t * c, t * c)

    e_f = pows[0][:, ::-1][:, 1:, :, None] * b_bar[0][:, None]
    e_r = pows[1][:, :t, :, None] * b_bar[1][:, None]
    e_f = jnp.transpose(e_f, (0, 1, 3, 2)).reshape(g, t * c, n)
    e_r = jnp.transpose(e_r, (0, 1, 3, 2)).reshape(g, t * c, n)
    e = jnp.concatenate([jnp.real(e_f), jnp.real(e_r), jnp.imag(e_f), jnp.imag(e_r)], axis=-1)

    g_f = c_mat[0][:, None] * pows[0][:, 1:, None, :]
    g_r = c_mat[1][:, None] * pows[1][:, ::-1][:, :t, None, :]
    g_f = jnp.transpose(g_f, (0, 3, 1, 2)).reshape(g, n, t * c)
    g_r = jnp.transpose(g_r, (0, 3, 1, 2)).reshape(g, n, t * c)
    f = jnp.concatenate([jnp.real(g_f), jnp.real(g_r), -jnp.imag(g_f), -jnp.imag(g_r)], axis=1)

    a1 = jnp.concatenate([pows[0][:, t], pows[1][:, t]], axis=-1)
    a2 = a1 * a1
    half = F32_SUBLANES // 2
    rows_lo = (jnp.arange(F32_SUBLANES) < half)[None, :, None]
    a_fwd = jnp.where(rows_lo, a1[:, None], a2[:, None])
    a_rev = jnp.where(rows_lo, a2[:, None], a1[:, None])
    a_one = jnp.broadcast_to(a1[:, None], a_fwd.shape)
    a = jnp.stack([jnp.real(a_one), jnp.imag(a_one), jnp.real(a_fwd), jnp.imag(a_fwd),
                   jnp.real(a_rev), jnp.imag(a_rev)], axis=1)
    return m.astype(BF16), e.astype(BF16), f.astype(BF16), a.astype(F32)


def _ssm_kernel(v_ref, m_ref, e_ref, f_ref, a_ref, y_ref, ee_ref, pf_ref, pr_ref, *, groups, tiles, nstate):
    sub = F32_SUBLANES
    half = sub // 2
    w = 2 * nstate
    row = lax.broadcasted_iota(jnp.int32, (sub, w), 0)
    lo = row < half

    for gi in range(groups):
        ee_ref[gi] = jnp.dot(v_ref[gi], e_ref[gi], preferred_element_type=F32)

    def cmul(ar, ai, xr, xi):
        return ar * xr - ai * xi, ar * xi + ai * xr

    def swap(x):
        return pltpu.roll(x, half, 0)

    def body(j, carry):
        new = []
        for gi in range(groups):
            sr, si, rr, ri = carry[gi]
            a1r, a1i = a_ref[gi, 0], a_ref[gi, 1]
            fs = pl.ds(pl.multiple_of(j * sub, sub), sub)
            er, ei = ee_ref[gi, fs, 0:w], ee_ref[gi, fs, w:2 * w]
            aer, aei = cmul(a1r, a1i, er, ei)
            tr = er + jnp.where(lo, 0.0, swap(aer))
            ti = ei + jnp.where(lo, 0.0, swap(aei))
            lr = jnp.where(lo, swap(sr), sr)
            li = jnp.where(lo, swap(si), si)
            dr_, di_ = cmul(a_ref[gi, 2], a_ref[gi, 3], lr, li)
            nsr, nsi = tr + dr_, ti + di_
            pf_ref[gi, fs, 0:w] = swap(jnp.where(lo, nsr, sr))
            pf_ref[gi, fs, w:2 * w] = swap(jnp.where(lo, nsi, si))
            rs = pl.ds(pl.multiple_of((tiles - 1 - j) * sub, sub), sub)
            er, ei = ee_ref[gi, rs, 0:w], ee_ref[gi, rs, w:2 * w]
            aer, aei = cmul(a1r, a1i, er, ei)
            tr = er + jnp.where(lo, swap(aer), 0.0)
            ti = ei + jnp.where(lo, swap(aei), 0.0)
            fr = jnp.where(lo, rr, swap(rr))
            fi = jnp.where(lo, ri, swap(ri))
            dr_, di_ = cmul(a_ref[gi, 4], a_ref[gi, 5], fr, fi)
            nrr, nri = tr + dr_, ti + di_
            pr_ref[gi, rs, 0:w] = swap(jnp.where(lo, rr, nrr))
            pr_ref[gi, rs, w:2 * w] = swap(jnp.where(lo, ri, nri))
            new.append((nsr, nsi, nrr, nri))
        return tuple(new)

    zero = jnp.zeros((sub, w), F32)
    lax.fori_loop(0, tiles, body, tuple((zero, zero, zero, zero) for _ in range(groups)))

    lane = lax.broadcasted_iota(jnp.int32, (1, 2 * w), 1)
    fwd_lane = (lane % w) < nstate
    for gi in range(groups):
        prev = jnp.where(fwd_lane, pf_ref[gi], pr_ref[gi]).astype(BF16)
        y = jnp.dot(v_ref[gi], m_ref[gi], preferred_element_type=F32)
        y = y + jnp.dot(prev, f_ref[gi], preferred_element_type=F32)
        y_ref[gi] = y.astype(y_ref.dtype)


def s5_scan(u_main, u_meta, ops, batch, seq):
    m, e, f, a = ops
    g = m.shape[0]
    t = SSM_CHUNK
    width = u_main.shape[1]
    c = width // g
    nstate = e.shape[2] // 4
    assert batch * 2 == F32_SUBLANES and t * c == 256 and 4 * nstate == 256
    assert N_META % t == 0 and seq % t == 0
    chunks = (N_META + seq) // t
    tile_chunks = 2 * BF16_SUBLANES // F32_SUBLANES
    chunks_p = -(-chunks // tile_chunks) * tile_chunks
    rows = chunks_p * batch
    tiles = rows // F32_SUBLANES

    useq = jnp.concatenate([u_meta.reshape(batch, N_META, width), u_main.reshape(batch, seq, width)], axis=1)
    useq = useq.reshape(batch, chunks, t, g, c)
    useq = jnp.pad(useq, ((0, 0), (0, chunks_p - chunks), (0, 0), (0, 0), (0, 0)))
    v = jnp.transpose(useq, (3, 1, 0, 2, 4)).reshape(g, rows, t * c)

    gb = _pick(g, (4, 2, 1))
    kernel = functools.partial(_ssm_kernel, groups=gb, tiles=tiles, nstate=nstate)
    op_spec = pl.BlockSpec((gb, t * c, t * c), lambda i: (i, 0, 0))
    y = pl.pallas_call(
        kernel,
        grid=(g // gb,),
        in_specs=[
            pl.BlockSpec((gb, rows, t * c), lambda i: (i, 0, 0)),
            op_spec, op_spec, op_spec,
            pl.BlockSpec((gb, 6, F32_SUBLANES, 2 * nstate), lambda i: (i, 0, 0, 0)),
        ],
        out_specs=pl.BlockSpec((gb, rows, t * c), lambda i: (i, 0, 0)),
        out_shape=jax.ShapeDtypeStruct((g, rows, t * c), BF16),
        scratch_shapes=[pltpu.VMEM((gb, rows, 4 * nstate), F32)] * 3,
        compiler_params=_params("parallel"),
        name="s5_scan",
    )(v, m, e, f, a)

    y = y.reshape(g, chunks_p, batch, t, c)[:, :chunks]
    y = jnp.transpose(y, (2, 1, 3, 0, 4)).reshape(batch, N_META + seq, width)
    return y[:, N_META:].reshape(batch * seq, width), y[:, :N_META].reshape(batch * N_META, width)


def _mix_kernel(o_ref, y_ref, gna_ref, gssm_ref, wna_ref, wglu_ref, wssm_ref, out_ref):
    y_na = jnp.dot(o_ref[...], wna_ref[...], preferred_element_type=F32)
    gl = _gelu(y_ref[...].astype(F32))
    z = jnp.dot(gl.astype(BF16), wglu_ref[...], preferred_element_type=F32)
    t = (gl * _sigmoid(z)).astype(BF16)
    y_ssm = jnp.dot(t, wssm_ref[...], preferred_element_type=F32)
    mixed = _sigmoid(gna_ref[...].astype(F32)) * y_na + _sigmoid(gssm_ref[...].astype(F32)) * y_ssm
    out_ref[...] = mixed.astype(out_ref.dtype)


def mix_branches(o_na, y, proj, w_na, w_glu, w_ssm, gate_col):
    rows, na_width = o_na.shape
    ssm_width = y.shape[1]
    d = w_na.shape[1]
    tm = _pick(rows, (512, 256, 128, 64))
    assert gate_col % d == 0
    gblk = gate_col // d
    resident = dict(pipeline_mode=pl.Buffered(1))
    return pl.pallas_call(
        _mix_kernel,
        grid=(rows // tm,),
        in_specs=[
            pl.BlockSpec((tm, na_width), lambda i: (i, 0)),
            pl.BlockSpec((tm, ssm_width), lambda i: (i, 0)),
            pl.BlockSpec((tm, d), lambda i: (i, gblk)),
            pl.BlockSpec((tm, d), lambda i: (i, gblk + 1)),
            pl.BlockSpec(w_na.shape, lambda i: (0, 0), **resident),
            pl.BlockSpec(w_glu.shape, lambda i: (0, 0), **resident),
            pl.BlockSpec(w_ssm.shape, lambda i: (0, 0), **resident),
        ],
        out_specs=pl.BlockSpec((tm, d), lambda i: (i, 0)),
        out_shape=jax.ShapeDtypeStruct((rows, d), BF16),
        compiler_params=_params("parallel"),
        name="mix_branches",
    )(o_na, y, proj, proj, w_na, w_glu, w_ssm)


def _residual_matmul_kernel(h_ref, a_ref, w_ref, o_ref):
    o_ref[...] = h_ref[...] + jnp.dot(a_ref[...], w_ref[...], preferred_element_type=F32)


def residual_matmul(h, a, w):
    rows, d = h.shape
    tm = _pick(rows, (512, 256, 128, 64))
    return pl.pallas_call(
        _residual_matmul_kernel,
        grid=(rows // tm,),
        in_specs=[
            pl.BlockSpec((tm, d), lambda i: (i, 0)),
            pl.BlockSpec((tm, a.shape[1]), lambda i: (i, 0)),
            pl.BlockSpec(w.shape, lambda i: (0, 0), pipeline_mode=pl.Buffered(1)),
        ],
        out_specs=pl.BlockSpec((tm, d), lambda i: (i, 0)),
        out_shape=jax.ShapeDtypeStruct((rows, d), F32),
        compiler_params=_params("parallel"),
        name="residual_out_proj",
    )(h, a, w)


HALO = BF16_SUBLANES


def _ffn_up_kernel(h_ref, prev_ref, next_ref, g_ref, wa_ref, wg_ref, cw_ref, cb_ref, o_ref, hn_ref, *, tm):
    @pl.when(pl.program_id(1) == 0)
    def _():
        g = g_ref[...]
        hn_ref[0:HALO, :] = _rms(prev_ref[...], g).astype(BF16)
        hn_ref[HALO:HALO + tm, :] = _rms(h_ref[...], g).astype(BF16)
        hn_ref[HALO + tm:, :] = _rms(next_ref[...], g).astype(BF16)

    ext = tm + 2 * HALO
    a = jnp.dot(hn_ref[...], wa_ref[...], preferred_element_type=F32)
    gate = jnp.dot(hn_ref[HALO:HALO + tm, :], wg_ref[...], preferred_element_type=F32)
    a_prev = pltpu.roll(a, 1, 0)[HALO:HALO + tm]
    a_next = pltpu.roll(a, ext - 1, 0)[HALO:HALO + tm]
    conv = a_prev * cw_ref[0:1, :] + a[HALO:HALO + tm] * cw_ref[1:2, :] + a_next * cw_ref[2:3, :] + cb_ref[...]
    o_ref[...] = (_gelu(conv) * gate).astype(o_ref.dtype)


def ffn_up(h, halo_prev, halo_next, g, w_up, conv_w, conv_b, tm):
    rows, d = h.shape
    dff = conv_b.shape[0]
    tn = _pick(dff, COL_TILES)
    nj = dff // tn
    kernel = functools.partial(_ffn_up_kernel, tm=tm)
    return pl.pallas_call(
        kernel,
        grid=(rows // tm, nj),
        in_specs=[
            pl.BlockSpec((tm, d), lambda i, j: (i, 0)),
            pl.BlockSpec((HALO, d), lambda i, j: (i, 0)),
            pl.BlockSpec((HALO, d), lambda i, j: (i, 0)),
            pl.BlockSpec((1, d), lambda i, j: (0, 0)),
            pl.BlockSpec((d, tn), lambda i, j: (0, j)),
            pl.BlockSpec((d, tn), lambda i, j: (0, nj + j)),
            pl.BlockSpec((conv_w.shape[0], tn), lambda i, j: (0, j)),
            pl.BlockSpec((1, tn), lambda i, j: (0, j)),
        ],
        out_specs=pl.BlockSpec((tm, tn), lambda i, j: (i, j)),
        out_shape=jax.ShapeDtypeStruct((rows, dff), BF16),
        scratch_shapes=[pltpu.VMEM((tm + 2 * HALO, d), BF16)],
        compiler_params=_params("parallel", "arbitrary"),
        name="ffn_up_conv_gate",
    )(h, halo_prev, halo_next, g.reshape(1, d), w_up, w_up, conv_w, conv_b.reshape(1, dff))


def _ffn_down_kernel(h_ref, a_ref, w_ref, g_ref, o_ref, acc_ref):
    k = pl.program_id(1)

    @pl.when(k == 0)
    def _():
        acc_ref[...] = h_ref[...]

    acc_ref[...] += jnp.dot(a_ref[...], w_ref[...], preferred_element_type=F32)

    @pl.when(k == pl.num_programs(1) - 1)
    def _():
        o_ref[...] = _rms(acc_ref[...], g_ref[...])


def ffn_down(h, act, w_down, g):
    rows, d = h.shape
    dff = act.shape[1]
    tm = _pick(rows, (512, 256, 128, 64))
    tk = _pick(dff, COL_TILES)
    return pl.pallas_call(
        _ffn_down_kernel,
        grid=(rows // tm, dff // tk),
        in_specs=[
            pl.BlockSpec((tm, d), lambda i, k: (i, 0)),
            pl.BlockSpec((tm, tk), lambda i, k: (i, k)),
            pl.BlockSpec((tk, d), lambda i, k: (k, 0)),
            pl.BlockSpec((1, d), lambda i, k: (0, 0)),
        ],
        out_specs=pl.BlockSpec((tm, d), lambda i, k: (i, 0)),
        out_shape=jax.ShapeDtypeStruct((rows, d), F32),
        scratch_shapes=[pltpu.VMEM((tm, d), F32)],
        compiler_params=_params("parallel", "arbitrary"),
        name="ffn_down_final_norm",
    )(h, act, w_down, g.reshape(1, d))


def _conv_halos(h1_main, h1_meta, batch, seq, tm):
    d = h1_main.shape[1]
    per_seq = seq // tm
    hm = h1_main.reshape(batch, per_seq, tm, d)
    meta_tail = h1_meta.reshape(batch, 1, N_META, d)[:, :, N_META - HALO:]
    prev = jnp.concatenate([meta_tail, hm[:, :-1, tm - HALO:]], axis=1)
    nxt = jnp.concatenate([hm[:, 1:, :HALO], jnp.zeros((batch, 1, HALO, d), h1_main.dtype)], axis=1)
    return prev.reshape(batch * per_seq * HALO, d), nxt.reshape(batch * per_seq * HALO, d)


def kernel(x, meta_tokens, norm1_g, w_in, na_rpb, ssm_lam_re, ssm_lam_im, ssm_log_step, ssm_b_re, ssm_b_im,
           ssm_c_re, ssm_c_im, ssm_d, w_glu, w_proj_na, w_proj_ssm, w_out, norm2_g, w_up, conv_w, conv_b,
           w_down, final_g):
    batch, seq, d = x.shape
    depth = w_in.shape[0]
    na_width = w_proj_na.shape[1]
    ssm_width = w_proj_ssm.shape[1]
    assert depth == 1 and N_META >= HALO
    l = 0
    h_main = x.reshape(batch * seq, d)
    h_meta = jnp.broadcast_to(meta_tokens.astype(x.dtype)[None], (batch, N_META, d)).reshape(batch * N_META, d)

    w_in_b = w_in[l].astype(BF16)
    proj_main = norm_matmul(h_main, norm1_g[l], w_in_b)
    proj_meta = norm_matmul(h_meta, norm1_g[l], w_in_b)

    o_main, o_meta = neighbourhood_attention(proj_main, proj_meta, na_rpb[l], batch, seq, na_width)

    u_col = 3 * na_width
    ops = _ssm_operators(ssm_lam_re[l], ssm_lam_im[l], ssm_log_step[l], ssm_b_re[l], ssm_b_im[l],
                         ssm_c_re[l], ssm_c_im[l], ssm_d[l])
    y_main, y_meta = s5_scan(proj_main[:, u_col:u_col + ssm_width], proj_meta[:, u_col:u_col + ssm_width],
                             ops, batch, seq)

    w_na_b, w_glu_b, w_ssm_b = w_proj_na[l].astype(BF16), w_glu[l].astype(BF16), w_proj_ssm[l].astype(BF16)
    w_out_b = w_out[l].astype(BF16)
    gate_col = u_col + ssm_width
    h1 = []
    for h, o, y, proj in ((h_main, o_main, y_main, proj_main), (h_meta, o_meta, y_meta, proj_meta)):
        mixed = mix_branches(o, y, proj, w_na_b, w_glu_b, w_ssm_b, gate_col)
        h1.append(residual_matmul(h, mixed, w_out_b))
    h1_main, h1_meta = h1

    tm = _pick(seq, ROW_TILES)
    halo_prev, halo_next = _conv_halos(h1_main, h1_meta, batch, seq, tm)
    act = ffn_up(h1_main, halo_prev, halo_next, norm2_g[l], w_up[l].astype(BF16), conv_w[l], conv_b[l], tm)
    out = ffn_down(h1_main, act, w_down[l].astype(BF16), final_g)
    return out.reshape(batch, seq, d)
```

```python
import functools
import math

import jax
import jax.numpy as jnp
from jax import lax
from jax.experimental import pallas as pl
from jax.experimental.pallas import tpu as pltpu

F32 = jnp.float32
BF16 = jnp.bfloat16

N_META = 16
GRID_W = 64
ROW_WIN = 8
COL_WIN = 16
RMS_EPS = 1e-6
SSM_CHUNK = 16
MASK_BIAS = -1e30
NA_UNROLL = 4

VMEM_LIMIT_BYTES = 56 * 1024 * 1024
F32_SUBLANES = 8
BF16_SUBLANES = 16

ROW_TILES = (1024, 512, 256, 128, 64)
MID_ROW_TILES = (512, 256, 128, 64)
COL_TILES = (512, 256, 128)
WIDE_COL_TILES = (1024, 512, 256, 128)


def _pick(dim, prefs):
    for t in prefs:
        if dim % t == 0:
            return t
    return dim


def _params(*sem):
    return pltpu.CompilerParams(dimension_semantics=sem, vmem_limit_bytes=VMEM_LIMIT_BYTES)


def _rms(x, g):
    ms = jnp.mean(x * x, axis=-1, keepdims=True)
    return x * lax.rsqrt(ms + RMS_EPS) * g


def _gelu(x):
    c = math.sqrt(2.0 / math.pi)
    return 0.5 * x * (1.0 + jnp.tanh(c * (x + 0.044715 * (x * x * x))))


def _sigmoid(x):
    return 1.0 / (1.0 + jnp.exp(-x))


def _norm_matmul_kernel(x_ref, g_ref, w_ref, o_ref, hn_ref):
    @pl.when(pl.program_id(1) == 0)
    def _():
        hn_ref[...] = _rms(x_ref[...], g_ref[...]).astype(BF16)

    o_ref[...] = jnp.dot(hn_ref[...], w_ref[...], preferred_element_type=F32).astype(o_ref.dtype)


def norm_matmul(x, g, w):
    rows, d = x.shape
    n = w.shape[1]
    tm = _pick(rows, ROW_TILES)
    tn = _pick(n, WIDE_COL_TILES)
    return pl.pallas_call(
        _norm_matmul_kernel,
        grid=(rows // tm, n // tn),
        in_specs=[
            pl.BlockSpec((tm, d), lambda i, j: (i, 0)),
            pl.BlockSpec((1, d), lambda i, j: (0, 0)),
            pl.BlockSpec((d, tn), lambda i, j: (0, j)),
        ],
        out_specs=pl.BlockSpec((tm, tn), lambda i, j: (i, j)),
        out_shape=jax.ShapeDtypeStruct((rows, n), BF16),
        scratch_shapes=[pltpu.VMEM((tm, d), BF16)],
        compiler_params=_params("parallel", "arbitrary"),
        name="norm_in_proj",
    )(x, g.reshape(1, d), w)


def _na_kernel(q_ref, k_ref, v_ref, qm_ref, km_ref, vm_ref, tab_ref, o_ref, om_ref, bias_ref, sm_ref, acc_ref,
               *, rows, dh):
    lane = lax.broadcasted_iota(jnp.int32, (1, 2 * dh), 1)
    head_masks = (lane < dh, lane >= dh)
    scale = dh ** -0.5
    km = km_ref[...]
    vm = vm_ref[...]
    nt = (((1,), (1,)), ((), ()))
    kwin = ROW_WIN * GRID_W

    first_row = lane < GRID_W
    for hh in range(2):
        for d in range(ROW_WIN):
            for i2 in range(ROW_WIN // 2):
                bias_ref[d, hh * GRID_W:(hh + 1) * GRID_W, i2 * 2 * GRID_W:(i2 + 1) * 2 * GRID_W] = jnp.where(
                    first_row, tab_ref[hh, d + 2 * i2], tab_ref[hh, d + 2 * i2 + 1])

    def one_head(x, hh):
        return jnp.where(head_masks[hh], x, jnp.zeros_like(x))

    qm = (qm_ref[...] * scale).astype(BF16)
    om = None
    for hh in range(2):
        s = lax.dot_general(one_head(qm, hh), km, nt, preferred_element_type=F32)
        p = jnp.exp(s - jnp.max(s, axis=-1, keepdims=True))
        o_h = jnp.dot(p.astype(BF16), vm, preferred_element_type=F32) / jnp.sum(p, axis=-1, keepdims=True)
        om = o_h if om is None else jnp.where(head_masks[1], o_h, om)
    om_ref[...] = om.astype(om_ref.dtype)

    q_all = (q_ref[...] * scale).astype(BF16)
    for hh in range(2):
        sm_ref[hh] = lax.dot_general(one_head(q_all, hh), km, nt, preferred_element_type=F32)

    def body(r, carry):
        r0 = jnp.clip(r - ROW_WIN // 2, 0, rows - ROW_WIN)
        d = r0 - r + (ROW_WIN - 1)
        qrows = pl.ds(pl.multiple_of(r * GRID_W, GRID_W), GRID_W)
        qs = (q_ref[qrows, :] * scale).astype(BF16)
        q2 = jnp.concatenate([one_head(qs, 0), one_head(qs, 1)], axis=0)
        kstart = pl.multiple_of(r0 * GRID_W, GRID_W)
        s = lax.dot_general(q2, k_ref[pl.ds(kstart, kwin), :], nt, preferred_element_type=F32) + bias_ref[d]
        s_m = jnp.concatenate([sm_ref[0, qrows, :], sm_ref[1, qrows, :]], axis=0)
        m = jnp.maximum(jnp.max(s, axis=-1, keepdims=True), jnp.max(s_m, axis=-1, keepdims=True))
        p = jnp.exp(s - m)
        p_m = jnp.exp(s_m - m)
        inv = 1.0 / (jnp.sum(p, axis=-1, keepdims=True) + jnp.sum(p_m, axis=-1, keepdims=True))
        acc = jnp.dot(p.astype(BF16), v_ref[pl.ds(kstart, kwin), :], preferred_element_type=F32) * inv
        acc_ref[qrows, :] = jnp.where(head_masks[1], acc[GRID_W:], acc[:GRID_W])
        p_m = p_m * inv
        sm_ref[0, qrows, :] = p_m[:GRID_W]
        sm_ref[1, qrows, :] = p_m[GRID_W:]
        return carry

    lax.fori_loop(0, rows, body, 0, unroll=NA_UNROLL)

    o_meta = [jnp.dot(sm_ref[hh].astype(BF16), vm, preferred_element_type=F32) for hh in range(2)]
    o_ref[...] = (acc_ref[...] + jnp.where(head_masks[1], o_meta[1], o_meta[0])).astype(o_ref.dtype)


def _na_bias_table(rpb):
    c = jnp.arange(GRID_W)[:, None]
    kc = jnp.arange(GRID_W)[None, :]
    col_start = jnp.clip(c - COL_WIN // 2, 0, GRID_W - COL_WIN)
    valid = (kc >= col_start) & (kc < col_start + COL_WIN)
    dc = kc - c + (COL_WIN - 1)
    pick = dc[None] == jnp.arange(2 * COL_WIN - 1)[:, None, None]
    tab = jnp.sum(jnp.where(pick, rpb.astype(F32)[..., None, None], 0.0), axis=2)
    tab = jnp.where(valid, tab, MASK_BIAS)
    return jnp.concatenate([tab, tab], axis=-1)


def neighbourhood_attention(proj_main, proj_meta, rpb, batch, seq, na_width):
    heads = rpb.shape[0]
    dh = na_width // heads
    rows = seq // GRID_W
    assert rows >= ROW_WIN and heads % 2 == 0 and 2 * dh == 128
    pairs = heads // 2
    bias = _na_bias_table(rpb)
    blk = (seq, 2 * dh)
    mblk = (N_META, 2 * dh)
    kernel = functools.partial(_na_kernel, rows=rows, dh=dh)
    return pl.pallas_call(
        kernel,
        grid=(batch, pairs),
        in_specs=[
            pl.BlockSpec(blk, lambda b, h: (b, h)),
            pl.BlockSpec(blk, lambda b, h: (b, pairs + h)),
            pl.BlockSpec(blk, lambda b, h: (b, 2 * pairs + h)),
            pl.BlockSpec(mblk, lambda b, h: (b, h)),
            pl.BlockSpec(mblk, lambda b, h: (b, pairs + h)),
            pl.BlockSpec(mblk, lambda b, h: (b, 2 * pairs + h)),
            pl.BlockSpec((2, 2 * ROW_WIN - 1, GRID_W, 2 * GRID_W), lambda b, h: (h, 0, 0, 0)),
        ],
        out_specs=[
            pl.BlockSpec(blk, lambda b, h: (b, h)),
            pl.BlockSpec(mblk, lambda b, h: (b, h)),
        ],
        out_shape=[
            jax.ShapeDtypeStruct((batch * seq, na_width), BF16),
            jax.ShapeDtypeStruct((batch * N_META, na_width), BF16),
        ],
        scratch_shapes=[pltpu.VMEM((ROW_WIN, 2 * GRID_W, ROW_WIN * GRID_W), F32),
                        pltpu.VMEM((2, seq, N_META), F32),
                        pltpu.VMEM((seq, 2 * dh), F32)],
        compiler_params=_params("parallel", "parallel"),
        name="neighbourhood_attention",
    )(proj_main, proj_main, proj_main, proj_meta, proj_meta, proj_meta, bias)


def _ssm_operators(lam_re, lam_im, log_step, b_re, b_im, c_re, c_im, d_skip):
    t = SSM_CHUNK
    hp = lax.Precision.HIGHEST
    lr, li = lam_re.astype(F32), lam_im.astype(F32)
    step = jnp.exp(log_step.astype(F32))[..., None]
    dt_r, dt_i = lr * step, li * step
    g, n = lr.shape[1], lr.shape[2]
    c = b_re.shape[-1]

    def lam_bar_pow(k):
        mag = jnp.exp(dt_r[:, :, None] * k)
        return mag * jnp.cos(dt_i[:, :, None] * k), mag * jnp.sin(dt_i[:, :, None] * k)

    p_r, p_i = lam_bar_pow(jnp.arange(t + 1, dtype=F32)[:, None])
    x_r, x_i = p_r[:, :, 1] - 1.0, p_i[:, :, 1]
    den = lr * lr + li * li
    q_r, q_i = (x_r * lr + x_i * li) / den, (x_i * lr - x_r * li) / den
    bb_r = q_r[..., None] * b_re.astype(F32) - q_i[..., None] * b_im.astype(F32)
    bb_i = q_r[..., None] * b_im.astype(F32) + q_i[..., None] * b_re.astype(F32)
    cr, ci = c_re.astype(F32)[:, :, None], c_im.astype(F32)[:, :, None]
    cp_r = cr * p_r[:, :, :, None] - ci * p_i[:, :, :, None]
    cp_i = cr * p_i[:, :, :, None] + ci * p_r[:, :, :, None]

    kern = (jnp.einsum('dglcn,dgnk->dglck', cp_r[:, :, :t], bb_r, precision=hp)
            - jnp.einsum('dglcn,dgnk->dglck', cp_i[:, :, :t], bb_i, precision=hp))
    s_i = jnp.arange(t)[:, None]
    t_i = jnp.arange(t)[None, :]
    lag = jnp.abs(t_i - s_i)
    one_hot = (lag[None] == jnp.arange(t)[:, None, None]).astype(F32)
    m_f = jnp.einsum('lst,glck->gstck', one_hot * (t_i >= s_i), kern[0], precision=hp)
    m_r = jnp.einsum('lst,glck->gstck', one_hot * (s_i >= t_i), kern[1], precision=hp)
    skip = (jnp.eye(t, dtype=F32)[None, :, :, None, None]
            * (jnp.eye(c, dtype=F32)[None] * d_skip.astype(F32).reshape(g, c, 1))[:, None, None])
    m = jnp.transpose(m_f + m_r + skip, (0, 1, 4, 2, 3)).reshape(g, t * c, t * c)

    def in_map(d, pw_r, pw_i):
        b_r, b_i = jnp.swapaxes(bb_r[d], 1, 2)[:, None], jnp.swapaxes(bb_i[d], 1, 2)[:, None]
        e_r = pw_r[:, :, None] * b_r - pw_i[:, :, None] * b_i
        e_i = pw_r[:, :, None] * b_i + pw_i[:, :, None] * b_r
        return e_r.reshape(g, t * c, n), e_i.reshape(g, t * c, n)

    ef_r, ef_i = in_map(0, p_r[0][:, ::-1][:, 1:], p_i[0][:, ::-1][:, 1:])
    er_r, er_i = in_map(1, p_r[1][:, :t], p_i[1][:, :t])
    me = jnp.concatenate([m, ef_r, er_r, ef_i, er_i], axis=-1)

    def out_map(x):
        return jnp.transpose(x, (0, 3, 1, 2)).reshape(g, n, t * c)

    f = jnp.concatenate([out_map(cp_r[0][:, 1:]), out_map(cp_r[1][:, ::-1][:, :t]),
                         -out_map(cp_i[0][:, 1:]), -out_map(cp_i[1][:, ::-1][:, :t])], axis=1)

    sub = F32_SUBLANES
    row = jnp.arange(sub)

    def decay(k_rows, keep):
        a_r, a_i = lam_bar_pow((t * k_rows).astype(F32)[:, None])
        a_r = jnp.concatenate([a_r[0], a_r[1]], axis=-1) * keep[None, :, None]
        a_i = jnp.concatenate([a_i[0], a_i[1]], axis=-1) * keep[None, :, None]
        return [a_r, a_i]

    slots = []
    for k in (1, 2, 4):
        slots += decay(jnp.full((sub,), k), row >= k)
    slots += decay(row + 1, row >= 0)
    for k in (1, 2, 4):
        slots += decay(jnp.full((sub,), k), row < sub - k)
    slots += decay(sub - row, row >= 0)
    a = jnp.stack(slots, axis=1)

    lanes_per_tok = 128
    toks = lanes_per_tok // c
    grp = lanes_per_tok // c
    src = jnp.transpose(jnp.arange(toks * grp * c).reshape(toks, grp, c), (1, 0, 2)).reshape(-1)
    perm = (jnp.arange(toks * grp * c)[:, None] == src[None, :]).astype(BF16)
    return perm, me.astype(BF16), f.astype(BF16), a.astype(F32)


LANES = 128
SSM_SUB = 2
SSM_META_ROWS = BF16_SUBLANES


def _ssm_kernel(*refs, batch, nchunks, nstate, gsub):
    t = SSM_CHUNK
    u_refs, um_refs = refs[0:t], refs[t:2 * t]
    p_ref, me_ref, f_ref, a_ref = refs[2 * t:2 * t + 4]
    y_ref, ym_ref = refs[2 * t + 4:2 * t + 6]
    yi_ref, ee_ref, pf_ref, pr_ref, ycat_ref, ymcat_ref = refs[2 * t + 6:]
    gtile = me_ref.shape[0]
    tc = f_ref.shape[2]
    toks = p_ref.shape[0] // LANES
    parts = t // toks
    sub = F32_SUBLANES
    w = 2 * nstate
    tiles = nchunks // sub
    perm = p_ref[...]
    row = lax.broadcasted_iota(jnp.int32, (sub, w), 0)
    lane = lax.broadcasted_iota(jnp.int32, (1, w), 1)
    rev_lane = lane >= nstate

    def group_major(token_refs):
        out = []
        for part in range(parts):
            x = jnp.concatenate([token_refs[part * toks + i][...] for i in range(toks)], axis=1)
            out.append(jnp.dot(x, perm, preferred_element_type=F32).astype(BF16))
        return out

    def group_chunk(regrouped, g):
        return jnp.concatenate([x[:, g * LANES:(g + 1) * LANES] for x in regrouped], axis=1)

    v_main = group_major(u_refs)
    v_meta = group_major(um_refs)

    def cmul_add(xr, xi, ar, ai, yr, yi):
        return xr + ar * yr - ai * yi, xi + ar * yi + ai * yr

    def bcast_row(x, r):
        return jnp.broadcast_to(x[r:r + 1, :], (sub, w))

    for sb in range(gtile // gsub):
        gs = [sb * gsub + gi for gi in range(gsub)]
        meta = []
        for gi, g in enumerate(gs):
            me = jnp.dot(group_chunk(v_main, g), me_ref[g], preferred_element_type=F32)
            yi_ref[gi] = me[:, :tc]
            ee_ref[gi] = me[:, tc:]
            meta.append(jnp.dot(group_chunk(v_meta, g), me_ref[g], preferred_element_type=F32))

        def body(j, carry, gs=gs):
            new = []
            for gi, g in enumerate(gs):
                for b in range(batch):
                    lr, li, fr, fi = carry[gi * batch + b]
                    fs = pl.ds(pl.multiple_of(b * nchunks + j * sub, sub), sub)
                    xr, xi = ee_ref[gi, fs, 0:w], ee_ref[gi, fs, w:2 * w]
                    for step, k in enumerate((1, 2, 4)):
                        xr, xi = cmul_add(xr, xi, a_ref[g, 2 * step], a_ref[g, 2 * step + 1],
                                          pltpu.roll(xr, k, 0), pltpu.roll(xi, k, 0))
                    sr, si = cmul_add(xr, xi, a_ref[g, 6], a_ref[g, 7], lr, li)
                    pf_ref[gi, fs, 0:w] = jnp.where(row == 0, lr, pltpu.roll(sr, 1, 0))
                    pf_ref[gi, fs, w:2 * w] = jnp.where(row == 0, li, pltpu.roll(si, 1, 0))
                    rs = pl.ds(pl.multiple_of(b * nchunks + (tiles - 1 - j) * sub, sub), sub)
                    xr, xi = ee_ref[gi, rs, 0:w], ee_ref[gi, rs, w:2 * w]
                    for step, k in enumerate((1, 2, 4)):
                        xr, xi = cmul_add(xr, xi, a_ref[g, 8 + 2 * step], a_ref[g, 9 + 2 * step],
                                          pltpu.roll(xr, sub - k, 0), pltpu.roll(xi, sub - k, 0))
                    rr, ri = cmul_add(xr, xi, a_ref[g, 14], a_ref[g, 15], fr, fi)
                    pr_ref[gi, rs, 0:w] = jnp.where(row == sub - 1, fr, pltpu.roll(rr, sub - 1, 0))
                    pr_ref[gi, rs, w:2 * w] = jnp.where(row == sub - 1, fi, pltpu.roll(ri, sub - 1, 0))
                    new.append((bcast_row(sr, sub - 1), bcast_row(si, sub - 1), bcast_row(rr, 0), bcast_row(ri, 0)))
            return tuple(new)

        zero = jnp.zeros((sub, w), F32)
        init = tuple((bcast_row(meta[gi][:, tc:tc + w], b), bcast_row(meta[gi][:, tc + w:tc + 2 * w], b), zero, zero)
                     for gi in range(gsub) for b in range(batch))
        final = lax.fori_loop(0, tiles, body, init)

        for gi, g in enumerate(gs):
            prev = jnp.concatenate([jnp.where(rev_lane, pr_ref[gi, :, k * w:(k + 1) * w], pf_ref[gi, :, k * w:(k + 1) * w])
                                    for k in range(2)], axis=1).astype(BF16)
            y = (yi_ref[gi] + jnp.dot(prev, f_ref[g], preferred_element_type=F32)).astype(BF16)
            pm = []
            for k in (2, 3):
                x = zero
                for b in range(batch):
                    x = jnp.where((row == b) & rev_lane, final[gi * batch + b][k], x)
                pm.append(jnp.concatenate([x, jnp.zeros((SSM_META_ROWS - sub, w), F32)], axis=0))
            prev_m = jnp.concatenate(pm, axis=1).astype(BF16)
            y_m = (meta[gi][:, :tc] + jnp.dot(prev_m, f_ref[g], preferred_element_type=F32)).astype(BF16)
            for part in range(parts):
                ycat_ref[part, :, g * LANES:(g + 1) * LANES] = y[:, part * LANES:(part + 1) * LANES]
                ymcat_ref[part, :, g * LANES:(g + 1) * LANES] = y_m[:, part * LANES:(part + 1) * LANES]

    for part in range(parts):
        o = jnp.dot(ycat_ref[part], perm, preferred_element_type=F32).astype(y_ref.dtype)
        o_m = jnp.dot(ymcat_ref[part], perm, preferred_element_type=F32).astype(ym_ref.dtype)
        for i in range(toks):
            y_ref[part * toks + i] = o[:, i * LANES:(i + 1) * LANES]
            ym_ref[part * toks + i] = o_m[:, i * LANES:(i + 1) * LANES]


def s5_scan(proj_main, proj_meta, u_col, ops, batch, seq):
    perm, me, f, a = ops
    g = me.shape[0]
    t = SSM_CHUNK
    tc = f.shape[2]
    c = tc // t
    width = g * c
    nstate = f.shape[1] // 4
    gtile = LANES // c
    wtot = proj_main.shape[1]
    nchunks = seq // t
    rows = batch * nchunks
    assert N_META == t and seq % (t * F32_SUBLANES) == 0 and batch <= F32_SUBLANES
    assert u_col % LANES == 0 and wtot % LANES == 0 and g % gtile == 0 and gtile % SSM_SUB == 0

    chunk_rows = proj_main.reshape(rows, t * wtot)
    u_meta = proj_meta[:, u_col:u_col + width].reshape(batch, t * width)
    u_meta = jnp.pad(u_meta, ((0, SSM_META_ROWS - batch), (0, 0)))
    main_spec = [pl.BlockSpec((rows, LANES), lambda i, k=k: (0, k * (wtot // LANES) + u_col // LANES + i))
                 for k in range(t)]
    meta_spec = [pl.BlockSpec((SSM_META_ROWS, LANES), lambda i, k=k: (0, k * (width // LANES) + i)) for k in range(t)]
    kernel = functools.partial(_ssm_kernel, batch=batch, nchunks=nchunks, nstate=nstate, gsub=SSM_SUB)
    y, y_m = pl.pallas_call(
        kernel,
        grid=(g // gtile,),
        in_specs=main_spec + meta_spec + [
            pl.BlockSpec(perm.shape, lambda i: (0, 0)),
            pl.BlockSpec((gtile,) + me.shape[1:], lambda i: (i, 0, 0)),
            pl.BlockSpec((gtile,) + f.shape[1:], lambda i: (i, 0, 0)),
            pl.BlockSpec((gtile,) + a.shape[1:], lambda i: (i, 0, 0, 0)),
        ],
        out_specs=[
            pl.BlockSpec((t, rows, LANES), lambda i: (0, 0, i)),
            pl.BlockSpec((t, SSM_META_ROWS, LANES), lambda i: (0, 0, i)),
        ],
        out_shape=[
            jax.ShapeDtypeStruct((t, rows, width), BF16),
            jax.ShapeDtypeStruct((t, SSM_META_ROWS, width), BF16),
        ],
        scratch_shapes=[
            pltpu.VMEM((SSM_SUB, rows, tc), F32),
            pltpu.VMEM((SSM_SUB, rows, 4 * nstate), F32),
            pltpu.VMEM((SSM_SUB, rows, 4 * nstate), F32),
            pltpu.VMEM((SSM_SUB, rows, 4 * nstate), F32),
            pltpu.VMEM((t // (perm.shape[0] // LANES), rows, gtile * LANES), BF16),
            pltpu.VMEM((t // (perm.shape[0] // LANES), SSM_META_ROWS, gtile * LANES), BF16),
        ],
        compiler_params=_params("parallel"),
        name="s5_scan",
    )(*([chunk_rows] * t), *([u_meta] * t), perm, me, f, a)

    y_main = jnp.transpose(y, (1, 0, 2)).reshape(batch * seq, width)
    y_meta = jnp.transpose(y_m[:, :batch], (1, 0, 2)).reshape(batch * N_META, width)
    return y_main, y_meta


def _mix_kernel(o_ref, y_ref, gna_ref, gssm_ref, wna_ref, wglu_ref, wssm_ref, out_ref):
    y_na = jnp.dot(o_ref[...], wna_ref[...], preferred_element_type=F32)
    gl = _gelu(y_ref[...].astype(F32))
    z = jnp.dot(gl.astype(BF16), wglu_ref[...], preferred_element_type=F32)
    t = (gl * _sigmoid(z)).astype(BF16)
    y_ssm = jnp.dot(t, wssm_ref[...], preferred_element_type=F32)
    mixed = _sigmoid(gna_ref[...].astype(F32)) * y_na + _sigmoid(gssm_ref[...].astype(F32)) * y_ssm
    out_ref[...] = mixed.astype(out_ref.dtype)


def mix_branches(o_na, y, proj, w_na, w_glu, w_ssm, gate_col):
    rows, na_width = o_na.shape
    ssm_width = y.shape[1]
    d = w_na.shape[1]
    tm = _pick(rows, MID_ROW_TILES)
    assert gate_col % d == 0
    gblk = gate_col // d
    resident = dict(pipeline_mode=pl.Buffered(1))
    return pl.pallas_call(
        _mix_kernel,
        grid=(rows // tm,),
        in_specs=[
            pl.BlockSpec((tm, na_width), lambda i: (i, 0)),
            pl.BlockSpec((tm, ssm_width), lambda i: (i, 0)),
            pl.BlockSpec((tm, d), lambda i: (i, gblk)),
            pl.BlockSpec((tm, d), lambda i: (i, gblk + 1)),
            pl.BlockSpec(w_na.shape, lambda i: (0, 0), **resident),
            pl.BlockSpec(w_glu.shape, lambda i: (0, 0), **resident),
            pl.BlockSpec(w_ssm.shape, lambda i: (0, 0), **resident),
        ],
        out_specs=pl.BlockSpec((tm, d), lambda i: (i, 0)),
        out_shape=jax.ShapeDtypeStruct((rows, d), BF16),
        compiler_params=_params("parallel"),
        name="mix_branches",
    )(o_na, y, proj, proj, w_na, w_glu, w_ssm)


def _residual_matmul_kernel(h_ref, a_ref, w_ref, o_ref):
    o_ref[...] = h_ref[...] + jnp.dot(a_ref[...], w_ref[...], preferred_element_type=F32)


def residual_matmul(h, a, w):
    rows, d = h.shape
    tm = _pick(rows, MID_ROW_TILES)
    return pl.pallas_call(
        _residual_matmul_kernel,
        grid=(rows // tm,),
        in_specs=[
            pl.BlockSpec((tm, d), lambda i: (i, 0)),
            pl.BlockSpec((tm, a.shape[1]), lambda i: (i, 0)),
            pl.BlockSpec(w.shape, lambda i: (0, 0), pipeline_mode=pl.Buffered(1)),
        ],
        out_specs=pl.BlockSpec((tm, d), lambda i: (i, 0)),
        out_shape=jax.ShapeDtypeStruct((rows, d), F32),
        compiler_params=_params("parallel"),
        name="residual_out_proj",
    )(h, a, w)


HALO = BF16_SUBLANES


def _ffn_up_kernel(h_ref, prev_ref, next_ref, g_ref, wa_ref, wg_ref, cw_ref, cb_ref, o_ref, hn_ref, *, tm):
    @pl.when(pl.program_id(1) == 0)
    def _():
        g = g_ref[...]
        hn_ref[0:HALO, :] = _rms(prev_ref[...], g).astype(BF16)
        hn_ref[HALO:HALO + tm, :] = _rms(h_ref[...], g).astype(BF16)
        hn_ref[HALO + tm:, :] = _rms(next_ref[...], g).astype(BF16)

    ext = tm + 2 * HALO
    a = jnp.dot(hn_ref[...], wa_ref[...], preferred_element_type=F32)
    gate = jnp.dot(hn_ref[HALO:HALO + tm, :], wg_ref[...], preferred_element_type=F32)
    a_prev = pltpu.roll(a, 1, 0)[HALO:HALO + tm]
    a_next = pltpu.roll(a, ext - 1, 0)[HALO:HALO + tm]
    conv = a_prev * cw_ref[0:1, :] + a[HALO:HALO + tm] * cw_ref[1:2, :] + a_next * cw_ref[2:3, :] + cb_ref[...]
    o_ref[...] = (_gelu(conv) * gate).astype(o_ref.dtype)


def ffn_up(h, halo_prev, halo_next, g, w_up, conv_w, conv_b, tm):
    rows, d = h.shape
    dff = conv_b.shape[0]
    tn = _pick(dff, COL_TILES)
    nj = dff // tn
    kernel = functools.partial(_ffn_up_kernel, tm=tm)
    return pl.pallas_call(
        kernel,
        grid=(rows // tm, nj),
        in_specs=[
            pl.BlockSpec((tm, d), lambda i, j: (i, 0)),
            pl.BlockSpec((HALO, d), lambda i, j: (i, 0)),
            pl.BlockSpec((HALO, d), lambda i, j: (i, 0)),
            pl.BlockSpec((1, d), lambda i, j: (0, 0)),
            pl.BlockSpec((d, tn), lambda i, j: (0, j)),
            pl.BlockSpec((d, tn), lambda i, j: (0, nj + j)),
            pl.BlockSpec((conv_w.shape[0], tn), lambda i, j: (0, j)),
            pl.BlockSpec((1, tn), lambda i, j: (0, j)),
        ],
        out_specs=pl.BlockSpec((tm, tn), lambda i, j: (i, j)),
        out_shape=jax.ShapeDtypeStruct((rows, dff), BF16),
        scratch_shapes=[pltpu.VMEM((tm + 2 * HALO, d), BF16)],
        compiler_params=_params("parallel", "arbitrary"),
        name="ffn_up_conv_gate",
    )(h, halo_prev, halo_next, g.reshape(1, d), w_up, w_up, conv_w, conv_b.reshape(1, dff))


def _ffn_down_kernel(h_ref, a_ref, w_ref, g_ref, o_ref, acc_ref):
    k = pl.program_id(1)

    @pl.when(k == 0)
    def _():
        acc_ref[...] = h_ref[...]

    acc_ref[...] += jnp.dot(a_ref[...], w_ref[...], preferred_element_type=F32)

    @pl.when(k == pl.num_programs(1) - 1)
    def _():
        o_ref[...] = _rms(acc_ref[...], g_ref[...])


def ffn_down(h, act, w_down, g):
    rows, d = h.shape
    dff = act.shape[1]
    tm = _pick(rows, ROW_TILES)
    tk = _pick(dff, COL_TILES)
    return pl.pallas_call(
        _ffn_down_kernel,
        grid=(rows // tm, dff // tk),
        in_specs=[
            pl.BlockSpec((tm, d), lambda i, k: (i, 0)),
            pl.BlockSpec((tm, tk), lambda i, k: (i, k)),
            pl.BlockSpec((tk, d), lambda i, k: (k, 0)),
            pl.BlockSpec((1, d), lambda i, k: (0, 0)),
        ],
        out_specs=pl.BlockSpec((tm, d), lambda i, k: (i, 0)),
        out_shape=jax.ShapeDtypeStruct((rows, d), F32),
        scratch_shapes=[pltpu.VMEM((tm, d), F32)],
        compiler_params=_params("parallel", "arbitrary"),
        name="ffn_down_final_norm",
    )(h, act, w_down, g.reshape(1, d))


def _conv_halos(h1_main, h1_meta, batch, seq, tm):
    d = h1_main.shape[1]
    per_seq = seq // tm
    hm = h1_main.reshape(batch, per_seq, tm, d)
    meta_tail = h1_meta.reshape(batch, 1, N_META, d)[:, :, N_META - HALO:]
    prev = jnp.concatenate([meta_tail, hm[:, :-1, tm - HALO:]], axis=1)
    nxt = jnp.concatenate([hm[:, 1:, :HALO], jnp.zeros((batch, 1, HALO, d), h1_main.dtype)], axis=1)
    return prev.reshape(batch * per_seq * HALO, d), nxt.reshape(batch * per_seq * HALO, d)


def kernel(x, meta_tokens, norm1_g, w_in, na_rpb, ssm_lam_re, ssm_lam_im, ssm_log_step, ssm_b_re, ssm_b_im,
           ssm_c_re, ssm_c_im, ssm_d, w_glu, w_proj_na, w_proj_ssm, w_out, norm2_g, w_up, conv_w, conv_b,
           w_down, final_g):
    batch, seq, d = x.shape
    depth = w_in.shape[0]
    na_width = w_proj_na.shape[1]
    ssm_width = w_proj_ssm.shape[1]
    assert depth == 1 and N_META >= HALO
    l = 0
    h_main = x.reshape(batch * seq, d)
    h_meta = jnp.broadcast_to(meta_tokens.astype(x.dtype)[None], (batch, N_META, d)).reshape(batch * N_META, d)

    w_in_b = w_in[l].astype(BF16)
    proj_main = norm_matmul(h_main, norm1_g[l], w_in_b)
    proj_meta = norm_matmul(h_meta, norm1_g[l], w_in_b)

    o_main, o_meta = neighbourhood_attention(proj_main, proj_meta, na_rpb[l], batch, seq, na_width)

    u_col = 3 * na_width
    ops = _ssm_operators(ssm_lam_re[l], ssm_lam_im[l], ssm_log_step[l], ssm_b_re[l], ssm_b_im[l],
                         ssm_c_re[l], ssm_c_im[l], ssm_d[l])
    y_main, y_meta = s5_scan(proj_main, proj_meta, u_col, ops, batch, seq)

    w_na_b, w_glu_b, w_ssm_b = w_proj_na[l].astype(BF16), w_glu[l].astype(BF16), w_proj_ssm[l].astype(BF16)
    w_out_b = w_out[l].astype(BF16)
    gate_col = u_col + ssm_width
    h1 = []
    for h, o, y, proj in ((h_main, o_main, y_main, proj_main), (h_meta, o_meta, y_meta, proj_meta)):
        mixed = mix_branches(o, y, proj, w_na_b, w_glu_b, w_ssm_b, gate_col)
        h1.append(residual_matmul(h, mixed, w_out_b))
    h1_main, h1_meta = h1

    tm = _pick(seq, ROW_TILES)
    halo_prev, halo_next = _conv_halos(h1_main, h1_meta, batch, seq, tm)
    act = ffn_up(h1_main, halo_prev, halo_next, norm2_g[l], w_up[l].astype(BF16), conv_w[l], conv_b[l], tm)
    out = ffn_down(h1_main, act, w_down[l].astype(BF16), final_g)
    return out.reshape(batch, seq, d)
```

```python
import functools
import math

import jax
import jax.numpy as jnp
from jax import lax
from jax.experimental import pallas as pl
from jax.experimental.pallas import tpu as pltpu

F32 = jnp.float32
BF16 = jnp.bfloat16

N_META = 16
GRID_W = 64
ROW_WIN = 8
COL_WIN = 16
RMS_EPS = 1e-6
SSM_CHUNK = 16
MASK_BIAS = -1e30
NA_QROWS = 4
NA_KROWS = 12
NA_UNROLL = 2

VMEM_LIMIT_BYTES = 56 * 1024 * 1024
F32_SUBLANES = 8
BF16_SUBLANES = 16

ROW_TILES = (1024, 512, 256, 128, 64)
MID_ROW_TILES = (512, 256, 128, 64)
COL_TILES = (512, 256, 128)
WIDE_COL_TILES = (1024, 512, 256, 128)


def _pick(dim, prefs):
    for t in prefs:
        if dim % t == 0:
            return t
    return dim


def _params(*sem):
    return pltpu.CompilerParams(dimension_semantics=sem, vmem_limit_bytes=VMEM_LIMIT_BYTES)


def _rms(x, g):
    ms = jnp.mean(x * x, axis=-1, keepdims=True)
    return x * lax.rsqrt(ms + RMS_EPS) * g


def _gelu(x):
    c = math.sqrt(2.0 / math.pi)
    return 0.5 * x * (1.0 + jnp.tanh(c * (x + 0.044715 * (x * x * x))))


def _sigmoid(x):
    return 1.0 / (1.0 + jnp.exp(-x))


def _norm_matmul_kernel(x_ref, g_ref, w_ref, o_ref, *rest, chunk_cols):
    hn_ref = rest[-1] if chunk_cols is None else rest[1]

    @pl.when(pl.program_id(1) == 0)
    def _():
        hn_ref[...] = _rms(x_ref[...], g_ref[...]).astype(BF16)

    r = jnp.dot(hn_ref[...], w_ref[...], preferred_element_type=F32)
    o_ref[...] = r.astype(o_ref.dtype)

    if chunk_cols is not None:
        u3_ref, _, stage_ref = rest
        tile, off, width = chunk_cols
        t = SSM_CHUNK

        @pl.when(pl.program_id(1) == tile)
        def _():
            for lt in range(width // LANES):
                stage_ref[lt] = r[:, off + lt * LANES:off + (lt + 1) * LANES]
            for k in range(t):
                for lt in range(width // LANES):
                    u3_ref[k, :, lt * LANES:(lt + 1) * LANES] = stage_ref[
                        lt, pl.ds(k, stage_ref.shape[1] // t, stride=t), :].astype(u3_ref.dtype)


def norm_matmul(x, g, w, chunk_major=None):
    rows, d = x.shape
    n = w.shape[1]
    tm = _pick(rows, ROW_TILES)
    tn = _pick(n, WIDE_COL_TILES)
    out_specs = [pl.BlockSpec((tm, tn), lambda i, j: (i, j))]
    out_shape = [jax.ShapeDtypeStruct((rows, n), BF16)]
    scratch = [pltpu.VMEM((tm, d), BF16)]
    chunk_cols = None
    if chunk_major is not None:
        col, width = chunk_major
        t = SSM_CHUNK
        chunk_cols = (col // tn, col % tn, width)
        assert col % tn + width <= tn and tm % (t * F32_SUBLANES) == 0
        out_specs.append(pl.BlockSpec((t, tm // t, width), lambda i, j: (0, i, 0)))
        out_shape.append(jax.ShapeDtypeStruct((t, rows // t, width), BF16))
        scratch.append(pltpu.VMEM((width // LANES, tm, LANES), F32))
    out = pl.pallas_call(
        functools.partial(_norm_matmul_kernel, chunk_cols=chunk_cols),
        grid=(rows // tm, n // tn),
        in_specs=[
            pl.BlockSpec((tm, d), lambda i, j: (i, 0)),
            pl.BlockSpec((1, d), lambda i, j: (0, 0)),
            pl.BlockSpec((d, tn), lambda i, j: (0, j)),
        ],
        out_specs=out_specs,
        out_shape=out_shape,
        scratch_shapes=scratch,
        compiler_params=_params("parallel", "arbitrary"),
        name="norm_in_proj",
    )(x, g.reshape(1, d), w)
    return out[0] if chunk_major is None else out


def _na_kernel(q_ref, k_ref, v_ref, qm_ref, km_ref, vm_ref, tab_ref, o_ref, om_ref, bias_ref, sm_ref, acc_ref,
               *, rows, dh):
    lane = lax.broadcasted_iota(jnp.int32, (1, 2 * dh), 1)
    head_masks = (lane < dh, lane >= dh)
    scale = dh ** -0.5
    km = km_ref[...]
    vm = vm_ref[...]
    nt = (((1,), (1,)), ((), ()))

    masked = 2 * ROW_WIN - 1
    nblk = rows // NA_QROWS
    qblk = NA_QROWS * GRID_W
    kblk = NA_KROWS * GRID_W

    def rel_row(kind, j, i):
        if kind == 0:
            return i - j + (ROW_WIN - 1) if i < ROW_WIN else masked
        if kind == 1:
            return i - j + (ROW_WIN - 1 - ROW_WIN // 2) if j <= i < j + ROW_WIN else masked
        lo = NA_KROWS - ROW_WIN
        return i - j + (NA_QROWS - NA_KROWS) + (ROW_WIN - 1) if i >= lo else masked

    first_row = lane < GRID_W
    for kind in range(3):
        for hh in range(2):
            for j in range(NA_QROWS):
                r_lo = (hh * NA_QROWS + j) * GRID_W
                for i2 in range(NA_KROWS // 2):
                    bias_ref[kind, r_lo:r_lo + GRID_W, i2 * 2 * GRID_W:(i2 + 1) * 2 * GRID_W] = jnp.where(
                        first_row, tab_ref[hh, rel_row(kind, j, 2 * i2)], tab_ref[hh, rel_row(kind, j, 2 * i2 + 1)])

    def one_head(x, hh):
        return jnp.where(head_masks[hh], x, jnp.zeros_like(x))

    qm = (qm_ref[...] * scale).astype(BF16)
    om = None
    for hh in range(2):
        s = lax.dot_general(one_head(qm, hh), km, nt, preferred_element_type=F32)
        p = jnp.exp(s - jnp.max(s, axis=-1, keepdims=True))
        o_h = jnp.dot(p.astype(BF16), vm, preferred_element_type=F32) / jnp.sum(p, axis=-1, keepdims=True)
        om = o_h if om is None else jnp.where(head_masks[1], o_h, om)
    om_ref[...] = om.astype(om_ref.dtype)

    q_all = (q_ref[...] * scale).astype(BF16)
    for hh in range(2):
        sm_ref[hh] = lax.dot_general(one_head(q_all, hh), km, nt, preferred_element_type=F32)

    def body(blk, carry):
        r = blk * NA_QROWS
        k0 = jnp.clip(r - ROW_WIN // 2, 0, rows - NA_KROWS)
        kind = jnp.where(blk == 0, 0, jnp.where(blk == nblk - 1, 2, 1))
        qrows = pl.ds(pl.multiple_of(r * GRID_W, qblk), qblk)
        krows = pl.ds(pl.multiple_of(k0 * GRID_W, GRID_W), kblk)
        qs = (q_ref[qrows, :] * scale).astype(BF16)
        q2 = jnp.concatenate([one_head(qs, 0), one_head(qs, 1)], axis=0)
        s = lax.dot_general(q2, k_ref[krows, :], nt, preferred_element_type=F32) + bias_ref[kind]
        s_m = jnp.concatenate([sm_ref[0, qrows, :], sm_ref[1, qrows, :]], axis=0)
        m = jnp.maximum(jnp.max(s, axis=-1, keepdims=True), jnp.max(s_m, axis=-1, keepdims=True))
        p = jnp.exp(s - m)
        p_m = jnp.exp(s_m - m)
        inv = 1.0 / (jnp.sum(p, axis=-1, keepdims=True) + jnp.sum(p_m, axis=-1, keepdims=True))
        acc = jnp.dot(p.astype(BF16), v_ref[krows, :], preferred_element_type=F32) * inv
        acc_ref[qrows, :] = jnp.where(head_masks[1], acc[qblk:], acc[:qblk])
        p_m = p_m * inv
        sm_ref[0, qrows, :] = p_m[:qblk]
        sm_ref[1, qrows, :] = p_m[qblk:]
        return carry

    lax.fori_loop(0, nblk, body, 0, unroll=NA_UNROLL)

    o_meta = [jnp.dot(sm_ref[hh].astype(BF16), vm, preferred_element_type=F32) for hh in range(2)]
    o_ref[...] = (acc_ref[...] + jnp.where(head_masks[1], o_meta[1], o_meta[0])).astype(o_ref.dtype)


def _na_bias_table(rpb):
    c = jnp.arange(GRID_W)[:, None]
    kc = jnp.arange(GRID_W)[None, :]
    col_start = jnp.clip(c - COL_WIN // 2, 0, GRID_W - COL_WIN)
    valid = (kc >= col_start) & (kc < col_start + COL_WIN)
    dc = kc - c + (COL_WIN - 1)
    pick = dc[None] == jnp.arange(2 * COL_WIN - 1)[:, None, None]
    tab = jnp.sum(jnp.where(pick, rpb.astype(F32)[..., None, None], 0.0), axis=2)
    tab = jnp.where(valid, tab, MASK_BIAS)
    tab = jnp.concatenate([tab, jnp.full_like(tab[:, :1], MASK_BIAS)], axis=1)
    return jnp.concatenate([tab, tab], axis=-1)


def neighbourhood_attention(proj_main, proj_meta, rpb, batch, seq, na_width):
    heads = rpb.shape[0]
    dh = na_width // heads
    rows = seq // GRID_W
    assert rows >= NA_KROWS and rows % NA_QROWS == 0 and heads % 2 == 0 and 2 * dh == LANES
    assert NA_KROWS % 2 == 0 and NA_KROWS >= ROW_WIN + NA_QROWS - 1 and NA_QROWS <= ROW_WIN // 2
    pairs = heads // 2
    bias = _na_bias_table(rpb)
    blk = (seq, 2 * dh)
    mblk = (N_META, 2 * dh)
    kernel = functools.partial(_na_kernel, rows=rows, dh=dh)
    return pl.pallas_call(
        kernel,
        grid=(batch, pairs),
        in_specs=[
            pl.BlockSpec(blk, lambda b, h: (b, h)),
            pl.BlockSpec(blk, lambda b, h: (b, pairs + h)),
            pl.BlockSpec(blk, lambda b, h: (b, 2 * pairs + h)),
            pl.BlockSpec(mblk, lambda b, h: (b, h)),
            pl.BlockSpec(mblk, lambda b, h: (b, pairs + h)),
            pl.BlockSpec(mblk, lambda b, h: (b, 2 * pairs + h)),
            pl.BlockSpec((2, 2 * ROW_WIN, GRID_W, 2 * GRID_W), lambda b, h: (h, 0, 0, 0)),
        ],
        out_specs=[
            pl.BlockSpec(blk, lambda b, h: (b, h)),
            pl.BlockSpec(mblk, lambda b, h: (b, h)),
        ],
        out_shape=[
            jax.ShapeDtypeStruct((batch * seq, na_width), BF16),
            jax.ShapeDtypeStruct((batch * N_META, na_width), BF16),
        ],
        scratch_shapes=[pltpu.VMEM((3, 2 * NA_QROWS * GRID_W, NA_KROWS * GRID_W), F32),
                        pltpu.VMEM((2, seq, N_META), F32),
                        pltpu.VMEM((seq, 2 * dh), F32)],
        compiler_params=_params("parallel", "parallel"),
        name="neighbourhood_attention",
    )(proj_main, proj_main, proj_main, proj_meta, proj_meta, proj_meta, bias)


def _ssm_operators(lam_re, lam_im, log_step, b_re, b_im, c_re, c_im, d_skip):
    t = SSM_CHUNK
    hp = lax.Precision.HIGHEST
    lr, li = lam_re.astype(F32), lam_im.astype(F32)
    step = jnp.exp(log_step.astype(F32))[..., None]
    dt_r, dt_i = lr * step, li * step
    g, n = lr.shape[1], lr.shape[2]
    c = b_re.shape[-1]

    def lam_bar_pow(k):
        mag = jnp.exp(dt_r[:, :, None] * k)
        return mag * jnp.cos(dt_i[:, :, None] * k), mag * jnp.sin(dt_i[:, :, None] * k)

    p_r, p_i = lam_bar_pow(jnp.arange(t + 1, dtype=F32)[:, None])
    x_r, x_i = p_r[:, :, 1] - 1.0, p_i[:, :, 1]
    den = lr * lr + li * li
    q_r, q_i = (x_r * lr + x_i * li) / den, (x_i * lr - x_r * li) / den
    bb_r = q_r[..., None] * b_re.astype(F32) - q_i[..., None] * b_im.astype(F32)
    bb_i = q_r[..., None] * b_im.astype(F32) + q_i[..., None] * b_re.astype(F32)
    cr, ci = c_re.astype(F32)[:, :, None], c_im.astype(F32)[:, :, None]
    cp_r = cr * p_r[:, :, :, None] - ci * p_i[:, :, :, None]
    cp_i = cr * p_i[:, :, :, None] + ci * p_r[:, :, :, None]

    kern = (jnp.einsum('dglcn,dgnk->dglck', cp_r[:, :, :t], bb_r, precision=hp)
            - jnp.einsum('dglcn,dgnk->dglck', cp_i[:, :, :t], bb_i, precision=hp))
    s_i = jnp.arange(t)[:, None]
    t_i = jnp.arange(t)[None, :]
    lag = jnp.abs(t_i - s_i)
    one_hot = (lag[None] == jnp.arange(t)[:, None, None]).astype(F32)
    m_f = jnp.einsum('lst,glck->gstck', one_hot * (t_i >= s_i), kern[0], precision=hp)
    m_r = jnp.einsum('lst,glck->gstck', one_hot * (s_i >= t_i), kern[1], precision=hp)
    skip = (jnp.eye(t, dtype=F32)[None, :, :, None, None]
            * (jnp.eye(c, dtype=F32)[None] * d_skip.astype(F32).reshape(g, c, 1))[:, None, None])
    m = jnp.transpose(m_f + m_r + skip, (0, 1, 4, 2, 3)).reshape(g, t * c, t * c)

    def in_map(d, pw_r, pw_i):
        b_r, b_i = jnp.swapaxes(bb_r[d], 1, 2)[:, None], jnp.swapaxes(bb_i[d], 1, 2)[:, None]
        e_r = pw_r[:, :, None] * b_r - pw_i[:, :, None] * b_i
        e_i = pw_r[:, :, None] * b_i + pw_i[:, :, None] * b_r
        return e_r.reshape(g, t * c, n), e_i.reshape(g, t * c, n)

    ef_r, ef_i = in_map(0, p_r[0][:, ::-1][:, 1:], p_i[0][:, ::-1][:, 1:])
    er_r, er_i = in_map(1, p_r[1][:, :t], p_i[1][:, :t])
    me = jnp.concatenate([m, ef_r, er_r, ef_i, er_i], axis=-1)

    def out_map(x):
        return jnp.transpose(x, (0, 3, 1, 2)).reshape(g, n, t * c)

    f = jnp.concatenate([out_map(cp_r[0][:, 1:]), out_map(cp_r[1][:, ::-1][:, :t]),
                         -out_map(cp_i[0][:, 1:]), -out_map(cp_i[1][:, ::-1][:, :t])], axis=1)

    sub = F32_SUBLANES
    row = jnp.arange(sub)

    def decay(k_rows, keep):
        a_r, a_i = lam_bar_pow((t * k_rows).astype(F32)[:, None])
        a_r = jnp.concatenate([a_r[0], a_r[1]], axis=-1) * keep[None, :, None]
        a_i = jnp.concatenate([a_i[0], a_i[1]], axis=-1) * keep[None, :, None]
        return [a_r, a_i]

    slots = []
    for k in (1, 2, 4):
        slots += decay(jnp.full((sub,), k), row >= k)
    slots += decay(row + 1, row >= 0)
    for k in (1, 2, 4):
        slots += decay(jnp.full((sub,), k), row < sub - k)
    slots += decay(sub - row, row >= 0)
    a = jnp.stack(slots, axis=1)

    lanes_per_tok = 128
    toks = lanes_per_tok // c
    grp = lanes_per_tok // c
    src = jnp.transpose(jnp.arange(toks * grp * c).reshape(toks, grp, c), (1, 0, 2)).reshape(-1)
    perm = (jnp.arange(toks * grp * c)[:, None] == src[None, :]).astype(BF16)
    return perm, me.astype(BF16), f.astype(BF16), a.astype(F32)


LANES = 128
SSM_SUB = 2
SSM_META_ROWS = BF16_SUBLANES


def _ssm_kernel(*refs, batch, nchunks, nstate, gsub):
    t = SSM_CHUNK
    (u_ref, um_ref, p_ref, me_ref, f_ref, a_ref, y_ref, ym_ref,
     yi_ref, ee_ref, pf_ref, pr_ref, ycat_ref, ymcat_ref) = refs
    gtile = me_ref.shape[0]
    tc = f_ref.shape[2]
    toks = p_ref.shape[0] // LANES
    parts = t // toks
    sub = F32_SUBLANES
    w = 2 * nstate
    tiles = nchunks // sub
    perm = p_ref[...]
    row = lax.broadcasted_iota(jnp.int32, (sub, w), 0)
    lane = lax.broadcasted_iota(jnp.int32, (1, w), 1)
    rev_lane = lane >= nstate

    def group_major(token_refs):
        out = []
        for part in range(parts):
            x = jnp.concatenate([token_refs[part * toks + i] for i in range(toks)], axis=1)
            out.append(jnp.dot(x, perm, preferred_element_type=F32).astype(BF16))
        return out

    def group_chunk(regrouped, g):
        return jnp.concatenate([x[:, g * LANES:(g + 1) * LANES] for x in regrouped], axis=1)

    v_main = group_major(u_ref)
    v_meta = group_major(um_ref)

    def cmul_add(xr, xi, ar, ai, yr, yi):
        return xr + ar * yr - ai * yi, xi + ar * yi + ai * yr

    def bcast_row(x, r):
        return jnp.broadcast_to(x[r:r + 1, :], (sub, w))

    for sb in range(gtile // gsub):
        gs = [sb * gsub + gi for gi in range(gsub)]
        meta = []
        for gi, g in enumerate(gs):
            me = jnp.dot(group_chunk(v_main, g), me_ref[g], preferred_element_type=F32)
            yi_ref[gi] = me[:, :tc]
            ee_ref[gi] = me[:, tc:]
            meta.append(jnp.dot(group_chunk(v_meta, g), me_ref[g], preferred_element_type=F32))

        def body(j, carry, gs=gs):
            new = []
            for gi, g in enumerate(gs):
                for b in range(batch):
                    lr, li, fr, fi = carry[gi * batch + b]
                    fs = pl.ds(pl.multiple_of(b * nchunks + j * sub, sub), sub)
                    xr, xi = ee_ref[gi, fs, 0:w], ee_ref[gi, fs, w:2 * w]
                    for step, k in enumerate((1, 2, 4)):
                        xr, xi = cmul_add(xr, xi, a_ref[g, 2 * step], a_ref[g, 2 * step + 1],
                                          pltpu.roll(xr, k, 0), pltpu.roll(xi, k, 0))
                    sr, si = cmul_add(xr, xi, a_ref[g, 6], a_ref[g, 7], lr, li)
                    pf_ref[gi, fs, 0:w] = jnp.where(row == 0, lr, pltpu.roll(sr, 1, 0))
                    pf_ref[gi, fs, w:2 * w] = jnp.where(row == 0, li, pltpu.roll(si, 1, 0))
                    rs = pl.ds(pl.multiple_of(b * nchunks + (tiles - 1 - j) * sub, sub), sub)
                    xr, xi = ee_ref[gi, rs, 0:w], ee_ref[gi, rs, w:2 * w]
                    for step, k in enumerate((1, 2, 4)):
                        xr, xi = cmul_add(xr, xi, a_ref[g, 8 + 2 * step], a_ref[g, 9 + 2 * step],
                                          pltpu.roll(xr, sub - k, 0), pltpu.roll(xi, sub - k, 0))
                    rr, ri = cmul_add(xr, xi, a_ref[g, 14], a_ref[g, 15], fr, fi)
                    pr_ref[gi, rs, 0:w] = jnp.where(row == sub - 1, fr, pltpu.roll(rr, sub - 1, 0))
                    pr_ref[gi, rs, w:2 * w] = jnp.where(row == sub - 1, fi, pltpu.roll(ri, sub - 1, 0))
                    new.append((bcast_row(sr, sub - 1), bcast_row(si, sub - 1), bcast_row(rr, 0), bcast_row(ri, 0)))
            return tuple(new)

        zero = jnp.zeros((sub, w), F32)
        init = tuple((bcast_row(meta[gi][:, tc:tc + w], b), bcast_row(meta[gi][:, tc + w:tc + 2 * w], b), zero, zero)
                     for gi in range(gsub) for b in range(batch))
        final = lax.fori_loop(0, tiles, body, init)

        for gi, g in enumerate(gs):
            prev = jnp.concatenate([jnp.where(rev_lane, pr_ref[gi, :, k * w:(k + 1) * w], pf_ref[gi, :, k * w:(k + 1) * w])
                                    for k in range(2)], axis=1).astype(BF16)
            y = (yi_ref[gi] + jnp.dot(prev, f_ref[g], preferred_element_type=F32)).astype(BF16)
            pm = []
            for k in (2, 3):
                x = zero
                for b in range(batch):
                    x = jnp.where((row == b) & rev_lane, final[gi * batch + b][k], x)
                pm.append(jnp.concatenate([x, jnp.zeros((SSM_META_ROWS - sub, w), F32)], axis=0))
            prev_m = jnp.concatenate(pm, axis=1).astype(BF16)
            y_m = (meta[gi][:, :tc] + jnp.dot(prev_m, f_ref[g], preferred_element_type=F32)).astype(BF16)
            for part in range(parts):
                ycat_ref[part, :, g * LANES:(g + 1) * LANES] = y[:, part * LANES:(part + 1) * LANES]
                ymcat_ref[part, :, g * LANES:(g + 1) * LANES] = y_m[:, part * LANES:(part + 1) * LANES]

    for part in range(parts):
        o = jnp.dot(ycat_ref[part], perm, preferred_element_type=F32).astype(y_ref.dtype)
        o_m = jnp.dot(ymcat_ref[part], perm, preferred_element_type=F32).astype(ym_ref.dtype)
        for i in range(toks):
            y_ref[part * toks + i] = o[:, i * LANES:(i + 1) * LANES]
            ym_ref[part * toks + i] = o_m[:, i * LANES:(i + 1) * LANES]


def s5_scan(u_main, u_meta_rows, ops, batch, seq):
    perm, me, f, a = ops
    g = me.shape[0]
    t = SSM_CHUNK
    tc = f.shape[2]
    c = tc // t
    width = g * c
    nstate = f.shape[1] // 4
    gtile = LANES // c
    nchunks = seq // t
    rows = batch * nchunks
    assert N_META == t and seq % (t * F32_SUBLANES) == 0 and batch <= F32_SUBLANES
    assert g % gtile == 0 and gtile % SSM_SUB == 0 and u_main.shape == (t, rows, width)

    u_meta = jnp.transpose(u_meta_rows.reshape(batch, t, width), (1, 0, 2))
    u_meta = jnp.pad(u_meta, ((0, 0), (0, SSM_META_ROWS - batch), (0, 0)))
    kernel = functools.partial(_ssm_kernel, batch=batch, nchunks=nchunks, nstate=nstate, gsub=SSM_SUB)
    y, y_m = pl.pallas_call(
        kernel,
        grid=(g // gtile,),
        in_specs=[
            pl.BlockSpec((t, rows, LANES), lambda i: (0, 0, i)),
            pl.BlockSpec((t, SSM_META_ROWS, LANES), lambda i: (0, 0, i)),
            pl.BlockSpec(perm.shape, lambda i: (0, 0)),
            pl.BlockSpec((gtile,) + me.shape[1:], lambda i: (i, 0, 0)),
            pl.BlockSpec((gtile,) + f.shape[1:], lambda i: (i, 0, 0)),
            pl.BlockSpec((gtile,) + a.shape[1:], lambda i: (i, 0, 0, 0)),
        ],
        out_specs=[
            pl.BlockSpec((t, rows, LANES), lambda i: (0, 0, i)),
            pl.BlockSpec((t, SSM_META_ROWS, LANES), lambda i: (0, 0, i)),
        ],
        out_shape=[
            jax.ShapeDtypeStruct((t, rows, width), BF16),
            jax.ShapeDtypeStruct((t, SSM_META_ROWS, width), BF16),
        ],
        scratch_shapes=[
            pltpu.VMEM((SSM_SUB, rows, tc), F32),
            pltpu.VMEM((SSM_SUB, rows, 4 * nstate), F32),
            pltpu.VMEM((SSM_SUB, rows, 4 * nstate), F32),
            pltpu.VMEM((SSM_SUB, rows, 4 * nstate), F32),
            pltpu.VMEM((t // (perm.shape[0] // LANES), rows, gtile * LANES), BF16),
            pltpu.VMEM((t // (perm.shape[0] // LANES), SSM_META_ROWS, gtile * LANES), BF16),
        ],
        compiler_params=_params("parallel"),
        name="s5_scan",
    )(u_main, u_meta, perm, me, f, a)

    y_meta = jnp.transpose(y_m[:, :batch], (1, 0, 2)).reshape(batch * N_META, width)
    return y, y_meta


def _mix_kernel(o_ref, y_ref, gna_ref, gssm_ref, wna_ref, wglu_ref, wssm_ref, out_ref, *stage, chunk_major):
    y_na = jnp.dot(o_ref[...], wna_ref[...], preferred_element_type=F32)
    if chunk_major:
        stage_ref, = stage
        t = y_ref.shape[0]
        ntile = stage_ref.shape[0]
        for k in range(t):
            for lt in range(ntile):
                stage_ref[lt, pl.ds(k, y_ref.shape[1], stride=t), :] = y_ref[
                    k, :, lt * LANES:(lt + 1) * LANES].astype(F32)
        y = jnp.concatenate([stage_ref[lt] for lt in range(ntile)], axis=1)
    else:
        y = y_ref[...].astype(F32)
    gl = _gelu(y)
    z = jnp.dot(gl.astype(BF16), wglu_ref[...], preferred_element_type=F32)
    t = (gl * _sigmoid(z)).astype(BF16)
    y_ssm = jnp.dot(t, wssm_ref[...], preferred_element_type=F32)
    mixed = _sigmoid(gna_ref[...].astype(F32)) * y_na + _sigmoid(gssm_ref[...].astype(F32)) * y_ssm
    out_ref[...] = mixed.astype(out_ref.dtype)


def mix_branches(o_na, y, proj, w_na, w_glu, w_ssm, gate_col):
    rows, na_width = o_na.shape
    ssm_width = y.shape[-1]
    d = w_na.shape[1]
    tm = _pick(rows, MID_ROW_TILES)
    assert gate_col % d == 0
    gblk = gate_col // d
    resident = dict(pipeline_mode=pl.Buffered(1))
    chunk_major = y.ndim == 3
    if chunk_major:
        t = y.shape[0]
        y_spec = pl.BlockSpec((t, tm // t, ssm_width), lambda i: (0, i, 0))
        scratch = [pltpu.VMEM((ssm_width // LANES, tm, LANES), F32)]
    else:
        y_spec = pl.BlockSpec((tm, ssm_width), lambda i: (i, 0))
        scratch = []
    return pl.pallas_call(
        functools.partial(_mix_kernel, chunk_major=chunk_major),
        grid=(rows // tm,),
        scratch_shapes=scratch,
        in_specs=[
            pl.BlockSpec((tm, na_width), lambda i: (i, 0)),
            y_spec,
            pl.BlockSpec((tm, d), lambda i: (i, gblk)),
            pl.BlockSpec((tm, d), lambda i: (i, gblk + 1)),
            pl.BlockSpec(w_na.shape, lambda i: (0, 0), **resident),
            pl.BlockSpec(w_glu.shape, lambda i: (0, 0), **resident),
            pl.BlockSpec(w_ssm.shape, lambda i: (0, 0), **resident),
        ],
        out_specs=pl.BlockSpec((tm, d), lambda i: (i, 0)),
        out_shape=jax.ShapeDtypeStruct((rows, d), BF16),
        compiler_params=_params("parallel"),
        name="mix_branches",
    )(o_na, y, proj, proj, w_na, w_glu, w_ssm)


def _residual_matmul_kernel(h_ref, a_ref, w_ref, o_ref):
    o_ref[...] = h_ref[...] + jnp.dot(a_ref[...], w_ref[...], preferred_element_type=F32)


def residual_matmul(h, a, w):
    rows, d = h.shape
    tm = _pick(rows, MID_ROW_TILES)
    return pl.pallas_call(
        _residual_matmul_kernel,
        grid=(rows // tm,),
        in_specs=[
            pl.BlockSpec((tm, d), lambda i: (i, 0)),
            pl.BlockSpec((tm, a.shape[1]), lambda i: (i, 0)),
            pl.BlockSpec(w.shape, lambda i: (0, 0), pipeline_mode=pl.Buffered(1)),
        ],
        out_specs=pl.BlockSpec((tm, d), lambda i: (i, 0)),
        out_shape=jax.ShapeDtypeStruct((rows, d), F32),
        compiler_params=_params("parallel"),
        name="residual_out_proj",
    )(h, a, w)


HALO = BF16_SUBLANES


def _ffn_up_kernel(h_ref, prev_ref, next_ref, g_ref, wa_ref, wg_ref, cw_ref, cb_ref, o_ref, hn_ref, *, tm):
    @pl.when(pl.program_id(1) == 0)
    def _():
        g = g_ref[...]
        hn_ref[0:HALO, :] = _rms(prev_ref[...], g).astype(BF16)
        hn_ref[HALO:HALO + tm, :] = _rms(h_ref[...], g).astype(BF16)
        hn_ref[HALO + tm:, :] = _rms(next_ref[...], g).astype(BF16)

    ext = tm + 2 * HALO
    a = jnp.dot(hn_ref[...], wa_ref[...], preferred_element_type=F32)
    gate = jnp.dot(hn_ref[HALO:HALO + tm, :], wg_ref[...], preferred_element_type=F32)
    a_prev = pltpu.roll(a, 1, 0)[HALO:HALO + tm]
    a_next = pltpu.roll(a, ext - 1, 0)[HALO:HALO + tm]
    conv = a_prev * cw_ref[0:1, :] + a[HALO:HALO + tm] * cw_ref[1:2, :] + a_next * cw_ref[2:3, :] + cb_ref[...]
    o_ref[...] = (_gelu(conv) * gate).astype(o_ref.dtype)


def ffn_up(h, halo_prev, halo_next, g, w_up, conv_w, conv_b, tm):
    rows, d = h.shape
    dff = conv_b.shape[0]
    tn = _pick(dff, COL_TILES)
    nj = dff // tn
    kernel = functools.partial(_ffn_up_kernel, tm=tm)
    return pl.pallas_call(
        kernel,
        grid=(rows // tm, nj),
        in_specs=[
            pl.BlockSpec((tm, d), lambda i, j: (i, 0)),
            pl.BlockSpec((HALO, d), lambda i, j: (i, 0)),
            pl.BlockSpec((HALO, d), lambda i, j: (i, 0)),
            pl.BlockSpec((1, d), lambda i, j: (0, 0)),
            pl.BlockSpec((d, tn), lambda i, j: (0, j)),
            pl.BlockSpec((d, tn), lambda i, j: (0, nj + j)),
            pl.BlockSpec((conv_w.shape[0], tn), lambda i, j: (0, j)),
            pl.BlockSpec((1, tn), lambda i, j: (0, j)),
        ],
        out_specs=pl.BlockSpec((tm, tn), lambda i, j: (i, j)),
        out_shape=jax.ShapeDtypeStruct((rows, dff), BF16),
        scratch_shapes=[pltpu.VMEM((tm + 2 * HALO, d), BF16)],
        compiler_params=_params("parallel", "arbitrary"),
        name="ffn_up_conv_gate",
    )(h, halo_prev, halo_next, g.reshape(1, d), w_up, w_up, conv_w, conv_b.reshape(1, dff))


def _ffn_down_kernel(h_ref, a_ref, w_ref, g_ref, o_ref, acc_ref):
    k = pl.program_id(1)

    @pl.when(k == 0)
    def _():
        acc_ref[...] = h_ref[...]

    acc_ref[...] += jnp.dot(a_ref[...], w_ref[...], preferred_element_type=F32)

    @pl.when(k == pl.num_programs(1) - 1)
    def _():
        o_ref[...] = _rms(acc_ref[...], g_ref[...])


def ffn_down(h, act, w_down, g):
    rows, d = h.shape
    dff = act.shape[1]
    tm = _pick(rows, ROW_TILES)
    tk = _pick(dff, COL_TILES)
    return pl.pallas_call(
        _ffn_down_kernel,
        grid=(rows // tm, dff // tk),
        in_specs=[
            pl.BlockSpec((tm, d), lambda i, k: (i, 0)),
            pl.BlockSpec((tm, tk), lambda i, k: (i, k)),
            pl.BlockSpec((tk, d), lambda i, k: (k, 0)),
            pl.BlockSpec((1, d), lambda i, k: (0, 0)),
        ],
        out_specs=pl.BlockSpec((tm, d), lambda i, k: (i, 0)),
        out_shape=jax.ShapeDtypeStruct((rows, d), F32),
        scratch_shapes=[pltpu.VMEM((tm, d), F32)],
        compiler_params=_params("parallel", "arbitrary"),
        name="ffn_down_final_norm",
    )(h, act, w_down, g.reshape(1, d))


def _conv_halos(h1_main, h1_meta, batch, seq, tm):
    d = h1_main.shape[1]
    per_seq = seq // tm
    hm = h1_main.reshape(batch, per_seq, tm, d)
    meta_tail = h1_meta.reshape(batch, 1, N_META, d)[:, :, N_META - HALO:]
    prev = jnp.concatenate([meta_tail, hm[:, :-1, tm - HALO:]], axis=1)
    nxt = jnp.concatenate([hm[:, 1:, :HALO], jnp.zeros((batch, 1, HALO, d), h1_main.dtype)], axis=1)
    return prev.reshape(batch * per_seq * HALO, d), nxt.reshape(batch * per_seq * HALO, d)


def kernel(x, meta_tokens, norm1_g, w_in, na_rpb, ssm_lam_re, ssm_lam_im, ssm_log_step, ssm_b_re, ssm_b_im,
           ssm_c_re, ssm_c_im, ssm_d, w_glu, w_proj_na, w_proj_ssm, w_out, norm2_g, w_up, conv_w, conv_b,
           w_down, final_g):
    batch, seq, d = x.shape
    depth = w_in.shape[0]
    na_width = w_proj_na.shape[1]
    ssm_width = w_proj_ssm.shape[1]
    assert depth == 1 and N_META >= HALO
    l = 0
    h_main = x.reshape(batch * seq, d)
    h_meta = jnp.broadcast_to(meta_tokens.astype(x.dtype)[None], (batch, N_META, d)).reshape(batch * N_META, d)

    w_in_b = w_in[l].astype(BF16)
    u_col = 3 * na_width
    proj_main, u_main = norm_matmul(h_main, norm1_g[l], w_in_b, chunk_major=(u_col, ssm_width))
    proj_meta = norm_matmul(h_meta, norm1_g[l], w_in_b)

    o_main, o_meta = neighbourhood_attention(proj_main, proj_meta, na_rpb[l], batch, seq, na_width)

    ops = _ssm_operators(ssm_lam_re[l], ssm_lam_im[l], ssm_log_step[l], ssm_b_re[l], ssm_b_im[l],
                         ssm_c_re[l], ssm_c_im[l], ssm_d[l])
    y_main, y_meta = s5_scan(u_main, proj_meta[:, u_col:u_col + ssm_width], ops, batch, seq)

    w_na_b, w_glu_b, w_ssm_b = w_proj_na[l].astype(BF16), w_glu[l].astype(BF16), w_proj_ssm[l].astype(BF16)
    w_out_b = w_out[l].astype(BF16)
    gate_col = u_col + ssm_width
    h1 = []
    for h, o, y, proj in ((h_main, o_main, y_main, proj_main), (h_meta, o_meta, y_meta, proj_meta)):
        mixed = mix_branches(o, y, proj, w_na_b, w_glu_b, w_ssm_b, gate_col)
        h1.append(residual_matmul(h, mixed, w_out_b))
    h1_main, h1_meta = h1

    tm = _pick(seq, ROW_TILES)
    halo_prev, halo_next = _conv_halos(h1_main, h1_meta, batch, seq, tm)
    act = ffn_up(h1_main, halo_prev, halo_next, norm2_g[l], w_up[l].astype(BF16), conv_w[l], conv_b[l], tm)
    out = ffn_down(h1_main, act, w_down[l].astype(BF16), final_g)
    return out.reshape(batch, seq, d)
```

```python
import functools
import math

import jax
import jax.numpy as jnp
from jax import lax
from jax.experimental import pallas as pl
from jax.experimental.pallas import tpu as pltpu

F32 = jnp.float32
BF16 = jnp.bfloat16

N_META = 16
GRID_W = 64
ROW_WIN = 8
COL_WIN = 16
RMS_EPS = 1e-6
SSM_CHUNK = 16
MASK_BIAS = -1e30
NA_QROWS = 4
NA_KROWS = 12
NA_UNROLL = 4

VMEM_LIMIT_BYTES = 56 * 1024 * 1024
F32_SUBLANES = 8
BF16_SUBLANES = 16
LANES = 128

SSM_SUB = 2
SSM_SCAN_UNROLL = True
SSM_META_ROWS = BF16_SUBLANES

ROW_TILES = (1024, 512, 256, 128, 64)
MID_ROW_TILES = (512, 256, 128, 64)
COL_TILES = (512, 256, 128)
WIDE_COL_TILES = (1024, 512, 256, 128)


def _pick(dim, prefs):
    for t in prefs:
        if dim % t == 0:
            return t
    return dim


def _params(*sem):
    return pltpu.CompilerParams(dimension_semantics=sem, vmem_limit_bytes=VMEM_LIMIT_BYTES)


def _rms(x, g):
    ms = jnp.mean(x * x, axis=-1, keepdims=True)
    return x * lax.rsqrt(ms + RMS_EPS) * g


def _gelu(x):
    c = math.sqrt(2.0 / math.pi)
    return 0.5 * x * (1.0 + jnp.tanh(c * (x + 0.044715 * (x * x * x))))


def _sigmoid(x):
    return 1.0 / (1.0 + jnp.exp(-x))


def _norm_matmul_kernel(x_ref, g_ref, w_ref, o_ref, *rest, chunk_cols):
    hn_ref = rest[-1] if chunk_cols is None else rest[1]

    @pl.when(pl.program_id(1) == 0)
    def _():
        hn_ref[...] = _rms(x_ref[...], g_ref[...]).astype(BF16)

    r = jnp.dot(hn_ref[...], w_ref[...], preferred_element_type=F32)
    o_ref[...] = r.astype(o_ref.dtype)

    if chunk_cols is not None:
        u3_ref, _, stage_ref = rest
        tile, off, width = chunk_cols
        t = SSM_CHUNK

        @pl.when(pl.program_id(1) == tile)
        def _():
            for lt in range(width // LANES):
                stage_ref[lt] = r[:, off + lt * LANES:off + (lt + 1) * LANES]
            for k in range(t):
                for lt in range(width // LANES):
                    u3_ref[k, :, lt * LANES:(lt + 1) * LANES] = stage_ref[
                        lt, pl.ds(k, stage_ref.shape[1] // t, stride=t), :].astype(u3_ref.dtype)


def norm_matmul(x, g, w, chunk_major=None):
    rows, d = x.shape
    n = w.shape[1]
    tm = _pick(rows, ROW_TILES)
    tn = _pick(n, WIDE_COL_TILES)
    out_specs = [pl.BlockSpec((tm, tn), lambda i, j: (i, j))]
    out_shape = [jax.ShapeDtypeStruct((rows, n), BF16)]
    scratch = [pltpu.VMEM((tm, d), BF16)]
    chunk_cols = None
    if chunk_major is not None:
        col, width = chunk_major
        t = SSM_CHUNK
        chunk_cols = (col // tn, col % tn, width)
        assert col % tn + width <= tn and tm % (t * F32_SUBLANES) == 0
        out_specs.append(pl.BlockSpec((t, tm // t, width), lambda i, j: (0, i, 0)))
        out_shape.append(jax.ShapeDtypeStruct((t, rows // t, width), BF16))
        scratch.append(pltpu.VMEM((width // LANES, tm, LANES), F32))
    out = pl.pallas_call(
        functools.partial(_norm_matmul_kernel, chunk_cols=chunk_cols),
        grid=(rows // tm, n // tn),
        in_specs=[
            pl.BlockSpec((tm, d), lambda i, j: (i, 0)),
            pl.BlockSpec((1, d), lambda i, j: (0, 0)),
            pl.BlockSpec((d, tn), lambda i, j: (0, j)),
        ],
        out_specs=out_specs,
        out_shape=out_shape,
        scratch_shapes=scratch,
        compiler_params=_params("parallel", "arbitrary"),
        name="norm_in_proj",
    )(x, g.reshape(1, d), w)
    return out[0] if chunk_major is None else out


def _na_kernel(q_ref, k_ref, v_ref, qm_ref, km_ref, vm_ref, tab_ref, o_ref, om_ref, bias_ref, sm_ref, acc_ref,
               *, rows, dh):
    lane = lax.broadcasted_iota(jnp.int32, (1, 2 * dh), 1)
    head_masks = (lane < dh, lane >= dh)
    scale = dh ** -0.5
    km = km_ref[...]
    vm = vm_ref[...]
    nt = (((1,), (1,)), ((), ()))

    masked = 2 * ROW_WIN - 1
    nblk = rows // NA_QROWS
    qblk = NA_QROWS * GRID_W
    kblk = NA_KROWS * GRID_W

    def rel_row(kind, j, i):
        if kind == 0:
            return i - j + (ROW_WIN - 1) if i < ROW_WIN else masked
        if kind == 1:
            return i - j + (ROW_WIN - 1 - ROW_WIN // 2) if j <= i < j + ROW_WIN else masked
        lo = NA_KROWS - ROW_WIN
        return i - j + (NA_QROWS - NA_KROWS) + (ROW_WIN - 1) if i >= lo else masked

    first_row = lane < GRID_W
    for kind in range(3):
        for hh in range(2):
            for j in range(NA_QROWS):
                r_lo = (hh * NA_QROWS + j) * GRID_W
                for i2 in range(NA_KROWS // 2):
                    bias_ref[kind, r_lo:r_lo + GRID_W, i2 * 2 * GRID_W:(i2 + 1) * 2 * GRID_W] = jnp.where(
                        first_row, tab_ref[hh, rel_row(kind, j, 2 * i2)], tab_ref[hh, rel_row(kind, j, 2 * i2 + 1)])

    def one_head(x, hh):
        return jnp.where(head_masks[hh], x, jnp.zeros_like(x))

    qm = (qm_ref[...] * scale).astype(BF16)
    om = None
    for hh in range(2):
        s = lax.dot_general(one_head(qm, hh), km, nt, preferred_element_type=F32)
        p = jnp.exp(s - jnp.max(s, axis=-1, keepdims=True))
        o_h = jnp.dot(p.astype(BF16), vm, preferred_element_type=F32) / jnp.sum(p, axis=-1, keepdims=True)
        om = o_h if om is None else jnp.where(head_masks[1], o_h, om)
    om_ref[...] = om.astype(om_ref.dtype)

    q_all = (q_ref[...] * scale).astype(BF16)
    for hh in range(2):
        sm_ref[hh] = lax.dot_general(one_head(q_all, hh), km, nt, preferred_element_type=F32)

    def body(blk, carry):
        r = blk * NA_QROWS
        k0 = jnp.clip(r - ROW_WIN // 2, 0, rows - NA_KROWS)
        kind = jnp.where(blk == 0, 0, jnp.where(blk == nblk - 1, 2, 1))
        qrows = pl.ds(pl.multiple_of(r * GRID_W, qblk), qblk)
        krows = pl.ds(pl.multiple_of(k0 * GRID_W, GRID_W), kblk)
        qs = (q_ref[qrows, :] * scale).astype(BF16)
        q2 = jnp.concatenate([one_head(qs, 0), one_head(qs, 1)], axis=0)
        s = lax.dot_general(q2, k_ref[krows, :], nt, preferred_element_type=F32) + bias_ref[kind]
        s_m = jnp.concatenate([sm_ref[0, qrows, :], sm_ref[1, qrows, :]], axis=0)
        m = jnp.maximum(jnp.max(s, axis=-1, keepdims=True), jnp.max(s_m, axis=-1, keepdims=True))
        p = jnp.exp(s - m)
        p_m = jnp.exp(s_m - m)
        inv = 1.0 / (jnp.sum(p, axis=-1, keepdims=True) + jnp.sum(p_m, axis=-1, keepdims=True))
        acc = jnp.dot(p.astype(BF16), v_ref[krows, :], preferred_element_type=F32) * inv
        acc_ref[qrows, :] = jnp.where(head_masks[1], acc[qblk:], acc[:qblk])
        p_m = p_m * inv
        sm_ref[0, qrows, :] = p_m[:qblk]
        sm_ref[1, qrows, :] = p_m[qblk:]
        return carry

    lax.fori_loop(0, nblk, body, 0, unroll=NA_UNROLL)

    o_meta = [jnp.dot(sm_ref[hh].astype(BF16), vm, preferred_element_type=F32) for hh in range(2)]
    o_ref[...] = (acc_ref[...] + jnp.where(head_masks[1], o_meta[1], o_meta[0])).astype(o_ref.dtype)


def _na_bias_table(rpb):
    c = jnp.arange(GRID_W)[:, None]
    kc = jnp.arange(GRID_W)[None, :]
    col_start = jnp.clip(c - COL_WIN // 2, 0, GRID_W - COL_WIN)
    valid = (kc >= col_start) & (kc < col_start + COL_WIN)
    dc = kc - c + (COL_WIN - 1)
    pick = (dc.reshape(-1)[None] == jnp.arange(2 * COL_WIN - 1)[:, None]).astype(F32)
    tab = jnp.dot(rpb.astype(F32).reshape(-1, 2 * COL_WIN - 1), pick, precision=lax.Precision.HIGHEST)
    tab = tab.reshape(rpb.shape[0], 2 * ROW_WIN - 1, GRID_W, GRID_W)
    tab = jnp.where(valid, tab, MASK_BIAS)
    tab = jnp.concatenate([tab, jnp.full_like(tab[:, :1], MASK_BIAS)], axis=1)
    return jnp.concatenate([tab, tab], axis=-1)


def neighbourhood_attention(proj_main, proj_meta, rpb, batch, seq, na_width):
    heads = rpb.shape[0]
    dh = na_width // heads
    rows = seq // GRID_W
    assert rows >= NA_KROWS and rows % NA_QROWS == 0 and heads % 2 == 0 and 2 * dh == LANES
    assert NA_KROWS % 2 == 0 and NA_KROWS >= ROW_WIN + NA_QROWS - 1 and NA_QROWS <= ROW_WIN // 2
    pairs = heads // 2
    bias = _na_bias_table(rpb)
    blk = (seq, 2 * dh)
    mblk = (N_META, 2 * dh)
    kernel = functools.partial(_na_kernel, rows=rows, dh=dh)
    return pl.pallas_call(
        kernel,
        grid=(batch, pairs),
        in_specs=[
            pl.BlockSpec(blk, lambda b, h: (b, h)),
            pl.BlockSpec(blk, lambda b, h: (b, pairs + h)),
            pl.BlockSpec(blk, lambda b, h: (b, 2 * pairs + h)),
            pl.BlockSpec(mblk, lambda b, h: (b, h)),
            pl.BlockSpec(mblk, lambda b, h: (b, pairs + h)),
            pl.BlockSpec(mblk, lambda b, h: (b, 2 * pairs + h)),
            pl.BlockSpec((2, 2 * ROW_WIN, GRID_W, 2 * GRID_W), lambda b, h: (h, 0, 0, 0)),
        ],
        out_specs=[
            pl.BlockSpec(blk, lambda b, h: (b, h)),
            pl.BlockSpec(mblk, lambda b, h: (b, h)),
        ],
        out_shape=[
            jax.ShapeDtypeStruct((batch * seq, na_width), BF16),
            jax.ShapeDtypeStruct((batch * N_META, na_width), BF16),
        ],
        scratch_shapes=[pltpu.VMEM((3, 2 * NA_QROWS * GRID_W, NA_KROWS * GRID_W), F32),
                        pltpu.VMEM((2, seq, N_META), F32),
                        pltpu.VMEM((seq, 2 * dh), F32)],
        compiler_params=_params("parallel", "parallel"),
        name="neighbourhood_attention",
    )(proj_main, proj_main, proj_main, proj_meta, proj_meta, proj_meta, bias)


def _ssm_operators(lam_re, lam_im, log_step, b_re, b_im, c_re, c_im, d_skip):
    t = SSM_CHUNK
    hp = lax.Precision.HIGHEST
    lr, li = lam_re.astype(F32), lam_im.astype(F32)
    step = jnp.exp(log_step.astype(F32))[..., None]
    dt_r, dt_i = lr * step, li * step
    g, n = lr.shape[1], lr.shape[2]
    c = b_re.shape[-1]

    def lam_bar_pow(k):
        mag = jnp.exp(dt_r[:, :, None] * k)
        return mag * jnp.cos(dt_i[:, :, None] * k), mag * jnp.sin(dt_i[:, :, None] * k)

    p_r, p_i = lam_bar_pow(jnp.arange(t + 1, dtype=F32)[:, None])
    x_r, x_i = p_r[:, :, 1] - 1.0, p_i[:, :, 1]
    den = lr * lr + li * li
    q_r, q_i = (x_r * lr + x_i * li) / den, (x_i * lr - x_r * li) / den
    bt_r, bt_i = jnp.swapaxes(b_re.astype(F32), 2, 3), jnp.swapaxes(b_im.astype(F32), 2, 3)
    bb_r = q_r[:, :, None] * bt_r - q_i[:, :, None] * bt_i
    bb_i = q_r[:, :, None] * bt_i + q_i[:, :, None] * bt_r
    ct_r, ct_i = jnp.swapaxes(c_re.astype(F32), 2, 3), jnp.swapaxes(c_im.astype(F32), 2, 3)

    q_pos = jnp.arange(t * c) // c
    q_ch = jnp.arange(t * c) % c
    spread_pos = (jnp.arange(t)[:, None] == q_pos[None]).astype(F32)
    spread_ch = (jnp.arange(c)[:, None] == q_ch[None]).astype(F32)
    cx_r = jnp.einsum('dgnc,cq->dgnq', ct_r, spread_ch, precision=hp)
    cx_i = jnp.einsum('dgnc,cq->dgnq', ct_i, spread_ch, precision=hp)

    def c_times_pow(d, pw_r, pw_i):
        e_r = jnp.einsum('gln,lq->gnq', pw_r, spread_pos, precision=hp)
        e_i = jnp.einsum('gln,lq->gnq', pw_i, spread_pos, precision=hp)
        return cx_r[d] * e_r - cx_i[d] * e_i, cx_r[d] * e_i + cx_i[d] * e_r

    def lag_rows(d, pw_r, pw_i):
        z_r, z_i = c_times_pow(d, pw_r, pw_i)
        return (jnp.einsum('gkn,gnq->gkq', bb_r[d], z_r, precision=hp)
                - jnp.einsum('gkn,gnq->gkq', bb_i[d], z_i, precision=hp))

    k_f = lag_rows(0, p_r[0][:, :t], p_i[0][:, :t])
    k_r = lag_rows(1, p_r[1][:, :t][:, ::-1], p_i[1][:, :t][:, ::-1])
    room = (t - 1) * c
    k_f = jnp.pad(k_f, ((0, 0), (0, 0), (room, 0)))
    k_r = jnp.pad(k_r, ((0, 0), (0, 0), (0, room)))
    m_f = jnp.stack([k_f[:, :, room - s * c:room - s * c + t * c] for s in range(t)], axis=1)
    m_r = jnp.stack([k_r[:, :, room - s * c:room - s * c + t * c] for s in range(t)], axis=1)
    d_lane = jnp.tile(d_skip.astype(F32).reshape(g, 1, c), (1, 1, t))
    skip = jnp.eye(t * c, dtype=F32)[None] * d_lane
    m = ((m_f + m_r).reshape(g, t * c, t * c) + skip).astype(BF16)

    def in_map(d, pw_r, pw_i):
        e_r = jnp.einsum('sq,gsn->gqn', spread_pos, pw_r, precision=hp)
        e_i = jnp.einsum('sq,gsn->gqn', spread_pos, pw_i, precision=hp)
        b_r = jnp.einsum('cq,gcn->gqn', spread_ch, bb_r[d], precision=hp)
        b_i = jnp.einsum('cq,gcn->gqn', spread_ch, bb_i[d], precision=hp)
        return (e_r * b_r - e_i * b_i).astype(BF16), (e_r * b_i + e_i * b_r).astype(BF16)

    ef_r, ef_i = in_map(0, p_r[0][:, ::-1][:, 1:], p_i[0][:, ::-1][:, 1:])
    er_r, er_i = in_map(1, p_r[1][:, :t], p_i[1][:, :t])
    me = jnp.concatenate([m, ef_r, er_r, ef_i, er_i], axis=-1)

    gf_r, gf_i = c_times_pow(0, p_r[0][:, 1:], p_i[0][:, 1:])
    gr_r, gr_i = c_times_pow(1, p_r[1][:, ::-1][:, :t], p_i[1][:, ::-1][:, :t])
    f = jnp.concatenate([gf_r, gr_r, -gf_i, -gr_i], axis=1)

    sub = F32_SUBLANES
    row = jnp.arange(sub)

    def decay(k_rows, keep):
        a_r, a_i = lam_bar_pow((t * k_rows).astype(F32)[:, None])
        a_r = jnp.concatenate([a_r[0], a_r[1]], axis=-1) * keep[None, :, None]
        a_i = jnp.concatenate([a_i[0], a_i[1]], axis=-1) * keep[None, :, None]
        return [a_r, a_i]

    slots = []
    for k in (1, 2, 4):
        slots += decay(jnp.full((sub,), k), row >= k)
    slots += decay(row + 1, row >= 0)
    for k in (1, 2, 4):
        slots += decay(jnp.full((sub,), k), row < sub - k)
    slots += decay(sub - row, row >= 0)
    a = jnp.stack(slots, axis=1)

    lanes_per_tok = 128
    toks = lanes_per_tok // c
    grp = lanes_per_tok // c
    src = jnp.transpose(jnp.arange(toks * grp * c).reshape(toks, grp, c), (1, 0, 2)).reshape(-1)
    perm = (jnp.arange(toks * grp * c)[:, None] == src[None, :]).astype(BF16)
    return perm, me.astype(BF16), f.astype(BF16), a.astype(F32)


def _ssm_kernel(*refs, batch, nchunks, nstate, gsub):
    t = SSM_CHUNK
    (u_ref, um_ref, p_ref, me_ref, f_ref, a_ref, y_ref, ym_ref,
     yi_ref, ee_ref, pf_ref, pr_ref, ycat_ref, ymcat_ref) = refs
    gtile = me_ref.shape[0]
    tc = f_ref.shape[2]
    toks = p_ref.shape[0] // LANES
    parts = t // toks
    sub = F32_SUBLANES
    w = 2 * nstate
    tiles = nchunks // sub
    perm = p_ref[...]
    row = lax.broadcasted_iota(jnp.int32, (sub, w), 0)
    lane = lax.broadcasted_iota(jnp.int32, (1, w), 1)
    rev_lane = lane >= nstate

    def group_major(token_refs):
        out = []
        for part in range(parts):
            x = jnp.concatenate([token_refs[part * toks + i] for i in range(toks)], axis=1)
            out.append(jnp.dot(x, perm, preferred_element_type=F32).astype(BF16))
        return out

    def group_chunk(regrouped, g):
        return jnp.concatenate([x[:, g * LANES:(g + 1) * LANES] for x in regrouped], axis=1)

    v_main = group_major(u_ref)
    v_meta = group_major(um_ref)

    def cmul_add(xr, xi, ar, ai, yr, yi):
        return xr + ar * yr - ai * yi, xi + ar * yi + ai * yr

    def bcast_row(x, r):
        return jnp.broadcast_to(x[r:r + 1, :], (sub, w))

    for sb in range(gtile // gsub):
        gs = [sb * gsub + gi for gi in range(gsub)]
        meta = []
        for gi, g in enumerate(gs):
            me = jnp.dot(group_chunk(v_main, g), me_ref[g], preferred_element_type=F32)
            yi_ref[gi] = me[:, :tc]
            ee_ref[gi] = me[:, tc:]
            meta.append(jnp.dot(group_chunk(v_meta, g), me_ref[g], preferred_element_type=F32))

        def body(j, carry, gs=gs):
            new = []
            for gi, g in enumerate(gs):
                for b in range(batch):
                    lr, li, fr, fi = carry[gi * batch + b]
                    fs = pl.ds(pl.multiple_of(b * nchunks + j * sub, sub), sub)
                    xr, xi = ee_ref[gi, fs, 0:w], ee_ref[gi, fs, w:2 * w]
                    for step, k in enumerate((1, 2, 4)):
                        xr, xi = cmul_add(xr, xi, a_ref[g, 2 * step], a_ref[g, 2 * step + 1],
                                          pltpu.roll(xr, k, 0), pltpu.roll(xi, k, 0))
                    sr, si = cmul_add(xr, xi, a_ref[g, 6], a_ref[g, 7], lr, li)
                    pf_ref[gi, fs, 0:w] = jnp.where(row == 0, lr, pltpu.roll(sr, 1, 0))
                    pf_ref[gi, fs, w:2 * w] = jnp.where(row == 0, li, pltpu.roll(si, 1, 0))
                    rs = pl.ds(pl.multiple_of(b * nchunks + (tiles - 1 - j) * sub, sub), sub)
                    xr, xi = ee_ref[gi, rs, 0:w], ee_ref[gi, rs, w:2 * w]
                    for step, k in enumerate((1, 2, 4)):
                        xr, xi = cmul_add(xr, xi, a_ref[g, 8 + 2 * step], a_ref[g, 9 + 2 * step],
                                          pltpu.roll(xr, sub - k, 0), pltpu.roll(xi, sub - k, 0))
                    rr, ri = cmul_add(xr, xi, a_ref[g, 14], a_ref[g, 15], fr, fi)
                    pr_ref[gi, rs, 0:w] = jnp.where(row == sub - 1, fr, pltpu.roll(rr, sub - 1, 0))
                    pr_ref[gi, rs, w:2 * w] = jnp.where(row == sub - 1, fi, pltpu.roll(ri, sub - 1, 0))
                    new.append((bcast_row(sr, sub - 1), bcast_row(si, sub - 1), bcast_row(rr, 0), bcast_row(ri, 0)))
            return tuple(new)

        zero = jnp.zeros((sub, w), F32)
        init = tuple((bcast_row(meta[gi][:, tc:tc + w], b), bcast_row(meta[gi][:, tc + w:tc + 2 * w], b), zero, zero)
                     for gi in range(gsub) for b in range(batch))
        final = lax.fori_loop(0, tiles, body, init, unroll=SSM_SCAN_UNROLL)

        for gi, g in enumerate(gs):
            prev = jnp.concatenate([jnp.where(rev_lane, pr_ref[gi, :, k * w:(k + 1) * w], pf_ref[gi, :, k * w:(k + 1) * w])
                                    for k in range(2)], axis=1).astype(BF16)
            y = (yi_ref[gi] + jnp.dot(prev, f_ref[g], preferred_element_type=F32)).astype(BF16)
            pm = []
            for k in (2, 3):
                x = zero
                for b in range(batch):
                    x = jnp.where((row == b) & rev_lane, final[gi * batch + b][k], x)
                pm.append(jnp.concatenate([x, jnp.zeros((SSM_META_ROWS - sub, w), F32)], axis=0))
            prev_m = jnp.concatenate(pm, axis=1).astype(BF16)
            y_m = (meta[gi][:, :tc] + jnp.dot(prev_m, f_ref[g], preferred_element_type=F32)).astype(BF16)
            for part in range(parts):
                ycat_ref[part, :, g * LANES:(g + 1) * LANES] = y[:, part * LANES:(part + 1) * LANES]
                ymcat_ref[part, :, g * LANES:(g + 1) * LANES] = y_m[:, part * LANES:(part + 1) * LANES]

    for part in range(parts):
        o = jnp.dot(ycat_ref[part], perm, preferred_element_type=F32).astype(y_ref.dtype)
        o_m = jnp.dot(ymcat_ref[part], perm, preferred_element_type=F32).astype(ym_ref.dtype)
        for i in range(toks):
            y_ref[part * toks + i] = o[:, i * LANES:(i + 1) * LANES]
            ym_ref[part * toks + i] = o_m[:, i * LANES:(i + 1) * LANES]


def s5_scan(u_main, u_meta_rows, ops, batch, seq):
    perm, me, f, a = ops
    g = me.shape[0]
    t = SSM_CHUNK
    tc = f.shape[2]
    c = tc // t
    width = g * c
    nstate = f.shape[1] // 4
    gtile = LANES // c
    nchunks = seq // t
    rows = batch * nchunks
    assert N_META == t and seq % (t * F32_SUBLANES) == 0 and batch <= F32_SUBLANES
    assert g % gtile == 0 and gtile % SSM_SUB == 0 and u_main.shape == (t, rows, width)

    u_meta = jnp.transpose(u_meta_rows.reshape(batch, t, width), (1, 0, 2))
    u_meta = jnp.pad(u_meta, ((0, 0), (0, SSM_META_ROWS - batch), (0, 0)))
    kernel = functools.partial(_ssm_kernel, batch=batch, nchunks=nchunks, nstate=nstate, gsub=SSM_SUB)
    y, y_m = pl.pallas_call(
        kernel,
        grid=(g // gtile,),
        in_specs=[
            pl.BlockSpec((t, rows, LANES), lambda i: (0, 0, i)),
            pl.BlockSpec((t, SSM_META_ROWS, LANES), lambda i: (0, 0, i)),
            pl.BlockSpec(perm.shape, lambda i: (0, 0)),
            pl.BlockSpec((gtile,) + me.shape[1:], lambda i: (i, 0, 0)),
            pl.BlockSpec((gtile,) + f.shape[1:], lambda i: (i, 0, 0)),
            pl.BlockSpec((gtile,) + a.shape[1:], lambda i: (i, 0, 0, 0)),
        ],
        out_specs=[
            pl.BlockSpec((t, rows, LANES), lambda i: (0, 0, i)),
            pl.BlockSpec((t, SSM_META_ROWS, LANES), lambda i: (0, 0, i)),
        ],
        out_shape=[
            jax.ShapeDtypeStruct((t, rows, width), BF16),
            jax.ShapeDtypeStruct((t, SSM_META_ROWS, width), BF16),
        ],
        scratch_shapes=[
            pltpu.VMEM((SSM_SUB, rows, tc), F32),
            pltpu.VMEM((SSM_SUB, rows, 4 * nstate), F32),
            pltpu.VMEM((SSM_SUB, rows, 4 * nstate), F32),
            pltpu.VMEM((SSM_SUB, rows, 4 * nstate), F32),
            pltpu.VMEM((t // (perm.shape[0] // LANES), rows, gtile * LANES), BF16),
            pltpu.VMEM((t // (perm.shape[0] // LANES), SSM_META_ROWS, gtile * LANES), BF16),
        ],
        compiler_params=_params("parallel"),
        name="s5_scan",
    )(u_main, u_meta, perm, me, f, a)

    y_meta = jnp.transpose(y_m[:, :batch], (1, 0, 2)).reshape(batch * N_META, width)
    return y, y_meta


def _mix_kernel(o_ref, y_ref, gna_ref, gssm_ref, wna_ref, wglu_ref, wssm_ref, out_ref, *stage, chunk_major):
    y_na = jnp.dot(o_ref[...], wna_ref[...], preferred_element_type=F32)
    if chunk_major:
        stage_ref, = stage
        t = y_ref.shape[0]
        ntile = stage_ref.shape[0]
        for k in range(t):
            for lt in range(ntile):
                stage_ref[lt, pl.ds(k, y_ref.shape[1], stride=t), :] = y_ref[
                    k, :, lt * LANES:(lt + 1) * LANES].astype(F32)
        y = jnp.concatenate([stage_ref[lt] for lt in range(ntile)], axis=1)
    else:
        y = y_ref[...].astype(F32)
    gl = _gelu(y)
    z = jnp.dot(gl.astype(BF16), wglu_ref[...], preferred_element_type=F32)
    t = (gl * _sigmoid(z)).astype(BF16)
    y_ssm = jnp.dot(t, wssm_ref[...], preferred_element_type=F32)
    mixed = _sigmoid(gna_ref[...].astype(F32)) * y_na + _sigmoid(gssm_ref[...].astype(F32)) * y_ssm
    out_ref[...] = mixed.astype(out_ref.dtype)


def mix_branches(o_na, y, proj, w_na, w_glu, w_ssm, gate_col):
    rows, na_width = o_na.shape
    ssm_width = y.shape[-1]
    d = w_na.shape[1]
    tm = _pick(rows, MID_ROW_TILES)
    assert gate_col % d == 0
    gblk = gate_col // d
    resident = dict(pipeline_mode=pl.Buffered(1))
    chunk_major = y.ndim == 3
    if chunk_major:
        t = y.shape[0]
        y_spec = pl.BlockSpec((t, tm // t, ssm_width), lambda i: (0, i, 0))
        scratch = [pltpu.VMEM((ssm_width // LANES, tm, LANES), F32)]
    else:
        y_spec = pl.BlockSpec((tm, ssm_width), lambda i: (i, 0))
        scratch = []
    return pl.pallas_call(
        functools.partial(_mix_kernel, chunk_major=chunk_major),
        grid=(rows // tm,),
        scratch_shapes=scratch,
        in_specs=[
            pl.BlockSpec((tm, na_width), lambda i: (i, 0)),
            y_spec,
            pl.BlockSpec((tm, d), lambda i: (i, gblk)),
            pl.BlockSpec((tm, d), lambda i: (i, gblk + 1)),
            pl.BlockSpec(w_na.shape, lambda i: (0, 0), **resident),
            pl.BlockSpec(w_glu.shape, lambda i: (0, 0), **resident),
            pl.BlockSpec(w_ssm.shape, lambda i: (0, 0), **resident),
        ],
        out_specs=pl.BlockSpec((tm, d), lambda i: (i, 0)),
        out_shape=jax.ShapeDtypeStruct((rows, d), BF16),
        compiler_params=_params("parallel"),
        name="mix_branches",
    )(o_na, y, proj, proj, w_na, w_glu, w_ssm)


def _residual_matmul_kernel(h_ref, a_ref, w_ref, o_ref):
    o_ref[...] = h_ref[...] + jnp.dot(a_ref[...], w_ref[...], preferred_element_type=F32)


def residual_matmul(h, a, w):
    rows, d = h.shape
    tm = _pick(rows, MID_ROW_TILES)
    return pl.pallas_call(
        _residual_matmul_kernel,
        grid=(rows // tm,),
        in_specs=[
            pl.BlockSpec((tm, d), lambda i: (i, 0)),
            pl.BlockSpec((tm, a.shape[1]), lambda i: (i, 0)),
            pl.BlockSpec(w.shape, lambda i: (0, 0), pipeline_mode=pl.Buffered(1)),
        ],
        out_specs=pl.BlockSpec((tm, d), lambda i: (i, 0)),
        out_shape=jax.ShapeDtypeStruct((rows, d), F32),
        compiler_params=_params("parallel"),
        name="residual_out_proj",
    )(h, a, w)


HALO = BF16_SUBLANES


def _ffn_up_kernel(h_ref, prev_ref, next_ref, g_ref, wa_ref, wg_ref, cw_ref, cb_ref, o_ref, hn_ref, *, tm):
    @pl.when(pl.program_id(1) == 0)
    def _():
        g = g_ref[...]
        hn_ref[0:HALO, :] = _rms(prev_ref[...], g).astype(BF16)
        hn_ref[HALO:HALO + tm, :] = _rms(h_ref[...], g).astype(BF16)
        hn_ref[HALO + tm:, :] = _rms(next_ref[...], g).astype(BF16)

    ext = tm + 2 * HALO
    a = jnp.dot(hn_ref[...], wa_ref[...], preferred_element_type=F32)
    gate = jnp.dot(hn_ref[HALO:HALO + tm, :], wg_ref[...], preferred_element_type=F32)
    a_prev = pltpu.roll(a, 1, 0)[HALO:HALO + tm]
    a_next = pltpu.roll(a, ext - 1, 0)[HALO:HALO + tm]
    conv = a_prev * cw_ref[0:1, :] + a[HALO:HALO + tm] * cw_ref[1:2, :] + a_next * cw_ref[2:3, :] + cb_ref[...]
    o_ref[...] = (_gelu(conv) * gate).astype(o_ref.dtype)


def ffn_up(h, halo_prev, halo_next, g, w_up, conv_w, conv_b, tm):
    rows, d = h.shape
    dff = conv_b.shape[0]
    tn = _pick(dff, COL_TILES)
    nj = dff // tn
    kernel = functools.partial(_ffn_up_kernel, tm=tm)
    return pl.pallas_call(
        kernel,
        grid=(rows // tm, nj),
        in_specs=[
            pl.BlockSpec((tm, d), lambda i, j: (i, 0)),
            pl.BlockSpec((HALO, d), lambda i, j: (i, 0)),
            pl.BlockSpec((HALO, d), lambda i, j: (i, 0)),
            pl.BlockSpec((1, d), lambda i, j: (0, 0)),
            pl.BlockSpec((d, tn), lambda i, j: (0, j)),
            pl.BlockSpec((d, tn), lambda i, j: (0, nj + j)),
            pl.BlockSpec((conv_w.shape[0], tn), lambda i, j: (0, j)),
            pl.BlockSpec((1, tn), lambda i, j: (0, j)),
        ],
        out_specs=pl.BlockSpec((tm, tn), lambda i, j: (i, j)),
        out_shape=jax.ShapeDtypeStruct((rows, dff), BF16),
        scratch_shapes=[pltpu.VMEM((tm + 2 * HALO, d), BF16)],
        compiler_params=_params("parallel", "arbitrary"),
        name="ffn_up_conv_gate",
    )(h, halo_prev, halo_next, g.reshape(1, d), w_up, w_up, conv_w, conv_b.reshape(1, dff))


def _ffn_down_kernel(h_ref, a_ref, w_ref, g_ref, o_ref):
    k = pl.program_id(1)

    @pl.when(k == 0)
    def _():
        o_ref[...] = h_ref[...]

    o_ref[...] += jnp.dot(a_ref[...], w_ref[...], preferred_element_type=F32)

    @pl.when(k == pl.num_programs(1) - 1)
    def _():
        o_ref[...] = _rms(o_ref[...], g_ref[...])


def ffn_down(h, act, w_down, g):
    rows, d = h.shape
    dff = act.shape[1]
    tm = _pick(rows, ROW_TILES)
    tk = _pick(dff, COL_TILES)
    return pl.pallas_call(
        _ffn_down_kernel,
        grid=(rows // tm, dff // tk),
        in_specs=[
            pl.BlockSpec((tm, d), lambda i, k: (i, 0)),
            pl.BlockSpec((tm, tk), lambda i, k: (i, k)),
            pl.BlockSpec((tk, d), lambda i, k: (k, 0)),
            pl.BlockSpec((1, d), lambda i, k: (0, 0)),
        ],
        out_specs=pl.BlockSpec((tm, d), lambda i, k: (i, 0)),
        out_shape=jax.ShapeDtypeStruct((rows, d), F32),
        compiler_params=_params("parallel", "arbitrary"),
        name="ffn_down_final_norm",
    )(h, act, w_down, g.reshape(1, d))


def _conv_halos(h1_main, h1_meta, batch, seq, tm):
    d = h1_main.shape[1]
    per_seq = seq // tm
    hm = h1_main.reshape(batch, per_seq, tm, d)
    meta_tail = h1_meta.reshape(batch, 1, N_META, d)[:, :, N_META - HALO:]
    prev = jnp.concatenate([meta_tail, hm[:, :-1, tm - HALO:]], axis=1)
    nxt = jnp.concatenate([hm[:, 1:, :HALO], jnp.zeros((batch, 1, HALO, d), h1_main.dtype)], axis=1)
    return prev.reshape(batch * per_seq * HALO, d), nxt.reshape(batch * per_seq * HALO, d)


def kernel(x, meta_tokens, norm1_g, w_in, na_rpb, ssm_lam_re, ssm_lam_im, ssm_log_step, ssm_b_re, ssm_b_im,
           ssm_c_re, ssm_c_im, ssm_d, w_glu, w_proj_na, w_proj_ssm, w_out, norm2_g, w_up, conv_w, conv_b,
           w_down, final_g):
    batch, seq, d = x.shape
    depth = w_in.shape[0]
    na_width = w_proj_na.shape[1]
    ssm_width = w_proj_ssm.shape[1]
    assert depth == 1 and N_META >= HALO
    l = 0
    h_main = x.reshape(batch * seq, d)
    h_meta = jnp.broadcast_to(meta_tokens.astype(x.dtype)[None], (batch, N_META, d)).reshape(batch * N_META, d)

    w_in_b = w_in[l].astype(BF16)
    u_col = 3 * na_width
    proj_main, u_main = norm_matmul(h_main, norm1_g[l], w_in_b, chunk_major=(u_col, ssm_width))
    proj_meta = norm_matmul(h_meta, norm1_g[l], w_in_b)

    o_main, o_meta = neighbourhood_attention(proj_main, proj_meta, na_rpb[l], batch, seq, na_width)

    ops = _ssm_operators(ssm_lam_re[l], ssm_lam_im[l], ssm_log_step[l], ssm_b_re[l], ssm_b_im[l],
                         ssm_c_re[l], ssm_c_im[l], ssm_d[l])
    y_main, y_meta = s5_scan(u_main, proj_meta[:, u_col:u_col + ssm_width], ops, batch, seq)

    w_na_b, w_glu_b, w_ssm_b = w_proj_na[l].astype(BF16), w_glu[l].astype(BF16), w_proj_ssm[l].astype(BF16)
    w_out_b = w_out[l].astype(BF16)
    gate_col = u_col + ssm_width
    h1 = []
    for h, o, y, proj in ((h_main, o_main, y_main, proj_main), (h_meta, o_meta, y_meta, proj_meta)):
        mixed = mix_branches(o, y, proj, w_na_b, w_glu_b, w_ssm_b, gate_col)
        h1.append(residual_matmul(h, mixed, w_out_b))
    h1_main, h1_meta = h1

    tm = _pick(seq, ROW_TILES)
    halo_prev, halo_next = _conv_halos(h1_main, h1_meta, batch, seq, tm)
    act = ffn_up(h1_main, halo_prev, halo_next, norm2_g[l], w_up[l].astype(BF16), conv_w[l], conv_b[l], tm)
    out = ffn_down(h1_main, act, w_down[l].astype(BF16), final_g)
    return out.reshape(batch, seq, d)
```

```python
import functools
import math

import jax
import jax.numpy as jnp
from jax import lax
from jax.experimental import pallas as pl
from jax.experimental.pallas import tpu as pltpu

F32 = jnp.float32
BF16 = jnp.bfloat16

N_META = 16
GRID_W = 64
ROW_WIN = 8
COL_WIN = 16
RMS_EPS = 1e-6
SSM_CHUNK = 16
MASK_BIAS = -1e30
NA_QROWS = 4
NA_KROWS = 12
NA_UNROLL = 4

VMEM_LIMIT_BYTES = 56 * 1024 * 1024
F32_SUBLANES = 8
BF16_SUBLANES = 16
LANES = 128

SSM_SUB = 2
SSM_SCAN_UNROLL = True
SSM_META_ROWS = BF16_SUBLANES

ROW_TILES = (1024, 512, 256, 128, 64)
MID_ROW_TILES = (512, 256, 128, 64)
COL_TILES = (512, 256, 128)
WIDE_COL_TILES = (1024, 512, 256, 128)


def _pick(dim, prefs):
    for t in prefs:
        if dim % t == 0:
            return t
    return dim


def _params(*sem):
    return pltpu.CompilerParams(dimension_semantics=sem, vmem_limit_bytes=VMEM_LIMIT_BYTES)


def _rms(x, g):
    ms = jnp.mean(x * x, axis=-1, keepdims=True)
    return x * lax.rsqrt(ms + RMS_EPS) * g


def _gelu(x):
    c = math.sqrt(2.0 / math.pi)
    return 0.5 * x * (1.0 + jnp.tanh(c * (x + 0.044715 * (x * x * x))))


def _sigmoid(x):
    return 0.5 * jnp.tanh(0.5 * x) + 0.5


def _norm_matmul_kernel(x_ref, g_ref, w_ref, o_ref, *rest, chunk_cols):
    hn_ref = rest[-1] if chunk_cols is None else rest[1]

    @pl.when(pl.program_id(1) == 0)
    def _():
        hn_ref[...] = _rms(x_ref[...], g_ref[...]).astype(BF16)

    r = jnp.dot(hn_ref[...], w_ref[...], preferred_element_type=F32)
    o_ref[...] = r.astype(o_ref.dtype)

    if chunk_cols is not None:
        u3_ref, _, stage_ref = rest
        tile, off, width = chunk_cols
        t = SSM_CHUNK

        @pl.when(pl.program_id(1) == tile)
        def _():
            for lt in range(width // LANES):
                stage_ref[lt] = r[:, off + lt * LANES:off + (lt + 1) * LANES]
            for k in range(t):
                for lt in range(width // LANES):
                    u3_ref[k, :, lt * LANES:(lt + 1) * LANES] = stage_ref[
                        lt, pl.ds(k, stage_ref.shape[1] // t, stride=t), :].astype(u3_ref.dtype)


def norm_matmul(x, g, w, chunk_major=None):
    rows, d = x.shape
    n = w.shape[1]
    tm = _pick(rows, ROW_TILES)
    tn = _pick(n, WIDE_COL_TILES)
    out_specs = [pl.BlockSpec((tm, tn), lambda i, j: (i, j))]
    out_shape = [jax.ShapeDtypeStruct((rows, n), BF16)]
    scratch = [pltpu.VMEM((tm, d), BF16)]
    chunk_cols = None
    if chunk_major is not None:
        col, width = chunk_major
        t = SSM_CHUNK
        chunk_cols = (col // tn, col % tn, width)
        assert col % tn + width <= tn and tm % (t * F32_SUBLANES) == 0
        out_specs.append(pl.BlockSpec((t, tm // t, width), lambda i, j: (0, i, 0)))
        out_shape.append(jax.ShapeDtypeStruct((t, rows // t, width), BF16))
        scratch.append(pltpu.VMEM((width // LANES, tm, LANES), F32))
    out = pl.pallas_call(
        functools.partial(_norm_matmul_kernel, chunk_cols=chunk_cols),
        grid=(rows // tm, n // tn),
        in_specs=[
            pl.BlockSpec((tm, d), lambda i, j: (i, 0)),
            pl.BlockSpec((1, d), lambda i, j: (0, 0)),
            pl.BlockSpec((d, tn), lambda i, j: (0, j)),
        ],
        out_specs=out_specs,
        out_shape=out_shape,
        scratch_shapes=scratch,
        compiler_params=_params("parallel", "arbitrary"),
        name="norm_in_proj",
    )(x, g.reshape(1, d), w)
    return out[0] if chunk_major is None else out


def _na_kernel(q_ref, k_ref, v_ref, qm_ref, km_ref, vm_ref, tab_ref, o_ref, om_ref, bias_ref, sm_ref, acc_ref,
               *, rows, dh):
    lane = lax.broadcasted_iota(jnp.int32, (1, 2 * dh), 1)
    head_masks = (lane < dh, lane >= dh)
    scale = dh ** -0.5
    km = km_ref[...]
    vm = vm_ref[...]
    nt = (((1,), (1,)), ((), ()))

    masked = 2 * ROW_WIN - 1
    nblk = rows // NA_QROWS
    qblk = NA_QROWS * GRID_W
    kblk = NA_KROWS * GRID_W

    def rel_row(kind, j, i):
        if kind == 0:
            return i - j + (ROW_WIN - 1) if i < ROW_WIN else masked
        if kind == 1:
            return i - j + (ROW_WIN - 1 - ROW_WIN // 2) if j <= i < j + ROW_WIN else masked
        lo = NA_KROWS - ROW_WIN
        return i - j + (NA_QROWS - NA_KROWS) + (ROW_WIN - 1) if i >= lo else masked

    first_row = lane < GRID_W
    for kind in range(3):
        for hh in range(2):
            for j in range(NA_QROWS):
                r_lo = (hh * NA_QROWS + j) * GRID_W
                for i2 in range(NA_KROWS // 2):
                    bias_ref[kind, r_lo:r_lo + GRID_W, i2 * 2 * GRID_W:(i2 + 1) * 2 * GRID_W] = jnp.where(
                        first_row, tab_ref[hh, rel_row(kind, j, 2 * i2)], tab_ref[hh, rel_row(kind, j, 2 * i2 + 1)])

    def one_head(x, hh):
        return jnp.where(head_masks[hh], x, jnp.zeros_like(x))

    qm = (qm_ref[...] * scale).astype(BF16)
    om = None
    for hh in range(2):
        s = lax.dot_general(one_head(qm, hh), km, nt, preferred_element_type=F32)
        p = jnp.exp(s - jnp.max(s, axis=-1, keepdims=True))
        o_h = jnp.dot(p.astype(BF16), vm, preferred_element_type=F32) / jnp.sum(p, axis=-1, keepdims=True)
        om = o_h if om is None else jnp.where(head_masks[1], o_h, om)
    om_ref[...] = om.astype(om_ref.dtype)

    q_all = (q_ref[...] * scale).astype(BF16)
    for hh in range(2):
        sm_ref[hh] = lax.dot_general(one_head(q_all, hh), km, nt, preferred_element_type=F32)

    def body(blk, carry):
        r = blk * NA_QROWS
        k0 = jnp.clip(r - ROW_WIN // 2, 0, rows - NA_KROWS)
        kind = jnp.where(blk == 0, 0, jnp.where(blk == nblk - 1, 2, 1))
        qrows = pl.ds(pl.multiple_of(r * GRID_W, qblk), qblk)
        krows = pl.ds(pl.multiple_of(k0 * GRID_W, GRID_W), kblk)
        qs = (q_ref[qrows, :] * scale).astype(BF16)
        q2 = jnp.concatenate([one_head(qs, 0), one_head(qs, 1)], axis=0)
        s = lax.dot_general(q2, k_ref[krows, :], nt, preferred_element_type=F32) + bias_ref[kind]
        s_m = jnp.concatenate([sm_ref[0, qrows, :], sm_ref[1, qrows, :]], axis=0)
        m = jnp.maximum(jnp.max(s, axis=-1, keepdims=True), jnp.max(s_m, axis=-1, keepdims=True))
        p = jnp.exp(s - m)
        p_m = jnp.exp(s_m - m)
        inv = 1.0 / (jnp.sum(p, axis=-1, keepdims=True) + jnp.sum(p_m, axis=-1, keepdims=True))
        acc = jnp.dot(p.astype(BF16), v_ref[krows, :], preferred_element_type=F32) * inv
        acc_ref[qrows, :] = jnp.where(head_masks[1], acc[qblk:], acc[:qblk])
        p_m = p_m * inv
        sm_ref[0, qrows, :] = p_m[:qblk]
        sm_ref[1, qrows, :] = p_m[qblk:]
        return carry

    lax.fori_loop(0, nblk, body, 0, unroll=NA_UNROLL)

    o_meta = [jnp.dot(sm_ref[hh].astype(BF16), vm, preferred_element_type=F32) for hh in range(2)]
    o_ref[...] = (acc_ref[...] + jnp.where(head_masks[1], o_meta[1], o_meta[0])).astype(o_ref.dtype)


def _na_bias_table(rpb):
    c = jnp.arange(GRID_W)[:, None]
    kc = jnp.arange(GRID_W)[None, :]
    col_start = jnp.clip(c - COL_WIN // 2, 0, GRID_W - COL_WIN)
    valid = (kc >= col_start) & (kc < col_start + COL_WIN)
    dc = kc - c + (COL_WIN - 1)
    pick = (dc.reshape(-1)[None] == jnp.arange(2 * COL_WIN - 1)[:, None]).astype(F32)
    tab = jnp.dot(rpb.astype(F32).reshape(-1, 2 * COL_WIN - 1), pick, precision=lax.Precision.HIGHEST)
    tab = tab.reshape(rpb.shape[0], 2 * ROW_WIN - 1, GRID_W, GRID_W)
    tab = jnp.where(valid, tab, MASK_BIAS)
    tab = jnp.concatenate([tab, jnp.full_like(tab[:, :1], MASK_BIAS)], axis=1)
    return jnp.concatenate([tab, tab], axis=-1)


def neighbourhood_attention(proj_main, proj_meta, rpb, batch, seq, na_width):
    heads = rpb.shape[0]
    dh = na_width // heads
    rows = seq // GRID_W
    assert rows >= NA_KROWS and rows % NA_QROWS == 0 and heads % 2 == 0 and 2 * dh == LANES
    assert NA_KROWS % 2 == 0 and NA_KROWS >= ROW_WIN + NA_QROWS - 1 and NA_QROWS <= ROW_WIN // 2
    pairs = heads // 2
    bias = _na_bias_table(rpb)
    blk = (seq, 2 * dh)
    mblk = (N_META, 2 * dh)
    kernel = functools.partial(_na_kernel, rows=rows, dh=dh)
    return pl.pallas_call(
        kernel,
        grid=(batch, pairs),
        in_specs=[
            pl.BlockSpec(blk, lambda b, h: (b, h)),
            pl.BlockSpec(blk, lambda b, h: (b, pairs + h)),
            pl.BlockSpec(blk, lambda b, h: (b, 2 * pairs + h)),
            pl.BlockSpec(mblk, lambda b, h: (b, h)),
            pl.BlockSpec(mblk, lambda b, h: (b, pairs + h)),
            pl.BlockSpec(mblk, lambda b, h: (b, 2 * pairs + h)),
            pl.BlockSpec((2, 2 * ROW_WIN, GRID_W, 2 * GRID_W), lambda b, h: (h, 0, 0, 0)),
        ],
        out_specs=[
            pl.BlockSpec(blk, lambda b, h: (b, h)),
            pl.BlockSpec(mblk, lambda b, h: (b, h)),
        ],
        out_shape=[
            jax.ShapeDtypeStruct((batch * seq, na_width), BF16),
            jax.ShapeDtypeStruct((batch * N_META, na_width), BF16),
        ],
        scratch_shapes=[pltpu.VMEM((3, 2 * NA_QROWS * GRID_W, NA_KROWS * GRID_W), F32),
                        pltpu.VMEM((2, seq, N_META), F32),
                        pltpu.VMEM((seq, 2 * dh), F32)],
        compiler_params=_params("parallel", "parallel"),
        name="neighbourhood_attention",
    )(proj_main, proj_main, proj_main, proj_meta, proj_meta, proj_meta, bias)


def _chunk_ops_kernel(lag_ref, shift_ref, pw_ref, bt_ref, me_ref):
    gt, c, _ = lag_ref.shape
    t, _, tc = shift_ref.shape
    w = pw_ref.shape[3]
    lag = lag_ref[...].reshape(gt * c, 2 * tc).astype(BF16)
    b_r, b_i = bt_ref[:, 0], bt_ref[:, 1]
    for s in range(t):
        rows = slice(s * c, (s + 1) * c)
        blk = jnp.dot(lag, shift_ref[s], preferred_element_type=F32)
        me_ref[:, rows, 0:tc] = blk.reshape(gt, c, tc).astype(me_ref.dtype)
        e_r, e_i = pw_ref[:, 0, s:s + 1, :], pw_ref[:, 1, s:s + 1, :]
        me_ref[:, rows, tc:tc + w] = (e_r * b_r - e_i * b_i).astype(me_ref.dtype)
        me_ref[:, rows, tc + w:tc + 2 * w] = (e_r * b_i + e_i * b_r).astype(me_ref.dtype)


def _chunk_operators(lag, pw, bt):
    g, c, two_tc = lag.shape
    tc = two_tc // 2
    t = tc // c
    w = pw.shape[-1]
    gt = _pick(g, (16, 8, 4, 2, 1))
    p = jnp.arange(2 * tc)[None, :, None]
    q = jnp.arange(tc)[None, None, :]
    s = jnp.arange(t)[:, None, None]
    shift = jnp.where(p < tc, p == q - s * c, p - tc == q + (t - 1 - s) * c).astype(BF16)
    return pl.pallas_call(
        _chunk_ops_kernel,
        grid=(g // gt,),
        in_specs=[
            pl.BlockSpec((gt, c, 2 * tc), lambda i: (i, 0, 0)),
            pl.BlockSpec((t, 2 * tc, tc), lambda i: (0, 0, 0)),
            pl.BlockSpec((gt, 2, t, w), lambda i: (i, 0, 0, 0)),
            pl.BlockSpec((gt, 2, c, w), lambda i: (i, 0, 0, 0)),
        ],
        out_specs=pl.BlockSpec((gt, t * c, tc + 2 * w), lambda i: (i, 0, 0)),
        out_shape=jax.ShapeDtypeStruct((g, t * c, tc + 2 * w), BF16),
        compiler_params=_params("parallel"),
        name="s5_chunk_operators",
    )(lag, shift, pw, bt)


def _ssm_operators(lam_re, lam_im, log_step, b_re, b_im, c_re, c_im, d_skip):
    t = SSM_CHUNK
    hp = lax.Precision.HIGHEST
    lr, li = lam_re.astype(F32), lam_im.astype(F32)
    step = jnp.exp(log_step.astype(F32))[..., None]
    dt_r, dt_i = lr * step, li * step
    g, n = lr.shape[1], lr.shape[2]
    c = b_re.shape[-1]

    def lam_bar_pow(k):
        mag = jnp.exp(dt_r[:, :, None] * k)
        return mag * jnp.cos(dt_i[:, :, None] * k), mag * jnp.sin(dt_i[:, :, None] * k)

    p_r, p_i = lam_bar_pow(jnp.arange(t + 1, dtype=F32)[:, None])
    x_r, x_i = p_r[:, :, 1] - 1.0, p_i[:, :, 1]
    den = lr * lr + li * li
    q_r, q_i = (x_r * lr + x_i * li) / den, (x_i * lr - x_r * li) / den
    bt_r, bt_i = jnp.swapaxes(b_re.astype(F32), 2, 3), jnp.swapaxes(b_im.astype(F32), 2, 3)
    bb_r = q_r[:, :, None] * bt_r - q_i[:, :, None] * bt_i
    bb_i = q_r[:, :, None] * bt_i + q_i[:, :, None] * bt_r
    ct_r, ct_i = jnp.swapaxes(c_re.astype(F32), 2, 3), jnp.swapaxes(c_im.astype(F32), 2, 3)

    cx_r, cx_i = jnp.tile(ct_r, (1, 1, 1, t)), jnp.tile(ct_i, (1, 1, 1, t))

    def c_times_pow(d, pw_r, pw_i):
        e_r = jnp.repeat(jnp.swapaxes(pw_r, 1, 2), c, axis=2)
        e_i = jnp.repeat(jnp.swapaxes(pw_i, 1, 2), c, axis=2)
        return cx_r[d] * e_r - cx_i[d] * e_i, cx_r[d] * e_i + cx_i[d] * e_r

    def lag_rows(d, pw_r, pw_i):
        z_r, z_i = c_times_pow(d, pw_r, pw_i)
        return (jnp.einsum('gkn,gnq->gkq', bb_r[d], z_r, precision=hp)
                - jnp.einsum('gkn,gnq->gkq', bb_i[d], z_i, precision=hp))

    k_f = lag_rows(0, p_r[0][:, :t], p_i[0][:, :t])
    k_r = lag_rows(1, p_r[1][:, :t][:, ::-1], p_i[1][:, :t][:, ::-1])
    lag0 = k_r[:, :, (t - 1) * c:] + jnp.eye(c, dtype=F32)[None] * d_skip.astype(F32).reshape(g, 1, c)
    k_f = k_f.at[:, :, :c].add(lag0)
    k_r = k_r.at[:, :, (t - 1) * c:].set(0.0)
    lag = jnp.concatenate([k_f, k_r], axis=-1)

    pw = jnp.stack([jnp.concatenate([p_r[0][:, ::-1][:, 1:], p_r[1][:, :t]], axis=-1),
                    jnp.concatenate([p_i[0][:, ::-1][:, 1:], p_i[1][:, :t]], axis=-1)], axis=1)
    bt = jnp.stack([jnp.concatenate([bb_r[0], bb_r[1]], axis=-1),
                    jnp.concatenate([bb_i[0], bb_i[1]], axis=-1)], axis=1)
    me = _chunk_operators(lag, pw, bt)

    gf_r, gf_i = c_times_pow(0, p_r[0][:, 1:], p_i[0][:, 1:])
    gr_r, gr_i = c_times_pow(1, p_r[1][:, ::-1][:, :t], p_i[1][:, ::-1][:, :t])
    f = jnp.concatenate([gf_r, gr_r, -gf_i, -gr_i], axis=1)

    sub = F32_SUBLANES
    row = jnp.arange(sub)

    k_rows, keep = [], []
    for k in (1, 2, 4):
        k_rows.append(jnp.full((sub,), k))
        keep.append(row >= k)
    k_rows.append(row + 1)
    keep.append(row >= 0)
    for k in (1, 2, 4):
        k_rows.append(jnp.full((sub,), k))
        keep.append(row < sub - k)
    k_rows.append(sub - row)
    keep.append(row >= 0)
    k_all = (t * jnp.concatenate(k_rows)).astype(F32)
    keep_all = jnp.concatenate(keep)[None, :, None]
    a_r, a_i = lam_bar_pow(k_all[:, None])
    a_r = (jnp.concatenate([a_r[0], a_r[1]], axis=-1) * keep_all).reshape(g, len(k_rows), 1, sub, 2 * n)
    a_i = (jnp.concatenate([a_i[0], a_i[1]], axis=-1) * keep_all).reshape(g, len(k_rows), 1, sub, 2 * n)
    a = jnp.concatenate([a_r, a_i], axis=2).reshape(g, 2 * len(k_rows), sub, 2 * n)

    lanes_per_tok = 128
    toks = lanes_per_tok // c
    grp = lanes_per_tok // c
    src = jnp.transpose(jnp.arange(toks * grp * c).reshape(toks, grp, c), (1, 0, 2)).reshape(-1)
    perm = (jnp.arange(toks * grp * c)[:, None] == src[None, :]).astype(BF16)
    return perm, me.astype(BF16), f.astype(BF16), a.astype(F32)


def _ssm_kernel(*refs, batch, nchunks, nstate, gsub):
    t = SSM_CHUNK
    (u_ref, um_ref, p_ref, me_ref, f_ref, a_ref, y_ref, ym_ref,
     yi_ref, ee_ref, pf_ref, pr_ref, ycat_ref, ymcat_ref) = refs
    gtile = me_ref.shape[0]
    tc = f_ref.shape[2]
    toks = p_ref.shape[0] // LANES
    parts = t // toks
    sub = F32_SUBLANES
    w = 2 * nstate
    tiles = nchunks // sub
    perm = p_ref[...]
    row = lax.broadcasted_iota(jnp.int32, (sub, w), 0)
    lane = lax.broadcasted_iota(jnp.int32, (1, w), 1)
    rev_lane = lane >= nstate

    def group_major(token_refs):
        out = []
        for part in range(parts):
            x = jnp.concatenate([token_refs[part * toks + i] for i in range(toks)], axis=1)
            out.append(jnp.dot(x, perm, preferred_element_type=F32).astype(BF16))
        return out

    def group_chunk(regrouped, g):
        return jnp.concatenate([x[:, g * LANES:(g + 1) * LANES] for x in regrouped], axis=1)

    v_main = group_major(u_ref)
    v_meta = group_major(um_ref)

    def cmul_add(xr, xi, ar, ai, yr, yi):
        return xr + ar * yr - ai * yi, xi + ar * yi + ai * yr

    def bcast_row(x, r):
        return jnp.broadcast_to(x[r:r + 1, :], (sub, w))

    for sb in range(gtile // gsub):
        gs = [sb * gsub + gi for gi in range(gsub)]
        meta = []
        for gi, g in enumerate(gs):
            me = jnp.dot(group_chunk(v_main, g), me_ref[g], preferred_element_type=F32)
            yi_ref[gi] = me[:, :tc]
            ee_ref[gi] = me[:, tc:]
            meta.append(jnp.dot(group_chunk(v_meta, g), me_ref[g], preferred_element_type=F32))

        def body(j, carry, gs=gs):
            new = []
            for gi, g in enumerate(gs):
                for b in range(batch):
                    lr, li, fr, fi = carry[gi * batch + b]
                    fs = pl.ds(pl.multiple_of(b * nchunks + j * sub, sub), sub)
                    xr, xi = ee_ref[gi, fs, 0:w], ee_ref[gi, fs, w:2 * w]
                    for step, k in enumerate((1, 2, 4)):
                        xr, xi = cmul_add(xr, xi, a_ref[g, 2 * step], a_ref[g, 2 * step + 1],
                                          pltpu.roll(xr, k, 0), pltpu.roll(xi, k, 0))
                    sr, si = cmul_add(xr, xi, a_ref[g, 6], a_ref[g, 7], lr, li)
                    pf_ref[gi, fs, 0:w] = jnp.where(row == 0, lr, pltpu.roll(sr, 1, 0))
                    pf_ref[gi, fs, w:2 * w] = jnp.where(row == 0, li, pltpu.roll(si, 1, 0))
                    rs = pl.ds(pl.multiple_of(b * nchunks + (tiles - 1 - j) * sub, sub), sub)
                    xr, xi = ee_ref[gi, rs, 0:w], ee_ref[gi, rs, w:2 * w]
                    for step, k in enumerate((1, 2, 4)):
                        xr, xi = cmul_add(xr, xi, a_ref[g, 8 + 2 * step], a_ref[g, 9 + 2 * step],
                                          pltpu.roll(xr, sub - k, 0), pltpu.roll(xi, sub - k, 0))
                    rr, ri = cmul_add(xr, xi, a_ref[g, 14], a_ref[g, 15], fr, fi)
                    pr_ref[gi, rs, 0:w] = jnp.where(row == sub - 1, fr, pltpu.roll(rr, sub - 1, 0))
                    pr_ref[gi, rs, w:2 * w] = jnp.where(row == sub - 1, fi, pltpu.roll(ri, sub - 1, 0))
                    new.append((bcast_row(sr, sub - 1), bcast_row(si, sub - 1), bcast_row(rr, 0), bcast_row(ri, 0)))
            return tuple(new)

        zero = jnp.zeros((sub, w), F32)
        init = tuple((bcast_row(meta[gi][:, tc:tc + w], b), bcast_row(meta[gi][:, tc + w:tc + 2 * w], b), zero, zero)
                     for gi in range(gsub) for b in range(batch))
        final = lax.fori_loop(0, tiles, body, init, unroll=SSM_SCAN_UNROLL)

        for gi, g in enumerate(gs):
            prev = jnp.concatenate([jnp.where(rev_lane, pr_ref[gi, :, k * w:(k + 1) * w], pf_ref[gi, :, k * w:(k + 1) * w])
                                    for k in range(2)], axis=1).astype(BF16)
            y = (yi_ref[gi] + jnp.dot(prev, f_ref[g], preferred_element_type=F32)).astype(BF16)
            pm = []
            for k in (2, 3):
                x = zero
                for b in range(batch):
                    x = jnp.where((row == b) & rev_lane, final[gi * batch + b][k], x)
                pm.append(jnp.concatenate([x, jnp.zeros((SSM_META_ROWS - sub, w), F32)], axis=0))
            prev_m = jnp.concatenate(pm, axis=1).astype(BF16)
            y_m = (meta[gi][:, :tc] + jnp.dot(prev_m, f_ref[g], preferred_element_type=F32)).astype(BF16)
            for part in range(parts):
                ycat_ref[part, :, g * LANES:(g + 1) * LANES] = y[:, part * LANES:(part + 1) * LANES]
                ymcat_ref[part, :, g * LANES:(g + 1) * LANES] = y_m[:, part * LANES:(part + 1) * LANES]

    for part in range(parts):
        o = jnp.dot(ycat_ref[part], perm, preferred_element_type=F32).astype(y_ref.dtype)
        o_m = jnp.dot(ymcat_ref[part], perm, preferred_element_type=F32).astype(ym_ref.dtype)
        for i in range(toks):
            y_ref[part * toks + i] = o[:, i * LANES:(i + 1) * LANES]
            ym_ref[part * toks + i] = o_m[:, i * LANES:(i + 1) * LANES]


def s5_scan(u_main, u_meta_rows, ops, batch, seq):
    perm, me, f, a = ops
    g = me.shape[0]
    t = SSM_CHUNK
    tc = f.shape[2]
    c = tc // t
    width = g * c
    nstate = f.shape[1] // 4
    gtile = LANES // c
    nchunks = seq // t
    rows = batch * nchunks
    assert N_META == t and seq % (t * F32_SUBLANES) == 0 and batch <= F32_SUBLANES
    assert g % gtile == 0 and gtile % SSM_SUB == 0 and u_main.shape == (t, rows, width)

    u_meta = jnp.transpose(u_meta_rows.reshape(batch, t, width), (1, 0, 2))
    u_meta = jnp.pad(u_meta, ((0, 0), (0, SSM_META_ROWS - batch), (0, 0)))
    kernel = functools.partial(_ssm_kernel, batch=batch, nchunks=nchunks, nstate=nstate, gsub=SSM_SUB)
    y, y_m = pl.pallas_call(
        kernel,
        grid=(g // gtile,),
        in_specs=[
            pl.BlockSpec((t, rows, LANES), lambda i: (0, 0, i)),
            pl.BlockSpec((t, SSM_META_ROWS, LANES), lambda i: (0, 0, i)),
            pl.BlockSpec(perm.shape, lambda i: (0, 0)),
            pl.BlockSpec((gtile,) + me.shape[1:], lambda i: (i, 0, 0)),
            pl.BlockSpec((gtile,) + f.shape[1:], lambda i: (i, 0, 0)),
            pl.BlockSpec((gtile,) + a.shape[1:], lambda i: (i, 0, 0, 0)),
        ],
        out_specs=[
            pl.BlockSpec((t, rows, LANES), lambda i: (0, 0, i)),
            pl.BlockSpec((t, SSM_META_ROWS, LANES), lambda i: (0, 0, i)),
        ],
        out_shape=[
            jax.ShapeDtypeStruct((t, rows, width), BF16),
            jax.ShapeDtypeStruct((t, SSM_META_ROWS, width), BF16),
        ],
        scratch_shapes=[
            pltpu.VMEM((SSM_SUB, rows, tc), F32),
            pltpu.VMEM((SSM_SUB, rows, 4 * nstate), F32),
            pltpu.VMEM((SSM_SUB, rows, 4 * nstate), F32),
            pltpu.VMEM((SSM_SUB, rows, 4 * nstate), F32),
            pltpu.VMEM((t // (perm.shape[0] // LANES), rows, gtile * LANES), BF16),
            pltpu.VMEM((t // (perm.shape[0] // LANES), SSM_META_ROWS, gtile * LANES), BF16),
        ],
        compiler_params=_params("parallel"),
        name="s5_scan",
    )(u_main, u_meta, perm, me, f, a)

    y_meta = jnp.transpose(y_m[:, :batch], (1, 0, 2)).reshape(batch * N_META, width)
    return y, y_meta


def _mix_kernel(o_ref, y_ref, gna_ref, gssm_ref, wna_ref, wglu_ref, wssm_ref, out_ref, *stage, chunk_major):
    y_na = jnp.dot(o_ref[...], wna_ref[...], preferred_element_type=F32)
    if chunk_major:
        stage_ref, = stage
        t = y_ref.shape[0]
        ntile = stage_ref.shape[0]
        for k in range(t):
            for lt in range(ntile):
                stage_ref[lt, pl.ds(k, y_ref.shape[1], stride=t), :] = y_ref[
                    k, :, lt * LANES:(lt + 1) * LANES].astype(F32)
        y = jnp.concatenate([stage_ref[lt] for lt in range(ntile)], axis=1)
    else:
        y = y_ref[...].astype(F32)
    gl = _gelu(y)
    z = jnp.dot(gl.astype(BF16), wglu_ref[...], preferred_element_type=F32)
    t = (gl * _sigmoid(z)).astype(BF16)
    y_ssm = jnp.dot(t, wssm_ref[...], preferred_element_type=F32)
    mixed = _sigmoid(gna_ref[...].astype(F32)) * y_na + _sigmoid(gssm_ref[...].astype(F32)) * y_ssm
    out_ref[...] = mixed.astype(out_ref.dtype)


def mix_branches(o_na, y, proj, w_na, w_glu, w_ssm, gate_col):
    rows, na_width = o_na.shape
    ssm_width = y.shape[-1]
    d = w_na.shape[1]
    tm = _pick(rows, MID_ROW_TILES)
    assert gate_col % d == 0
    gblk = gate_col // d
    resident = dict(pipeline_mode=pl.Buffered(1))
    chunk_major = y.ndim == 3
    if chunk_major:
        t = y.shape[0]
        y_spec = pl.BlockSpec((t, tm // t, ssm_width), lambda i: (0, i, 0))
        scratch = [pltpu.VMEM((ssm_width // LANES, tm, LANES), F32)]
    else:
        y_spec = pl.BlockSpec((tm, ssm_width), lambda i: (i, 0))
        scratch = []
    return pl.pallas_call(
        functools.partial(_mix_kernel, chunk_major=chunk_major),
        grid=(rows // tm,),
        scratch_shapes=scratch,
        in_specs=[
            pl.BlockSpec((tm, na_width), lambda i: (i, 0)),
            y_spec,
            pl.BlockSpec((tm, d), lambda i: (i, gblk)),
            pl.BlockSpec((tm, d), lambda i: (i, gblk + 1)),
            pl.BlockSpec(w_na.shape, lambda i: (0, 0), **resident),
            pl.BlockSpec(w_glu.shape, lambda i: (0, 0), **resident),
            pl.BlockSpec(w_ssm.shape, lambda i: (0, 0), **resident),
        ],
        out_specs=pl.BlockSpec((tm, d), lambda i: (i, 0)),
        out_shape=jax.ShapeDtypeStruct((rows, d), BF16),
        compiler_params=_params("parallel"),
        name="mix_branches",
    )(o_na, y, proj, proj, w_na, w_glu, w_ssm)


def _residual_matmul_kernel(h_ref, a_ref, w_ref, o_ref):
    o_ref[...] = h_ref[...] + jnp.dot(a_ref[...], w_ref[...], preferred_element_type=F32)


def residual_matmul(h, a, w):
    rows, d = h.shape
    tm = _pick(rows, MID_ROW_TILES)
    return pl.pallas_call(
        _residual_matmul_kernel,
        grid=(rows // tm,),
        in_specs=[
            pl.BlockSpec((tm, d), lambda i: (i, 0)),
            pl.BlockSpec((tm, a.shape[1]), lambda i: (i, 0)),
            pl.BlockSpec(w.shape, lambda i: (0, 0), pipeline_mode=pl.Buffered(1)),
        ],
        out_specs=pl.BlockSpec((tm, d), lambda i: (i, 0)),
        out_shape=jax.ShapeDtypeStruct((rows, d), F32),
        compiler_params=_params("parallel"),
        name="residual_out_proj",
    )(h, a, w)


HALO = BF16_SUBLANES


def _ffn_up_kernel(h_ref, prev_ref, next_ref, g_ref, wa_ref, wg_ref, cw_ref, cb_ref, o_ref, hn_ref, *, tm):
    @pl.when(pl.program_id(1) == 0)
    def _():
        g = g_ref[...]
        hn_ref[0:HALO, :] = _rms(prev_ref[...], g).astype(BF16)
        hn_ref[HALO:HALO + tm, :] = _rms(h_ref[...], g).astype(BF16)
        hn_ref[HALO + tm:, :] = _rms(next_ref[...], g).astype(BF16)

    ext = tm + 2 * HALO
    a = jnp.dot(hn_ref[...], wa_ref[...], preferred_element_type=F32)
    gate = jnp.dot(hn_ref[HALO:HALO + tm, :], wg_ref[...], preferred_element_type=F32)
    a_prev = pltpu.roll(a, 1, 0)[HALO:HALO + tm]
    a_next = pltpu.roll(a, ext - 1, 0)[HALO:HALO + tm]
    conv = a_prev * cw_ref[0:1, :] + a[HALO:HALO + tm] * cw_ref[1:2, :] + a_next * cw_ref[2:3, :] + cb_ref[...]
    o_ref[...] = (_gelu(conv) * gate).astype(o_ref.dtype)


def ffn_up(h, halo_prev, halo_next, g, w_up, conv_w, conv_b, tm):
    rows, d = h.shape
    dff = conv_b.shape[0]
    tn = _pick(dff, COL_TILES)
    nj = dff // tn
    kernel = functools.partial(_ffn_up_kernel, tm=tm)
    return pl.pallas_call(
        kernel,
        grid=(rows // tm, nj),
        in_specs=[
            pl.BlockSpec((tm, d), lambda i, j: (i, 0)),
            pl.BlockSpec((HALO, d), lambda i, j: (i, 0)),
            pl.BlockSpec((HALO, d), lambda i, j: (i, 0)),
            pl.BlockSpec((1, d), lambda i, j: (0, 0)),
            pl.BlockSpec((d, tn), lambda i, j: (0, j)),
            pl.BlockSpec((d, tn), lambda i, j: (0, nj + j)),
            pl.BlockSpec((conv_w.shape[0], tn), lambda i, j: (0, j)),
            pl.BlockSpec((1, tn), lambda i, j: (0, j)),
        ],
        out_specs=pl.BlockSpec((tm, tn), lambda i, j: (i, j)),
        out_shape=jax.ShapeDtypeStruct((rows, dff), BF16),
        scratch_shapes=[pltpu.VMEM((tm + 2 * HALO, d), BF16)],
        compiler_params=_params("parallel", "arbitrary"),
        name="ffn_up_conv_gate",
    )(h, halo_prev, halo_next, g.reshape(1, d), w_up, w_up, conv_w, conv_b.reshape(1, dff))


def _ffn_down_kernel(h_ref, a_ref, w_ref, g_ref, o_ref):
    k = pl.program_id(1)

    @pl.when(k == 0)
    def _():
        o_ref[...] = h_ref[...]

    o_ref[...] += jnp.dot(a_ref[...], w_ref[...], preferred_element_type=F32)

    @pl.when(k == pl.num_programs(1) - 1)
    def _():
        o_ref[...] = _rms(o_ref[...], g_ref[...])


def ffn_down(h, act, w_down, g):
    rows, d = h.shape
    dff = act.shape[1]
    tm = _pick(rows, ROW_TILES)
    tk = _pick(dff, COL_TILES)
    return pl.pallas_call(
        _ffn_down_kernel,
        grid=(rows // tm, dff // tk),
        in_specs=[
            pl.BlockSpec((tm, d), lambda i, k: (i, 0)),
            pl.BlockSpec((tm, tk), lambda i, k: (i, k)),
            pl.BlockSpec((tk, d), lambda i, k: (k, 0)),
            pl.BlockSpec((1, d), lambda i, k: (0, 0)),
        ],
        out_specs=pl.BlockSpec((tm, d), lambda i, k: (i, 0)),
        out_shape=jax.ShapeDtypeStruct((rows, d), F32),
        compiler_params=_params("parallel", "arbitrary"),
        name="ffn_down_final_norm",
    )(h, act, w_down, g.reshape(1, d))


def _conv_halos(h1_main, h1_meta, batch, seq, tm):
    d = h1_main.shape[1]
    per_seq = seq // tm
    hm = h1_main.reshape(batch, per_seq, tm, d)
    meta_tail = h1_meta.reshape(batch, 1, N_META, d)[:, :, N_META - HALO:]
    prev = jnp.concatenate([meta_tail, hm[:, :-1, tm - HALO:]], axis=1)
    nxt = jnp.concatenate([hm[:, 1:, :HALO], jnp.zeros((batch, 1, HALO, d), h1_main.dtype)], axis=1)
    return prev.reshape(batch * per_seq * HALO, d), nxt.reshape(batch * per_seq * HALO, d)


def kernel(x, meta_tokens, norm1_g, w_in, na_rpb, ssm_lam_re, ssm_lam_im, ssm_log_step, ssm_b_re, ssm_b_im,
           ssm_c_re, ssm_c_im, ssm_d, w_glu, w_proj_na, w_proj_ssm, w_out, norm2_g, w_up, conv_w, conv_b,
           w_down, final_g):
    batch, seq, d = x.shape
    depth = w_in.shape[0]
    na_width = w_proj_na.shape[1]
    ssm_width = w_proj_ssm.shape[1]
    assert depth == 1 and N_META >= HALO
    l = 0
    h_main = x.reshape(batch * seq, d)
    h_meta = jnp.broadcast_to(meta_tokens.astype(x.dtype)[None], (batch, N_META, d)).reshape(batch * N_META, d)

    w_in_b = w_in[l].astype(BF16)
    u_col = 3 * na_width
    proj_main, u_main = norm_matmul(h_main, norm1_g[l], w_in_b, chunk_major=(u_col, ssm_width))
    proj_meta = norm_matmul(h_meta, norm1_g[l], w_in_b)

    o_main, o_meta = neighbourhood_attention(proj_main, proj_meta, na_rpb[l], batch, seq, na_width)

    ops = _ssm_operators(ssm_lam_re[l], ssm_lam_im[l], ssm_log_step[l], ssm_b_re[l], ssm_b_im[l],
                         ssm_c_re[l], ssm_c_im[l], ssm_d[l])
    y_main, y_meta = s5_scan(u_main, proj_meta[:, u_col:u_col + ssm_width], ops, batch, seq)

    w_na_b, w_glu_b, w_ssm_b = w_proj_na[l].astype(BF16), w_glu[l].astype(BF16), w_proj_ssm[l].astype(BF16)
    w_out_b = w_out[l].astype(BF16)
    gate_col = u_col + ssm_width
    h1 = []
    for h, o, y, proj in ((h_main, o_main, y_main, proj_main), (h_meta, o_meta, y_meta, proj_meta)):
        mixed = mix_branches(o, y, proj, w_na_b, w_glu_b, w_ssm_b, gate_col)
        h1.append(residual_matmul(h, mixed, w_out_b))
    h1_main, h1_meta = h1

    tm = _pick(seq, ROW_TILES)
    halo_prev, halo_next = _conv_halos(h1_main, h1_meta, batch, seq, tm)
    act = ffn_up(h1_main, halo_prev, halo_next, norm2_g[l], w_up[l].astype(BF16), conv_w[l], conv_b[l], tm)
    out = ffn_down(h1_main, act, w_down[l].astype(BF16), final_g)
    return out.reshape(batch, seq, d)
```

```python
import functools
import math

import jax
import jax.numpy as jnp
from jax import lax
from jax.experimental import pallas as pl
from jax.experimental.pallas import tpu as pltpu

F32 = jnp.float32
BF16 = jnp.bfloat16

N_META = 16
GRID_W = 64
ROW_WIN = 8
COL_WIN = 16
RMS_EPS = 1e-6
SSM_CHUNK = 16
MASK_BIAS = -1e30
NA_QROWS = 4
NA_KROWS = 12
NA_UNROLL = 4

VMEM_LIMIT_BYTES = 56 * 1024 * 1024
F32_SUBLANES = 8
BF16_SUBLANES = 16
LANES = 128

SSM_SUB = 2
SSM_SCAN_UNROLL = True
SSM_META_ROWS = BF16_SUBLANES

ROW_TILES = (1024, 512, 256, 128, 64)
MID_ROW_TILES = (512, 256, 128, 64)
COL_TILES = (512, 256, 128)
WIDE_COL_TILES = (1024, 512, 256, 128)


def _pick(dim, prefs):
    for t in prefs:
        if dim % t == 0:
            return t
    return dim


def _params(*sem):
    return pltpu.CompilerParams(dimension_semantics=sem, vmem_limit_bytes=VMEM_LIMIT_BYTES)


def _rms(x, g):
    ms = jnp.mean(x * x, axis=-1, keepdims=True)
    return x * lax.rsqrt(ms + RMS_EPS) * g


def _gelu(x):
    c = math.sqrt(2.0 / math.pi)
    return 0.5 * x * (1.0 + jnp.tanh(c * (x + 0.044715 * (x * x * x))))


def _sigmoid(x):
    return 0.5 * jnp.tanh(0.5 * x) + 0.5


def _norm_matmul_kernel(x_ref, g_ref, w_ref, o_ref, *rest, chunk_cols):
    hn_ref = rest[-1] if chunk_cols is None else rest[1]

    @pl.when(pl.program_id(1) == 0)
    def _():
        hn_ref[...] = _rms(x_ref[...], g_ref[...]).astype(BF16)

    r = jnp.dot(hn_ref[...], w_ref[...], preferred_element_type=F32)
    o_ref[...] = r.astype(o_ref.dtype)

    if chunk_cols is not None:
        u3_ref, _, stage_ref = rest
        tile, off, width = chunk_cols
        t = SSM_CHUNK

        @pl.when(pl.program_id(1) == tile)
        def _():
            for lt in range(width // LANES):
                stage_ref[lt] = r[:, off + lt * LANES:off + (lt + 1) * LANES]
            for k in range(t):
                for lt in range(width // LANES):
                    u3_ref[k, :, lt * LANES:(lt + 1) * LANES] = stage_ref[
                        lt, pl.ds(k, stage_ref.shape[1] // t, stride=t), :].astype(u3_ref.dtype)


def norm_matmul(x, g, w, chunk_major=None):
    rows, d = x.shape
    n = w.shape[1]
    tm = _pick(rows, ROW_TILES)
    tn = _pick(n, WIDE_COL_TILES)
    out_specs = [pl.BlockSpec((tm, tn), lambda i, j: (i, j))]
    out_shape = [jax.ShapeDtypeStruct((rows, n), BF16)]
    scratch = [pltpu.VMEM((tm, d), BF16)]
    chunk_cols = None
    if chunk_major is not None:
        col, width = chunk_major
        t = SSM_CHUNK
        chunk_cols = (col // tn, col % tn, width)
        assert col % tn + width <= tn and tm % (t * F32_SUBLANES) == 0
        out_specs.append(pl.BlockSpec((t, tm // t, width), lambda i, j: (0, i, 0)))
        out_shape.append(jax.ShapeDtypeStruct((t, rows // t, width), BF16))
        scratch.append(pltpu.VMEM((width // LANES, tm, LANES), F32))
    out = pl.pallas_call(
        functools.partial(_norm_matmul_kernel, chunk_cols=chunk_cols),
        grid=(rows // tm, n // tn),
        in_specs=[
            pl.BlockSpec((tm, d), lambda i, j: (i, 0)),
            pl.BlockSpec((1, d), lambda i, j: (0, 0)),
            pl.BlockSpec((d, tn), lambda i, j: (0, j)),
        ],
        out_specs=out_specs,
        out_shape=out_shape,
        scratch_shapes=scratch,
        compiler_params=_params("parallel", "arbitrary"),
        name="norm_in_proj",
    )(x, g.reshape(1, d), w)
    return out[0] if chunk_major is None else out


def _na_kernel(q_ref, k_ref, v_ref, qm_ref, km_ref, vm_ref, tab_ref, o_ref, om_ref, bias_ref, sm_ref, acc_ref,
               *, rows, dh):
    lane = lax.broadcasted_iota(jnp.int32, (1, 2 * dh), 1)
    head_masks = (lane < dh, lane >= dh)
    scale = dh ** -0.5
    km = km_ref[...]
    vm = vm_ref[...]
    nt = (((1,), (1,)), ((), ()))

    masked = 2 * ROW_WIN - 1
    nblk = rows // NA_QROWS
    qblk = NA_QROWS * GRID_W
    kblk = NA_KROWS * GRID_W

    def rel_row(kind, j, i):
        if kind == 0:
            return i - j + (ROW_WIN - 1) if i < ROW_WIN else masked
        if kind == 1:
            return i - j + (ROW_WIN - 1 - ROW_WIN // 2) if j <= i < j + ROW_WIN else masked
        lo = NA_KROWS - ROW_WIN
        return i - j + (NA_QROWS - NA_KROWS) + (ROW_WIN - 1) if i >= lo else masked

    first_row = lane < GRID_W
    for kind in range(3):
        for hh in range(2):
            for j in range(NA_QROWS):
                r_lo = (hh * NA_QROWS + j) * GRID_W
                for i2 in range(NA_KROWS // 2):
                    bias_ref[kind, r_lo:r_lo + GRID_W, i2 * 2 * GRID_W:(i2 + 1) * 2 * GRID_W] = jnp.where(
                        first_row, tab_ref[hh, rel_row(kind, j, 2 * i2)], tab_ref[hh, rel_row(kind, j, 2 * i2 + 1)])

    def one_head(x, hh):
        return jnp.where(head_masks[hh], x, jnp.zeros_like(x))

    qm = (qm_ref[...] * scale).astype(BF16)
    om = None
    for hh in range(2):
        s = lax.dot_general(one_head(qm, hh), km, nt, preferred_element_type=F32)
        p = jnp.exp(s - jnp.max(s, axis=-1, keepdims=True))
        o_h = jnp.dot(p.astype(BF16), vm, preferred_element_type=F32) / jnp.sum(p, axis=-1, keepdims=True)
        om = o_h if om is None else jnp.where(head_masks[1], o_h, om)
    om_ref[...] = om.astype(om_ref.dtype)

    q_all = (q_ref[...] * scale).astype(BF16)
    for hh in range(2):
        sm_ref[hh] = lax.dot_general(one_head(q_all, hh), km, nt, preferred_element_type=F32)

    def body(blk, carry):
        r = blk * NA_QROWS
        k0 = jnp.clip(r - ROW_WIN // 2, 0, rows - NA_KROWS)
        kind = jnp.where(blk == 0, 0, jnp.where(blk == nblk - 1, 2, 1))
        qrows = pl.ds(pl.multiple_of(r * GRID_W, qblk), qblk)
        krows = pl.ds(pl.multiple_of(k0 * GRID_W, GRID_W), kblk)
        qs = (q_ref[qrows, :] * scale).astype(BF16)
        q2 = jnp.concatenate([one_head(qs, 0), one_head(qs, 1)], axis=0)
        s = lax.dot_general(q2, k_ref[krows, :], nt, preferred_element_type=F32) + bias_ref[kind]
        s_m = jnp.concatenate([sm_ref[0, qrows, :], sm_ref[1, qrows, :]], axis=0)
        m = jnp.maximum(jnp.max(s, axis=-1, keepdims=True), jnp.max(s_m, axis=-1, keepdims=True))
        p = jnp.exp(s - m)
        p_m = jnp.exp(s_m - m)
        inv = 1.0 / (jnp.sum(p, axis=-1, keepdims=True) + jnp.sum(p_m, axis=-1, keepdims=True))
        acc = jnp.dot(p.astype(BF16), v_ref[krows, :], preferred_element_type=F32) * inv
        acc_ref[qrows, :] = jnp.where(head_masks[1], acc[qblk:], acc[:qblk])
        p_m = p_m * inv
        sm_ref[0, qrows, :] = p_m[:qblk]
        sm_ref[1, qrows, :] = p_m[qblk:]
        return carry

    lax.fori_loop(0, nblk, body, 0, unroll=NA_UNROLL)

    o_meta = [jnp.dot(sm_ref[hh].astype(BF16), vm, preferred_element_type=F32) for hh in range(2)]
    o_ref[...] = (acc_ref[...] + jnp.where(head_masks[1], o_meta[1], o_meta[0])).astype(o_ref.dtype)


def _na_bias_table(rpb):
    c = jnp.arange(GRID_W)[:, None]
    kc = jnp.arange(GRID_W)[None, :]
    col_start = jnp.clip(c - COL_WIN // 2, 0, GRID_W - COL_WIN)
    valid = (kc >= col_start) & (kc < col_start + COL_WIN)
    dc = kc - c + (COL_WIN - 1)
    pick = (dc.reshape(-1)[None] == jnp.arange(2 * COL_WIN - 1)[:, None]).astype(F32)
    tab = jnp.dot(rpb.astype(F32).reshape(-1, 2 * COL_WIN - 1), pick, precision=lax.Precision.HIGHEST)
    tab = tab.reshape(rpb.shape[0], 2 * ROW_WIN - 1, GRID_W, GRID_W)
    tab = jnp.where(valid, tab, MASK_BIAS)
    tab = jnp.concatenate([tab, jnp.full_like(tab[:, :1], MASK_BIAS)], axis=1)
    return jnp.concatenate([tab, tab], axis=-1)


def neighbourhood_attention(proj_main, proj_meta, rpb, batch, seq, na_width):
    heads = rpb.shape[0]
    dh = na_width // heads
    rows = seq // GRID_W
    assert rows >= NA_KROWS and rows % NA_QROWS == 0 and heads % 2 == 0 and 2 * dh == LANES
    assert NA_KROWS % 2 == 0 and NA_KROWS >= ROW_WIN + NA_QROWS - 1 and NA_QROWS <= ROW_WIN // 2
    pairs = heads // 2
    bias = _na_bias_table(rpb)
    blk = (seq, 2 * dh)
    mblk = (N_META, 2 * dh)
    kernel = functools.partial(_na_kernel, rows=rows, dh=dh)
    return pl.pallas_call(
        kernel,
        grid=(batch, pairs),
        in_specs=[
            pl.BlockSpec(blk, lambda b, h: (b, h)),
            pl.BlockSpec(blk, lambda b, h: (b, pairs + h)),
            pl.BlockSpec(blk, lambda b, h: (b, 2 * pairs + h)),
            pl.BlockSpec(mblk, lambda b, h: (b, h)),
            pl.BlockSpec(mblk, lambda b, h: (b, pairs + h)),
            pl.BlockSpec(mblk, lambda b, h: (b, 2 * pairs + h)),
            pl.BlockSpec((2, 2 * ROW_WIN, GRID_W, 2 * GRID_W), lambda b, h: (h, 0, 0, 0)),
        ],
        out_specs=[
            pl.BlockSpec(blk, lambda b, h: (b, h)),
            pl.BlockSpec(mblk, lambda b, h: (b, h)),
        ],
        out_shape=[
            jax.ShapeDtypeStruct((batch * seq, na_width), BF16),
            jax.ShapeDtypeStruct((batch * N_META, na_width), BF16),
        ],
        scratch_shapes=[pltpu.VMEM((3, 2 * NA_QROWS * GRID_W, NA_KROWS * GRID_W), F32),
                        pltpu.VMEM((2, seq, N_META), F32),
                        pltpu.VMEM((seq, 2 * dh), F32)],
        compiler_params=_params("parallel", "parallel"),
        name="neighbourhood_attention",
    )(proj_main, proj_main, proj_main, proj_meta, proj_meta, proj_meta, bias)


def _chunk_ops_kernel(lag_ref, shift_ref, pw_ref, bt_ref, me_ref):
    gt, c, _ = lag_ref.shape
    t, _, tc = shift_ref.shape
    w = pw_ref.shape[3]
    lag = lag_ref[...].reshape(gt * c, 2 * tc).astype(BF16)
    b_r, b_i = bt_ref[:, 0], bt_ref[:, 1]
    for s in range(t):
        rows = slice(s * c, (s + 1) * c)
        blk = jnp.dot(lag, shift_ref[s], preferred_element_type=F32)
        me_ref[:, rows, 0:tc] = blk.reshape(gt, c, tc).astype(me_ref.dtype)
        e_r, e_i = pw_ref[:, 0, s:s + 1, :], pw_ref[:, 1, s:s + 1, :]
        me_ref[:, rows, tc:tc + w] = (e_r * b_r - e_i * b_i).astype(me_ref.dtype)
        me_ref[:, rows, tc + w:tc + 2 * w] = (e_r * b_i + e_i * b_r).astype(me_ref.dtype)


def _chunk_operators(lag, pw, bt):
    g, c, two_tc = lag.shape
    tc = two_tc // 2
    t = tc // c
    w = pw.shape[-1]
    gt = _pick(g, (16, 8, 4, 2, 1))
    p = jnp.arange(2 * tc)[None, :, None]
    q = jnp.arange(tc)[None, None, :]
    s = jnp.arange(t)[:, None, None]
    shift = jnp.where(p < tc, p == q - s * c, p - tc == q + (t - 1 - s) * c).astype(BF16)
    return pl.pallas_call(
        _chunk_ops_kernel,
        grid=(g // gt,),
        in_specs=[
            pl.BlockSpec((gt, c, 2 * tc), lambda i: (i, 0, 0)),
            pl.BlockSpec((t, 2 * tc, tc), lambda i: (0, 0, 0)),
            pl.BlockSpec((gt, 2, t, w), lambda i: (i, 0, 0, 0)),
            pl.BlockSpec((gt, 2, c, w), lambda i: (i, 0, 0, 0)),
        ],
        out_specs=pl.BlockSpec((gt, t * c, tc + 2 * w), lambda i: (i, 0, 0)),
        out_shape=jax.ShapeDtypeStruct((g, t * c, tc + 2 * w), BF16),
        compiler_params=_params("parallel"),
        name="s5_chunk_operators",
    )(lag, shift, pw, bt)


def _ssm_operators(lam_re, lam_im, log_step, b_re, b_im, c_re, c_im, d_skip):
    t = SSM_CHUNK
    hp = lax.Precision.HIGHEST
    lr, li = lam_re.astype(F32), lam_im.astype(F32)
    step = jnp.exp(log_step.astype(F32))[..., None]
    dt_r, dt_i = lr * step, li * step
    g, n = lr.shape[1], lr.shape[2]
    c = b_re.shape[-1]

    def lam_bar_pow(k):
        mag = jnp.exp(dt_r[:, :, None] * k)
        return mag * jnp.cos(dt_i[:, :, None] * k), mag * jnp.sin(dt_i[:, :, None] * k)

    p_r, p_i = lam_bar_pow(jnp.arange(t + 1, dtype=F32)[:, None])
    x_r, x_i = p_r[:, :, 1] - 1.0, p_i[:, :, 1]
    den = lr * lr + li * li
    q_r, q_i = (x_r * lr + x_i * li) / den, (x_i * lr - x_r * li) / den
    bt_r, bt_i = jnp.swapaxes(b_re.astype(F32), 2, 3), jnp.swapaxes(b_im.astype(F32), 2, 3)
    bb_r = q_r[:, :, None] * bt_r - q_i[:, :, None] * bt_i
    bb_i = q_r[:, :, None] * bt_i + q_i[:, :, None] * bt_r
    ct_r, ct_i = jnp.swapaxes(c_re.astype(F32), 2, 3), jnp.swapaxes(c_im.astype(F32), 2, 3)

    cx_r, cx_i = jnp.tile(ct_r, (1, 1, 1, t)), jnp.tile(ct_i, (1, 1, 1, t))

    def c_times_pow(d, pw_r, pw_i):
        e_r = jnp.repeat(jnp.swapaxes(pw_r, 1, 2), c, axis=2)
        e_i = jnp.repeat(jnp.swapaxes(pw_i, 1, 2), c, axis=2)
        return cx_r[d] * e_r - cx_i[d] * e_i, cx_r[d] * e_i + cx_i[d] * e_r

    def lag_rows(d, pw_r, pw_i):
        z_r, z_i = c_times_pow(d, pw_r, pw_i)
        return (jnp.einsum('gkn,gnq->gkq', bb_r[d], z_r, precision=hp)
                - jnp.einsum('gkn,gnq->gkq', bb_i[d], z_i, precision=hp))

    k_f = lag_rows(0, p_r[0][:, :t], p_i[0][:, :t])
    k_r = lag_rows(1, p_r[1][:, :t][:, ::-1], p_i[1][:, :t][:, ::-1])
    lag0 = k_r[:, :, (t - 1) * c:] + jnp.eye(c, dtype=F32)[None] * d_skip.astype(F32).reshape(g, 1, c)
    k_f = k_f.at[:, :, :c].add(lag0)
    k_r = k_r.at[:, :, (t - 1) * c:].set(0.0)
    lag = jnp.concatenate([k_f, k_r], axis=-1)

    pw = jnp.stack([jnp.concatenate([p_r[0][:, ::-1][:, 1:], p_r[1][:, :t]], axis=-1),
                    jnp.concatenate([p_i[0][:, ::-1][:, 1:], p_i[1][:, :t]], axis=-1)], axis=1)
    bt = jnp.stack([jnp.concatenate([bb_r[0], bb_r[1]], axis=-1),
                    jnp.concatenate([bb_i[0], bb_i[1]], axis=-1)], axis=1)
    me = _chunk_operators(lag, pw, bt)

    gf_r, gf_i = c_times_pow(0, p_r[0][:, 1:], p_i[0][:, 1:])
    gr_r, gr_i = c_times_pow(1, p_r[1][:, ::-1][:, :t], p_i[1][:, ::-1][:, :t])
    f = jnp.concatenate([gf_r, gr_r, -gf_i, -gr_i], axis=1)

    sub = F32_SUBLANES
    row = jnp.arange(sub)

    k_rows, keep = [], []
    for k in (1, 2, 4):
        k_rows.append(jnp.full((sub,), k))
        keep.append(row >= k)
    k_rows.append(row + 1)
    keep.append(row >= 0)
    for k in (1, 2, 4):
        k_rows.append(jnp.full((sub,), k))
        keep.append(row < sub - k)
    k_rows.append(sub - row)
    keep.append(row >= 0)
    k_all = (t * jnp.concatenate(k_rows)).astype(F32)
    keep_all = jnp.concatenate(keep)[None, :, None]
    a_r, a_i = lam_bar_pow(k_all[:, None])
    a_r = (jnp.concatenate([a_r[0], a_r[1]], axis=-1) * keep_all).reshape(g, len(k_rows), 1, sub, 2 * n)
    a_i = (jnp.concatenate([a_i[0], a_i[1]], axis=-1) * keep_all).reshape(g, len(k_rows), 1, sub, 2 * n)
    a = jnp.concatenate([a_r, a_i], axis=2).reshape(g, 2 * len(k_rows), sub, 2 * n)

    lanes_per_tok = 128
    toks = lanes_per_tok // c
    grp = lanes_per_tok // c
    src = jnp.transpose(jnp.arange(toks * grp * c).reshape(toks, grp, c), (1, 0, 2)).reshape(-1)
    perm = (jnp.arange(toks * grp * c)[:, None] == src[None, :]).astype(BF16)
    return perm, me.astype(BF16), f.astype(BF16), a.astype(F32)


def _ssm_kernel(*refs, batch, nchunks, nstate, gsub):
    t = SSM_CHUNK
    (u_ref, um_ref, p_ref, me_ref, f_ref, a_ref, y_ref, ym_ref,
     yi_ref, ee_ref, pf_ref, pr_ref, ycat_ref, ymcat_ref) = refs
    gtile = me_ref.shape[0]
    tc = f_ref.shape[2]
    toks = p_ref.shape[0] // LANES
    parts = t // toks
    sub = F32_SUBLANES
    w = 2 * nstate
    tiles = nchunks // sub
    perm = p_ref[...]
    row = lax.broadcasted_iota(jnp.int32, (sub, w), 0)
    lane = lax.broadcasted_iota(jnp.int32, (1, w), 1)
    rev_lane = lane >= nstate

    def group_major(token_refs):
        out = []
        for part in range(parts):
            x = jnp.concatenate([token_refs[part * toks + i] for i in range(toks)], axis=1)
            out.append(jnp.dot(x, perm, preferred_element_type=F32).astype(BF16))
        return out

    def group_chunk(regrouped, g):
        return jnp.concatenate([x[:, g * LANES:(g + 1) * LANES] for x in regrouped], axis=1)

    v_main = group_major(u_ref)
    v_meta = group_major(um_ref)

    def cmul_add(xr, xi, ar, ai, yr, yi):
        return xr + ar * yr - ai * yi, xi + ar * yi + ai * yr

    def bcast_row(x, r):
        return jnp.broadcast_to(x[r:r + 1, :], (sub, w))

    for sb in range(gtile // gsub):
        gs = [sb * gsub + gi for gi in range(gsub)]
        meta = []
        for gi, g in enumerate(gs):
            me = jnp.dot(group_chunk(v_main, g), me_ref[g], preferred_element_type=F32)
            yi_ref[gi] = me[:, :tc]
            ee_ref[gi] = me[:, tc:]
            meta.append(jnp.dot(group_chunk(v_meta, g), me_ref[g], preferred_element_type=F32))

        def body(j, carry, gs=gs):
            new = []
            for gi, g in enumerate(gs):
                for b in range(batch):
                    lr, li, fr, fi = carry[gi * batch + b]
                    fs = pl.ds(pl.multiple_of(b * nchunks + j * sub, sub), sub)
                    xr, xi = ee_ref[gi, fs, 0:w], ee_ref[gi, fs, w:2 * w]
                    for step, k in enumerate((1, 2, 4)):
                        xr, xi = cmul_add(xr, xi, a_ref[g, 2 * step], a_ref[g, 2 * step + 1],
                                          pltpu.roll(xr, k, 0), pltpu.roll(xi, k, 0))
                    sr, si = cmul_add(xr, xi, a_ref[g, 6], a_ref[g, 7], lr, li)
                    pf_ref[gi, fs, 0:w] = jnp.where(row == 0, lr, pltpu.roll(sr, 1, 0))
                    pf_ref[gi, fs, w:2 * w] = jnp.where(row == 0, li, pltpu.roll(si, 1, 0))
                    rs = pl.ds(pl.multiple_of(b * nchunks + (tiles - 1 - j) * sub, sub), sub)
                    xr, xi = ee_ref[gi, rs, 0:w], ee_ref[gi, rs, w:2 * w]
                    for step, k in enumerate((1, 2, 4)):
                        xr, xi = cmul_add(xr, xi, a_ref[g, 8 + 2 * step], a_ref[g, 9 + 2 * step],
                                          pltpu.roll(xr, sub - k, 0), pltpu.roll(xi, sub - k, 0))
                    rr, ri = cmul_add(xr, xi, a_ref[g, 14], a_ref[g, 15], fr, fi)
                    pr_ref[gi, rs, 0:w] = jnp.where(row == sub - 1, fr, pltpu.roll(rr, sub - 1, 0))
                    pr_ref[gi, rs, w:2 * w] = jnp.where(row == sub - 1, fi, pltpu.roll(ri, sub - 1, 0))
                    new.append((bcast_row(sr, sub - 1), bcast_row(si, sub - 1), bcast_row(rr, 0), bcast_row(ri, 0)))
            return tuple(new)

        zero = jnp.zeros((sub, w), F32)
        init = tuple((bcast_row(meta[gi][:, tc:tc + w], b), bcast_row(meta[gi][:, tc + w:tc + 2 * w], b), zero, zero)
                     for gi in range(gsub) for b in range(batch))
        final = lax.fori_loop(0, tiles, body, init, unroll=SSM_SCAN_UNROLL)

        for gi, g in enumerate(gs):
            prev = jnp.concatenate([jnp.where(rev_lane, pr_ref[gi, :, k * w:(k + 1) * w], pf_ref[gi, :, k * w:(k + 1) * w])
                                    for k in range(2)], axis=1).astype(BF16)
            y = (yi_ref[gi] + jnp.dot(prev, f_ref[g], preferred_element_type=F32)).astype(BF16)
            pm = []
            for k in (2, 3):
                x = zero
                for b in range(batch):
                    x = jnp.where((row == b) & rev_lane, final[gi * batch + b][k], x)
                pm.append(jnp.concatenate([x, jnp.zeros((SSM_META_ROWS - sub, w), F32)], axis=0))
            prev_m = jnp.concatenate(pm, axis=1).astype(BF16)
            y_m = (meta[gi][:, :tc] + jnp.dot(prev_m, f_ref[g], preferred_element_type=F32)).astype(BF16)
            for part in range(parts):
                ycat_ref[part, :, g * LANES:(g + 1) * LANES] = y[:, part * LANES:(part + 1) * LANES]
                ymcat_ref[part, :, g * LANES:(g + 1) * LANES] = y_m[:, part * LANES:(part + 1) * LANES]

    for part in range(parts):
        o = jnp.dot(ycat_ref[part], perm, preferred_element_type=F32).astype(y_ref.dtype)
        o_m = jnp.dot(ymcat_ref[part], perm, preferred_element_type=F32).astype(ym_ref.dtype)
        for i in range(toks):
            y_ref[part * toks + i] = o[:, i * LANES:(i + 1) * LANES]
            ym_ref[part * toks + i] = o_m[:, i * LANES:(i + 1) * LANES]


def s5_scan(u_main, u_meta_rows, ops, batch, seq):
    perm, me, f, a = ops
    g = me.shape[0]
    t = SSM_CHUNK
    tc = f.shape[2]
    c = tc // t
    width = g * c
    nstate = f.shape[1] // 4
    gtile = LANES // c
    nchunks = seq // t
    rows = batch * nchunks
    assert N_META == t and seq % (t * F32_SUBLANES) == 0 and batch <= F32_SUBLANES
    assert g % gtile == 0 and gtile % SSM_SUB == 0 and u_main.shape == (t, rows, width)

    u_meta = jnp.transpose(u_meta_rows.reshape(batch, t, width), (1, 0, 2))
    u_meta = jnp.pad(u_meta, ((0, 0), (0, SSM_META_ROWS - batch), (0, 0)))
    kernel = functools.partial(_ssm_kernel, batch=batch, nchunks=nchunks, nstate=nstate, gsub=SSM_SUB)
    y, y_m = pl.pallas_call(
        kernel,
        grid=(g // gtile,),
        in_specs=[
            pl.BlockSpec((t, rows, LANES), lambda i: (0, 0, i)),
            pl.BlockSpec((t, SSM_META_ROWS, LANES), lambda i: (0, 0, i)),
            pl.BlockSpec(perm.shape, lambda i: (0, 0)),
            pl.BlockSpec((gtile,) + me.shape[1:], lambda i: (i, 0, 0)),
            pl.BlockSpec((gtile,) + f.shape[1:], lambda i: (i, 0, 0)),
            pl.BlockSpec((gtile,) + a.shape[1:], lambda i: (i, 0, 0, 0)),
        ],
        out_specs=[
            pl.BlockSpec((t, rows, LANES), lambda i: (0, 0, i)),
            pl.BlockSpec((t, SSM_META_ROWS, LANES), lambda i: (0, 0, i)),
        ],
        out_shape=[
            jax.ShapeDtypeStruct((t, rows, width), BF16),
            jax.ShapeDtypeStruct((t, SSM_META_ROWS, width), BF16),
        ],
        scratch_shapes=[
            pltpu.VMEM((SSM_SUB, rows, tc), F32),
            pltpu.VMEM((SSM_SUB, rows, 4 * nstate), F32),
            pltpu.VMEM((SSM_SUB, rows, 4 * nstate), F32),
            pltpu.VMEM((SSM_SUB, rows, 4 * nstate), F32),
            pltpu.VMEM((t // (perm.shape[0] // LANES), rows, gtile * LANES), BF16),
            pltpu.VMEM((t // (perm.shape[0] // LANES), SSM_META_ROWS, gtile * LANES), BF16),
        ],
        compiler_params=_params("parallel"),
        name="s5_scan",
    )(u_main, u_meta, perm, me, f, a)

    y_meta = jnp.transpose(y_m[:, :batch], (1, 0, 2)).reshape(batch * N_META, width)
    return y, y_meta


def _mix_kernel(o_ref, y_ref, gna_ref, gssm_ref, wna_ref, wglu_ref, wssm_ref, out_ref, *stage, chunk_major):
    y_na = jnp.dot(o_ref[...], wna_ref[...], preferred_element_type=F32)
    if chunk_major:
        stage_ref, = stage
        t = y_ref.shape[0]
        ntile = stage_ref.shape[0]
        for k in range(t):
            for lt in range(ntile):
                stage_ref[lt, pl.ds(k, y_ref.shape[1], stride=t), :] = y_ref[
                    k, :, lt * LANES:(lt + 1) * LANES].astype(F32)
        y = jnp.concatenate([stage_ref[lt] for lt in range(ntile)], axis=1)
    else:
        y = y_ref[...].astype(F32)
    gl = _gelu(y)
    z = jnp.dot(gl.astype(BF16), wglu_ref[...], preferred_element_type=F32)
    t = (gl * _sigmoid(z)).astype(BF16)
    y_ssm = jnp.dot(t, wssm_ref[...], preferred_element_type=F32)
    mixed = _sigmoid(gna_ref[...].astype(F32)) * y_na + _sigmoid(gssm_ref[...].astype(F32)) * y_ssm
    out_ref[...] = mixed.astype(out_ref.dtype)


def mix_branches(o_na, y, proj, w_na, w_glu, w_ssm, gate_col):
    rows, na_width = o_na.shape
    ssm_width = y.shape[-1]
    d = w_na.shape[1]
    tm = _pick(rows, MID_ROW_TILES)
    assert gate_col % d == 0
    gblk = gate_col // d
    resident = dict(pipeline_mode=pl.Buffered(1))
    chunk_major = y.ndim == 3
    if chunk_major:
        t = y.shape[0]
        y_spec = pl.BlockSpec((t, tm // t, ssm_width), lambda i: (0, i, 0))
        scratch = [pltpu.VMEM((ssm_width // LANES, tm, LANES), F32)]
    else:
        y_spec = pl.BlockSpec((tm, ssm_width), lambda i: (i, 0))
        scratch = []
    return pl.pallas_call(
        functools.partial(_mix_kernel, chunk_major=chunk_major),
        grid=(rows // tm,),
        scratch_shapes=scratch,
        in_specs=[
            pl.BlockSpec((tm, na_width), lambda i: (i, 0)),
            y_spec,
            pl.BlockSpec((tm, d), lambda i: (i, gblk)),
            pl.BlockSpec((tm, d), lambda i: (i, gblk + 1)),
            pl.BlockSpec(w_na.shape, lambda i: (0, 0), **resident),
            pl.BlockSpec(w_glu.shape, lambda i: (0, 0), **resident),
            pl.BlockSpec(w_ssm.shape, lambda i: (0, 0), **resident),
        ],
        out_specs=pl.BlockSpec((tm, d), lambda i: (i, 0)),
        out_shape=jax.ShapeDtypeStruct((rows, d), BF16),
        compiler_params=_params("parallel"),
        name="mix_branches",
    )(o_na, y, proj, proj, w_na, w_glu, w_ssm)


def _residual_matmul_kernel(h_ref, a_ref, w_ref, g_ref, o_ref, n_ref):
    h1 = h_ref[...] + jnp.dot(a_ref[...], w_ref[...], preferred_element_type=F32)
    o_ref[...] = h1
    n_ref[...] = _rms(h1, g_ref[...]).astype(n_ref.dtype)


def residual_matmul(h, a, w, g):
    rows, d = h.shape
    tm = _pick(rows, MID_ROW_TILES)
    row_spec = pl.BlockSpec((tm, d), lambda i: (i, 0))
    return pl.pallas_call(
        _residual_matmul_kernel,
        grid=(rows // tm,),
        in_specs=[
            row_spec,
            pl.BlockSpec((tm, a.shape[1]), lambda i: (i, 0)),
            pl.BlockSpec(w.shape, lambda i: (0, 0), pipeline_mode=pl.Buffered(1)),
            pl.BlockSpec((1, d), lambda i: (0, 0)),
        ],
        out_specs=[row_spec, row_spec],
        out_shape=[jax.ShapeDtypeStruct((rows, d), F32), jax.ShapeDtypeStruct((rows, d), BF16)],
        compiler_params=_params("parallel"),
        name="residual_out_proj",
    )(h, a, w, g.reshape(1, d))


HALO = BF16_SUBLANES


def _ffn_up_kernel(h_ref, prev_ref, next_ref, wa_ref, wg_ref, cw_ref, cb_ref, o_ref, hn_ref, *, tm):
    @pl.when(pl.program_id(1) == 0)
    def _():
        hn_ref[0:HALO, :] = prev_ref[...]
        hn_ref[HALO:HALO + tm, :] = h_ref[...]
        hn_ref[HALO + tm:, :] = next_ref[...]

    ext = tm + 2 * HALO
    a = jnp.dot(hn_ref[...], wa_ref[...], preferred_element_type=F32)
    gate = jnp.dot(hn_ref[HALO:HALO + tm, :], wg_ref[...], preferred_element_type=F32)
    a_prev = pltpu.roll(a, 1, 0)[HALO:HALO + tm]
    a_next = pltpu.roll(a, ext - 1, 0)[HALO:HALO + tm]
    conv = a_prev * cw_ref[0:1, :] + a[HALO:HALO + tm] * cw_ref[1:2, :] + a_next * cw_ref[2:3, :] + cb_ref[...]
    o_ref[...] = (_gelu(conv) * gate).astype(o_ref.dtype)


def ffn_up(h, halo_prev, halo_next, w_up, conv_w, conv_b, tm):
    rows, d = h.shape
    dff = conv_b.shape[0]
    tn = _pick(dff, COL_TILES)
    nj = dff // tn
    kernel = functools.partial(_ffn_up_kernel, tm=tm)
    return pl.pallas_call(
        kernel,
        grid=(rows // tm, nj),
        in_specs=[
            pl.BlockSpec((tm, d), lambda i, j: (i, 0)),
            pl.BlockSpec((HALO, d), lambda i, j: (i, 0)),
            pl.BlockSpec((HALO, d), lambda i, j: (i, 0)),
            pl.BlockSpec((d, tn), lambda i, j: (0, j)),
            pl.BlockSpec((d, tn), lambda i, j: (0, nj + j)),
            pl.BlockSpec((conv_w.shape[0], tn), lambda i, j: (0, j)),
            pl.BlockSpec((1, tn), lambda i, j: (0, j)),
        ],
        out_specs=pl.BlockSpec((tm, tn), lambda i, j: (i, j)),
        out_shape=jax.ShapeDtypeStruct((rows, dff), BF16),
        scratch_shapes=[pltpu.VMEM((tm + 2 * HALO, d), BF16)],
        compiler_params=_params("parallel", "arbitrary"),
        name="ffn_up_conv_gate",
    )(h, halo_prev, halo_next, w_up, w_up, conv_w, conv_b.reshape(1, dff))


def _ffn_down_kernel(h_ref, a_ref, w_ref, g_ref, o_ref):
    k = pl.program_id(1)

    @pl.when(k == 0)
    def _():
        o_ref[...] = h_ref[...]

    o_ref[...] += jnp.dot(a_ref[...], w_ref[...], preferred_element_type=F32)

    @pl.when(k == pl.num_programs(1) - 1)
    def _():
        o_ref[...] = _rms(o_ref[...], g_ref[...])


def ffn_down(h, act, w_down, g):
    rows, d = h.shape
    dff = act.shape[1]
    tm = _pick(rows, ROW_TILES)
    tk = _pick(dff, COL_TILES)
    return pl.pallas_call(
        _ffn_down_kernel,
        grid=(rows // tm, dff // tk),
        in_specs=[
            pl.BlockSpec((tm, d), lambda i, k: (i, 0)),
            pl.BlockSpec((tm, tk), lambda i, k: (i, k)),
            pl.BlockSpec((tk, d), lambda i, k: (k, 0)),
            pl.BlockSpec((1, d), lambda i, k: (0, 0)),
        ],
        out_specs=pl.BlockSpec((tm, d), lambda i, k: (i, 0)),
        out_shape=jax.ShapeDtypeStruct((rows, d), F32),
        compiler_params=_params("parallel", "arbitrary"),
        name="ffn_down_final_norm",
    )(h, act, w_down, g.reshape(1, d))


def _conv_halos(h1_main, h1_meta, batch, seq, tm):
    d = h1_main.shape[1]
    per_seq = seq // tm
    hm = h1_main.reshape(batch, per_seq, tm, d)
    meta_tail = h1_meta.reshape(batch, 1, N_META, d)[:, :, N_META - HALO:]
    prev = jnp.concatenate([meta_tail, hm[:, :-1, tm - HALO:]], axis=1)
    nxt = jnp.concatenate([hm[:, 1:, :HALO], jnp.zeros((batch, 1, HALO, d), h1_main.dtype)], axis=1)
    return prev.reshape(batch * per_seq * HALO, d), nxt.reshape(batch * per_seq * HALO, d)


def kernel(x, meta_tokens, norm1_g, w_in, na_rpb, ssm_lam_re, ssm_lam_im, ssm_log_step, ssm_b_re, ssm_b_im,
           ssm_c_re, ssm_c_im, ssm_d, w_glu, w_proj_na, w_proj_ssm, w_out, norm2_g, w_up, conv_w, conv_b,
           w_down, final_g):
    batch, seq, d = x.shape
    depth = w_in.shape[0]
    na_width = w_proj_na.shape[1]
    ssm_width = w_proj_ssm.shape[1]
    assert depth == 1 and N_META >= HALO
    l = 0
    h_main = x.reshape(batch * seq, d)
    h_meta = jnp.broadcast_to(meta_tokens.astype(x.dtype)[None], (batch, N_META, d)).reshape(batch * N_META, d)

    w_in_b = w_in[l].astype(BF16)
    u_col = 3 * na_width
    proj_main, u_main = norm_matmul(h_main, norm1_g[l], w_in_b, chunk_major=(u_col, ssm_width))
    proj_meta = norm_matmul(h_meta, norm1_g[l], w_in_b)

    o_main, o_meta = neighbourhood_attention(proj_main, proj_meta, na_rpb[l], batch, seq, na_width)

    ops = _ssm_operators(ssm_lam_re[l], ssm_lam_im[l], ssm_log_step[l], ssm_b_re[l], ssm_b_im[l],
                         ssm_c_re[l], ssm_c_im[l], ssm_d[l])
    y_main, y_meta = s5_scan(u_main, proj_meta[:, u_col:u_col + ssm_width], ops, batch, seq)

    w_na_b, w_glu_b, w_ssm_b = w_proj_na[l].astype(BF16), w_glu[l].astype(BF16), w_proj_ssm[l].astype(BF16)
    w_out_b = w_out[l].astype(BF16)
    gate_col = u_col + ssm_width
    h1 = []
    for h, o, y, proj in ((h_main, o_main, y_main, proj_main), (h_meta, o_meta, y_meta, proj_meta)):
        mixed = mix_branches(o, y, proj, w_na_b, w_glu_b, w_ssm_b, gate_col)
        h1.append(residual_matmul(h, mixed, w_out_b, norm2_g[l]))
    (h1_main, hn_main), (_, hn_meta) = h1

    tm = _pick(seq, ROW_TILES)
    halo_prev, halo_next = _conv_halos(hn_main, hn_meta, batch, seq, tm)
    act = ffn_up(hn_main, halo_prev, halo_next, w_up[l].astype(BF16), conv_w[l], conv_b[l], tm)
    out = ffn_down(h1_main, act, w_down[l].astype(BF16), final_g)
    return out.reshape(batch, seq, d)
```

```python
import functools
import math

import jax
import jax.numpy as jnp
from jax import lax
from jax.experimental import pallas as pl
from jax.experimental.pallas import tpu as pltpu

F32 = jnp.float32
BF16 = jnp.bfloat16

N_META = 16
GRID_W = 64
ROW_WIN = 8
COL_WIN = 16
RMS_EPS = 1e-6
SSM_CHUNK = 16
MASK_BIAS = -1e30
NA_QROWS = 4
NA_KROWS = 12
NA_UNROLL = 8

VMEM_LIMIT_BYTES = 56 * 1024 * 1024
F32_SUBLANES = 8
BF16_SUBLANES = 16
LANES = 128

SSM_SUB = 2
SSM_SCAN_UNROLL = True
SSM_META_ROWS = BF16_SUBLANES

ROW_TILES = (1024, 512, 256, 128, 64)
MID_ROW_TILES = (512, 256, 128, 64)
COL_TILES = (512, 256, 128)
WIDE_COL_TILES = (1024, 512, 256, 128)


def _pick(dim, prefs):
    for t in prefs:
        if dim % t == 0:
            return t
    return dim


def _params(*sem):
    return pltpu.CompilerParams(dimension_semantics=sem, vmem_limit_bytes=VMEM_LIMIT_BYTES)


def _rms(x, g):
    ms = jnp.mean(x * x, axis=-1, keepdims=True)
    return x * lax.rsqrt(ms + RMS_EPS) * g


def _gelu(x):
    c = math.sqrt(2.0 / math.pi)
    return 0.5 * x * (1.0 + jnp.tanh(c * (x + 0.044715 * (x * x * x))))


def _sigmoid(x):
    return 0.5 * jnp.tanh(0.5 * x) + 0.5


def _norm_matmul_kernel(x_ref, g_ref, w_ref, o_ref, *rest, chunk_cols):
    hn_ref = rest[-1] if chunk_cols is None else rest[1]

    @pl.when(pl.program_id(1) == 0)
    def _():
        hn_ref[...] = _rms(x_ref[...], g_ref[...]).astype(BF16)

    r = jnp.dot(hn_ref[...], w_ref[...], preferred_element_type=F32)
    o_ref[...] = r.astype(o_ref.dtype)

    if chunk_cols is not None:
        u3_ref, _, stage_ref = rest
        tile, off, width = chunk_cols
        t = SSM_CHUNK

        @pl.when(pl.program_id(1) == tile)
        def _():
            for lt in range(width // LANES):
                stage_ref[lt] = r[:, off + lt * LANES:off + (lt + 1) * LANES]
            for k in range(t):
                for lt in range(width // LANES):
                    u3_ref[k, :, lt * LANES:(lt + 1) * LANES] = stage_ref[
                        lt, pl.ds(k, stage_ref.shape[1] // t, stride=t), :].astype(u3_ref.dtype)


def norm_matmul(x, g, w, chunk_major=None):
    rows, d = x.shape
    n = w.shape[1]
    tm = _pick(rows, ROW_TILES)
    tn = _pick(n, WIDE_COL_TILES)
    out_specs = [pl.BlockSpec((tm, tn), lambda i, j: (i, j))]
    out_shape = [jax.ShapeDtypeStruct((rows, n), BF16)]
    scratch = [pltpu.VMEM((tm, d), BF16)]
    chunk_cols = None
    if chunk_major is not None:
        col, width = chunk_major
        t = SSM_CHUNK
        chunk_cols = (col // tn, col % tn, width)
        assert col % tn + width <= tn and tm % (t * F32_SUBLANES) == 0
        out_specs.append(pl.BlockSpec((t, tm // t, width), lambda i, j: (0, i, 0)))
        out_shape.append(jax.ShapeDtypeStruct((t, rows // t, width), BF16))
        scratch.append(pltpu.VMEM((width // LANES, tm, LANES), F32))
    out = pl.pallas_call(
        functools.partial(_norm_matmul_kernel, chunk_cols=chunk_cols),
        grid=(rows // tm, n // tn),
        in_specs=[
            pl.BlockSpec((tm, d), lambda i, j: (i, 0)),
            pl.BlockSpec((1, d), lambda i, j: (0, 0)),
            pl.BlockSpec((d, tn), lambda i, j: (0, j)),
        ],
        out_specs=out_specs,
        out_shape=out_shape,
        scratch_shapes=scratch,
        compiler_params=_params("parallel", "arbitrary"),
        name="norm_in_proj",
    )(x, g.reshape(1, d), w)
    return out[0] if chunk_major is None else out


def _na_kernel(q_ref, k_ref, v_ref, qm_ref, km_ref, vm_ref, tab_ref, o_ref, om_ref, bias_ref, sm_ref, acc_ref,
               *, rows, dh):
    lane = lax.broadcasted_iota(jnp.int32, (1, 2 * dh), 1)
    head_masks = (lane < dh, lane >= dh)
    scale = dh ** -0.5
    km = km_ref[...]
    vm = vm_ref[...]
    nt = (((1,), (1,)), ((), ()))

    masked = 2 * ROW_WIN - 1
    nblk = rows // NA_QROWS
    qblk = NA_QROWS * GRID_W
    kblk = NA_KROWS * GRID_W

    def rel_row(kind, j, i):
        if kind == 0:
            return i - j + (ROW_WIN - 1) if i < ROW_WIN else masked
        if kind == 1:
            return i - j + (ROW_WIN - 1 - ROW_WIN // 2) if j <= i < j + ROW_WIN else masked
        lo = NA_KROWS - ROW_WIN
        return i - j + (NA_QROWS - NA_KROWS) + (ROW_WIN - 1) if i >= lo else masked

    first_row = lane < GRID_W
    for kind in range(3):
        for hh in range(2):
            for j in range(NA_QROWS):
                r_lo = (hh * NA_QROWS + j) * GRID_W
                for i2 in range(NA_KROWS // 2):
                    bias_ref[kind, r_lo:r_lo + GRID_W, i2 * 2 * GRID_W:(i2 + 1) * 2 * GRID_W] = jnp.where(
                        first_row, tab_ref[hh, rel_row(kind, j, 2 * i2)], tab_ref[hh, rel_row(kind, j, 2 * i2 + 1)])

    def one_head(x, hh):
        return jnp.where(head_masks[hh], x, jnp.zeros_like(x))

    qm = (qm_ref[...] * scale).astype(BF16)
    om = None
    for hh in range(2):
        s = lax.dot_general(one_head(qm, hh), km, nt, preferred_element_type=F32)
        p = jnp.exp(s - jnp.max(s, axis=-1, keepdims=True))
        o_h = jnp.dot(p.astype(BF16), vm, preferred_element_type=F32) / jnp.sum(p, axis=-1, keepdims=True)
        om = o_h if om is None else jnp.where(head_masks[1], o_h, om)
    om_ref[...] = om.astype(om_ref.dtype)

    q_all = (q_ref[...] * scale).astype(BF16)
    for hh in range(2):
        sm_ref[hh] = lax.dot_general(one_head(q_all, hh), km, nt, preferred_element_type=F32)

    def body(blk, carry):
        r = blk * NA_QROWS
        k0 = jnp.clip(r - ROW_WIN // 2, 0, rows - NA_KROWS)
        kind = jnp.where(blk == 0, 0, jnp.where(blk == nblk - 1, 2, 1))
        qrows = pl.ds(pl.multiple_of(r * GRID_W, qblk), qblk)
        krows = pl.ds(pl.multiple_of(k0 * GRID_W, GRID_W), kblk)
        qs = (q_ref[qrows, :] * scale).astype(BF16)
        q2 = jnp.concatenate([one_head(qs, 0), one_head(qs, 1)], axis=0)
        s = lax.dot_general(q2, k_ref[krows, :], nt, preferred_element_type=F32) + bias_ref[kind]
        s_m = jnp.concatenate([sm_ref[0, qrows, :], sm_ref[1, qrows, :]], axis=0)
        m = jnp.maximum(jnp.max(s, axis=-1, keepdims=True), jnp.max(s_m, axis=-1, keepdims=True))
        p = jnp.exp(s - m)
        p_m = jnp.exp(s_m - m)
        inv = 1.0 / (jnp.sum(p, axis=-1, keepdims=True) + jnp.sum(p_m, axis=-1, keepdims=True))
        acc = jnp.dot(p.astype(BF16), v_ref[krows, :], preferred_element_type=F32) * inv
        acc_ref[qrows, :] = jnp.where(head_masks[1], acc[qblk:], acc[:qblk])
        p_m = p_m * inv
        sm_ref[0, qrows, :] = p_m[:qblk]
        sm_ref[1, qrows, :] = p_m[qblk:]
        return carry

    lax.fori_loop(0, nblk, body, 0, unroll=NA_UNROLL)

    o_meta = [jnp.dot(sm_ref[hh].astype(BF16), vm, preferred_element_type=F32) for hh in range(2)]
    o_ref[...] = (acc_ref[...] + jnp.where(head_masks[1], o_meta[1], o_meta[0])).astype(o_ref.dtype)


def _na_bias_table(rpb):
    c = jnp.arange(GRID_W)[:, None]
    kc = jnp.arange(GRID_W)[None, :]
    col_start = jnp.clip(c - COL_WIN // 2, 0, GRID_W - COL_WIN)
    valid = (kc >= col_start) & (kc < col_start + COL_WIN)
    dc = kc - c + (COL_WIN - 1)
    pick = (dc.reshape(-1)[None] == jnp.arange(2 * COL_WIN - 1)[:, None]).astype(F32)
    tab = jnp.dot(rpb.astype(F32).reshape(-1, 2 * COL_WIN - 1), pick, precision=lax.Precision.HIGHEST)
    tab = tab.reshape(rpb.shape[0], 2 * ROW_WIN - 1, GRID_W, GRID_W)
    tab = jnp.where(valid, tab, MASK_BIAS)
    tab = jnp.concatenate([tab, jnp.full_like(tab[:, :1], MASK_BIAS)], axis=1)
    return jnp.concatenate([tab, tab], axis=-1)


def neighbourhood_attention(proj_main, proj_meta, rpb, batch, seq, na_width):
    heads = rpb.shape[0]
    dh = na_width // heads
    rows = seq // GRID_W
    assert rows >= NA_KROWS and rows % NA_QROWS == 0 and heads % 2 == 0 and 2 * dh == LANES
    assert NA_KROWS % 2 == 0 and NA_KROWS >= ROW_WIN + NA_QROWS - 1 and NA_QROWS <= ROW_WIN // 2
    pairs = heads // 2
    bias = _na_bias_table(rpb)
    blk = (seq, 2 * dh)
    mblk = (N_META, 2 * dh)
    kernel = functools.partial(_na_kernel, rows=rows, dh=dh)
    return pl.pallas_call(
        kernel,
        grid=(batch, pairs),
        in_specs=[
            pl.BlockSpec(blk, lambda b, h: (b, h)),
            pl.BlockSpec(blk, lambda b, h: (b, pairs + h)),
            pl.BlockSpec(blk, lambda b, h: (b, 2 * pairs + h)),
            pl.BlockSpec(mblk, lambda b, h: (b, h)),
            pl.BlockSpec(mblk, lambda b, h: (b, pairs + h)),
            pl.BlockSpec(mblk, lambda b, h: (b, 2 * pairs + h)),
            pl.BlockSpec((2, 2 * ROW_WIN, GRID_W, 2 * GRID_W), lambda b, h: (h, 0, 0, 0)),
        ],
        out_specs=[
            pl.BlockSpec(blk, lambda b, h: (b, h)),
            pl.BlockSpec(mblk, lambda b, h: (b, h)),
        ],
        out_shape=[
            jax.ShapeDtypeStruct((batch * seq, na_width), BF16),
            jax.ShapeDtypeStruct((batch * N_META, na_width), BF16),
        ],
        scratch_shapes=[pltpu.VMEM((3, 2 * NA_QROWS * GRID_W, NA_KROWS * GRID_W), F32),
                        pltpu.VMEM((2, seq, N_META), F32),
                        pltpu.VMEM((seq, 2 * dh), F32)],
        compiler_params=_params("parallel", "parallel"),
        name="neighbourhood_attention",
    )(proj_main, proj_main, proj_main, proj_meta, proj_meta, proj_meta, bias)


def _chunk_ops_kernel(cp_ref, bn_ref, d_ref, shift_ref, pw_ref, bt_ref, me_ref, f_ref):
    gt, _, _, n, _ = cp_ref.shape
    c = bn_ref.shape[4]
    t, _, tc = shift_ref.shape
    w = pw_ref.shape[3]
    q = lax.broadcasted_iota(jnp.int32, (c, tc), 1)
    k_row = lax.broadcasted_iota(jnp.int32, (c, tc), 0)
    per_channel = (q % c == k_row).astype(BF16)
    per_position = (q // c == k_row).astype(BF16)

    def spread(x, mat):
        hi = x.astype(BF16)
        lo = (x - hi.astype(F32)).astype(BF16)
        return jnp.dot(hi, mat, preferred_element_type=F32) + jnp.dot(lo, mat, preferred_element_type=F32)

    def factor(d, k):
        return cp_ref[:, d, k].reshape(gt * n, cp_ref.shape[4])

    lag_rows = [[None, None] for _ in range(gt)]
    for d in range(2):
        cx_r, cx_i = spread(factor(d, 0), per_channel), spread(factor(d, 1), per_channel)
        e_r, e_i = spread(factor(d, 4), per_position), spread(factor(d, 5), per_position)
        f_ref[:, d * n:(d + 1) * n, :] = (cx_r * e_r - cx_i * e_i).reshape(gt, n, tc).astype(f_ref.dtype)
        f_ref[:, (2 + d) * n:(3 + d) * n, :] = (-(cx_r * e_i + cx_i * e_r)).reshape(gt, n, tc).astype(f_ref.dtype)
        e_r, e_i = spread(factor(d, 2), per_position), spread(factor(d, 3), per_position)
        z_r, z_i = cx_r * e_r - cx_i * e_i, cx_r * e_i + cx_i * e_r
        for g in range(gt):
            zr_g, zi_g = z_r[g * n:(g + 1) * n], z_i[g * n:(g + 1) * n]
            b_r, b_i = bn_ref[g, d, 0], bn_ref[g, d, 1]
            rows = jnp.zeros((c, tc), F32)
            for ci in range(c):
                row = jnp.sum(b_r[:, ci:ci + 1] * zr_g - b_i[:, ci:ci + 1] * zi_g, axis=0, keepdims=True)
                rows = jnp.where(k_row == ci, row, rows)
            lag_rows[g][d] = rows
    lag = jnp.concatenate([jnp.concatenate(lag_rows[g], axis=1) for g in range(gt)], axis=0).astype(BF16)

    lane = lax.broadcasted_iota(jnp.int32, (gt * c, tc), 1)
    c_in = lax.broadcasted_iota(jnp.int32, (gt * c, tc), 0) % c
    skip_gain = d_ref[...].reshape(gt * c, tc)
    b_r, b_i = bt_ref[:, 0], bt_ref[:, 1]
    for s in range(t):
        rows = slice(s * c, (s + 1) * c)
        blk = jnp.dot(lag, shift_ref[s], preferred_element_type=F32)
        blk = blk + jnp.where(lane == s * c + c_in, skip_gain, 0.0)
        me_ref[:, rows, 0:tc] = blk.reshape(gt, c, tc).astype(me_ref.dtype)
        e_r, e_i = pw_ref[:, 0, s:s + 1, :], pw_ref[:, 1, s:s + 1, :]
        me_ref[:, rows, tc:tc + w] = (e_r * b_r - e_i * b_i).astype(me_ref.dtype)
        me_ref[:, rows, tc + w:tc + 2 * w] = (e_r * b_i + e_i * b_r).astype(me_ref.dtype)


def _chunk_operators(cp, bn, d_rows, pw, bt):
    g, _, _, n, t = cp.shape
    c = bn.shape[4]
    tc = t * c
    w = pw.shape[-1]
    assert t == c
    gt = _pick(g, (8, 4, 2, 1))
    p = jnp.arange(2 * tc)[None, :, None]
    q = jnp.arange(tc)[None, None, :]
    s = jnp.arange(t)[:, None, None]
    shift = jnp.where(p < tc, p == q - s * c, p - tc == q + (t - 1 - s) * c).astype(BF16)

    def per_group(shape):
        return pl.BlockSpec((gt,) + tuple(shape), lambda i: (i,) + (0,) * len(shape))

    return pl.pallas_call(
        _chunk_ops_kernel,
        grid=(g // gt,),
        in_specs=[
            per_group(cp.shape[1:]), per_group(bn.shape[1:]), per_group(d_rows.shape[1:]),
            pl.BlockSpec(shift.shape, lambda i: (0, 0, 0)),
            per_group(pw.shape[1:]), per_group(bt.shape[1:]),
        ],
        out_specs=[per_group((tc, tc + 2 * w)), per_group((4 * n, tc))],
        out_shape=[jax.ShapeDtypeStruct((g, tc, tc + 2 * w), BF16), jax.ShapeDtypeStruct((g, 4 * n, tc), BF16)],
        compiler_params=_params("parallel"),
        name="s5_chunk_operators",
    )(cp, bn, d_rows, shift, pw, bt)


def _ssm_operators(lam_re, lam_im, log_step, b_re, b_im, c_re, c_im, d_skip):
    t = SSM_CHUNK
    lr, li = lam_re.astype(F32), lam_im.astype(F32)
    step = jnp.exp(log_step.astype(F32))[..., None]
    dt_r, dt_i = lr * step, li * step
    g, n = lr.shape[1], lr.shape[2]
    c = b_re.shape[-1]

    def lam_bar_pow(k):
        mag = jnp.exp(dt_r[:, :, None] * k)
        return mag * jnp.cos(dt_i[:, :, None] * k), mag * jnp.sin(dt_i[:, :, None] * k)

    p_r, p_i = lam_bar_pow(jnp.arange(t + 1, dtype=F32)[:, None])
    x_r, x_i = p_r[:, :, 1] - 1.0, p_i[:, :, 1]
    den = lr * lr + li * li
    q_r, q_i = (x_r * lr + x_i * li) / den, (x_i * lr - x_r * li) / den
    bt_r, bt_i = jnp.swapaxes(b_re.astype(F32), 2, 3), jnp.swapaxes(b_im.astype(F32), 2, 3)
    bb_r = q_r[:, :, None] * bt_r - q_i[:, :, None] * bt_i
    bb_i = q_r[:, :, None] * bt_i + q_i[:, :, None] * bt_r
    bn_r = q_r[..., None] * b_re.astype(F32) - q_i[..., None] * b_im.astype(F32)
    bn_i = q_r[..., None] * b_im.astype(F32) + q_i[..., None] * b_re.astype(F32)
    ct_r, ct_i = jnp.swapaxes(c_re.astype(F32), 2, 3), jnp.swapaxes(c_im.astype(F32), 2, 3)

    def cols(x):
        return jnp.swapaxes(x, 1, 2)

    factors = []
    for d, lag_sel, out_sel in ((0, lambda x: x[:, :t], lambda x: x[:, 1:]),
                                (1, lambda x: x[:, :t][:, ::-1], lambda x: x[:, ::-1][:, :t])):
        factors.append(jnp.stack([ct_r[d], ct_i[d], cols(lag_sel(p_r[d])), cols(lag_sel(p_i[d])),
                                  cols(out_sel(p_r[d])), cols(out_sel(p_i[d]))], axis=1))
    cp = jnp.stack(factors, axis=1)
    bn = jnp.stack([jnp.stack([bn_r[0], bn_i[0]], axis=1), jnp.stack([bn_r[1], bn_i[1]], axis=1)], axis=1)
    d_rows = jnp.broadcast_to(d_skip.astype(F32).reshape(g, c, 1), (g, c, t * c))

    pw = jnp.stack([jnp.concatenate([p_r[0][:, ::-1][:, 1:], p_r[1][:, :t]], axis=-1),
                    jnp.concatenate([p_i[0][:, ::-1][:, 1:], p_i[1][:, :t]], axis=-1)], axis=1)
    bt = jnp.stack([jnp.concatenate([bb_r[0], bb_r[1]], axis=-1),
                    jnp.concatenate([bb_i[0], bb_i[1]], axis=-1)], axis=1)
    me, f = _chunk_operators(cp, bn, d_rows, pw, bt)

    sub = F32_SUBLANES
    row = jnp.arange(sub)

    k_rows, keep = [], []
    for k in (1, 2, 4):
        k_rows.append(jnp.full((sub,), k))
        keep.append(row >= k)
    k_rows.append(row + 1)
    keep.append(row >= 0)
    for k in (1, 2, 4):
        k_rows.append(jnp.full((sub,), k))
        keep.append(row < sub - k)
    k_rows.append(sub - row)
    keep.append(row >= 0)
    k_all = (t * jnp.concatenate(k_rows)).astype(F32)
    keep_all = jnp.concatenate(keep)[None, :, None]
    a_r, a_i = lam_bar_pow(k_all[:, None])
    a_r = (jnp.concatenate([a_r[0], a_r[1]], axis=-1) * keep_all).reshape(g, len(k_rows), 1, sub, 2 * n)
    a_i = (jnp.concatenate([a_i[0], a_i[1]], axis=-1) * keep_all).reshape(g, len(k_rows), 1, sub, 2 * n)
    a = jnp.concatenate([a_r, a_i], axis=2).reshape(g, 2 * len(k_rows), sub, 2 * n)

    toks = LANES // c
    grp = LANES // c
    src = jnp.transpose(jnp.arange(toks * grp * c).reshape(toks, grp, c), (1, 0, 2)).reshape(-1)
    perm = (jnp.arange(toks * grp * c)[:, None] == src[None, :]).astype(BF16)
    return perm, me.astype(BF16), f.astype(BF16), a.astype(F32)


def _ssm_kernel(*refs, batch, nchunks, nstate, gsub):
    t = SSM_CHUNK
    (u_ref, um_ref, p_ref, me_ref, f_ref, a_ref, y_ref, ym_ref,
     yi_ref, ee_ref, pf_ref, pr_ref, ycat_ref, ymcat_ref) = refs
    gtile = me_ref.shape[0]
    tc = f_ref.shape[2]
    toks = p_ref.shape[0] // LANES
    parts = t // toks
    sub = F32_SUBLANES
    w = 2 * nstate
    tiles = nchunks // sub
    perm = p_ref[...]
    row = lax.broadcasted_iota(jnp.int32, (sub, w), 0)
    lane = lax.broadcasted_iota(jnp.int32, (1, w), 1)
    rev_lane = lane >= nstate

    def group_major(token_refs):
        out = []
        for part in range(parts):
            x = jnp.concatenate([token_refs[part * toks + i] for i in range(toks)], axis=1)
            out.append(jnp.dot(x, perm, preferred_element_type=F32).astype(BF16))
        return out

    def group_chunk(regrouped, g):
        return jnp.concatenate([x[:, g * LANES:(g + 1) * LANES] for x in regrouped], axis=1)

    v_main = group_major(u_ref)
    v_meta = group_major(um_ref)

    def cmul_add(xr, xi, ar, ai, yr, yi):
        return xr + ar * yr - ai * yi, xi + ar * yi + ai * yr

    def bcast_row(x, r):
        return jnp.broadcast_to(x[r:r + 1, :], (sub, w))

    for sb in range(gtile // gsub):
        gs = [sb * gsub + gi for gi in range(gsub)]
        meta = []
        for gi, g in enumerate(gs):
            me = jnp.dot(group_chunk(v_main, g), me_ref[g], preferred_element_type=F32)
            yi_ref[gi] = me[:, :tc]
            ee_ref[gi] = me[:, tc:]
            meta.append(jnp.dot(group_chunk(v_meta, g), me_ref[g], preferred_element_type=F32))

        def body(j, carry, gs=gs):
            new = []
            for gi, g in enumerate(gs):
                for b in range(batch):
                    lr, li, fr, fi = carry[gi * batch + b]
                    fs = pl.ds(pl.multiple_of(b * nchunks + j * sub, sub), sub)
                    xr, xi = ee_ref[gi, fs, 0:w], ee_ref[gi, fs, w:2 * w]
                    for step, k in enumerate((1, 2, 4)):
                        xr, xi = cmul_add(xr, xi, a_ref[g, 2 * step], a_ref[g, 2 * step + 1],
                                          pltpu.roll(xr, k, 0), pltpu.roll(xi, k, 0))
                    sr, si = cmul_add(xr, xi, a_ref[g, 6], a_ref[g, 7], lr, li)
                    pf_ref[gi, fs, 0:w] = jnp.where(row == 0, lr, pltpu.roll(sr, 1, 0))
                    pf_ref[gi, fs, w:2 * w] = jnp.where(row == 0, li, pltpu.roll(si, 1, 0))
                    rs = pl.ds(pl.multiple_of(b * nchunks + (tiles - 1 - j) * sub, sub), sub)
                    xr, xi = ee_ref[gi, rs, 0:w], ee_ref[gi, rs, w:2 * w]
                    for step, k in enumerate((1, 2, 4)):
                        xr, xi = cmul_add(xr, xi, a_ref[g, 8 + 2 * step], a_ref[g, 9 + 2 * step],
                                          pltpu.roll(xr, sub - k, 0), pltpu.roll(xi, sub - k, 0))
                    rr, ri = cmul_add(xr, xi, a_ref[g, 14], a_ref[g, 15], fr, fi)
                    pr_ref[gi, rs, 0:w] = jnp.where(row == sub - 1, fr, pltpu.roll(rr, sub - 1, 0))
                    pr_ref[gi, rs, w:2 * w] = jnp.where(row == sub - 1, fi, pltpu.roll(ri, sub - 1, 0))
                    new.append((bcast_row(sr, sub - 1), bcast_row(si, sub - 1), bcast_row(rr, 0), bcast_row(ri, 0)))
            return tuple(new)

        zero = jnp.zeros((sub, w), F32)
        init = tuple((bcast_row(meta[gi][:, tc:tc + w], b), bcast_row(meta[gi][:, tc + w:tc + 2 * w], b), zero, zero)
                     for gi in range(gsub) for b in range(batch))
        final = lax.fori_loop(0, tiles, body, init, unroll=SSM_SCAN_UNROLL)

        for gi, g in enumerate(gs):
            prev = jnp.concatenate([jnp.where(rev_lane, pr_ref[gi, :, k * w:(k + 1) * w], pf_ref[gi, :, k * w:(k + 1) * w])
                                    for k in range(2)], axis=1).astype(BF16)
            y = (yi_ref[gi] + jnp.dot(prev, f_ref[g], preferred_element_type=F32)).astype(BF16)
            pm = []
            for k in (2, 3):
                x = zero
                for b in range(batch):
                    x = jnp.where((row == b) & rev_lane, final[gi * batch + b][k], x)
                pm.append(jnp.concatenate([x, jnp.zeros((SSM_META_ROWS - sub, w), F32)], axis=0))
            prev_m = jnp.concatenate(pm, axis=1).astype(BF16)
            y_m = (meta[gi][:, :tc] + jnp.dot(prev_m, f_ref[g], preferred_element_type=F32)).astype(BF16)
            for part in range(parts):
                ycat_ref[part, :, g * LANES:(g + 1) * LANES] = y[:, part * LANES:(part + 1) * LANES]
                ymcat_ref[part, :, g * LANES:(g + 1) * LANES] = y_m[:, part * LANES:(part + 1) * LANES]

    for part in range(parts):
        o = jnp.dot(ycat_ref[part], perm, preferred_element_type=F32).astype(y_ref.dtype)
        o_m = jnp.dot(ymcat_ref[part], perm, preferred_element_type=F32).astype(ym_ref.dtype)
        for i in range(toks):
            y_ref[part * toks + i] = o[:, i * LANES:(i + 1) * LANES]
            ym_ref[part * toks + i] = o_m[:, i * LANES:(i + 1) * LANES]


def s5_scan(u_main, u_meta_rows, ops, batch, seq):
    perm, me, f, a = ops
    g = me.shape[0]
    t = SSM_CHUNK
    tc = f.shape[2]
    c = tc // t
    width = g * c
    nstate = f.shape[1] // 4
    gtile = LANES // c
    nchunks = seq // t
    rows = batch * nchunks
    assert N_META == t and seq % (t * F32_SUBLANES) == 0 and batch <= F32_SUBLANES
    assert g % gtile == 0 and gtile % SSM_SUB == 0 and u_main.shape == (t, rows, width)

    u_meta = jnp.transpose(u_meta_rows.reshape(batch, t, width), (1, 0, 2))
    u_meta = jnp.pad(u_meta, ((0, 0), (0, SSM_META_ROWS - batch), (0, 0)))
    kernel = functools.partial(_ssm_kernel, batch=batch, nchunks=nchunks, nstate=nstate, gsub=SSM_SUB)
    y, y_m = pl.pallas_call(
        kernel,
        grid=(g // gtile,),
        in_specs=[
            pl.BlockSpec((t, rows, LANES), lambda i: (0, 0, i)),
            pl.BlockSpec((t, SSM_META_ROWS, LANES), lambda i: (0, 0, i)),
            pl.BlockSpec(perm.shape, lambda i: (0, 0)),
            pl.BlockSpec((gtile,) + me.shape[1:], lambda i: (i, 0, 0)),
            pl.BlockSpec((gtile,) + f.shape[1:], lambda i: (i, 0, 0)),
            pl.BlockSpec((gtile,) + a.shape[1:], lambda i: (i, 0, 0, 0)),
        ],
        out_specs=[
            pl.BlockSpec((t, rows, LANES), lambda i: (0, 0, i)),
            pl.BlockSpec((t, SSM_META_ROWS, LANES), lambda i: (0, 0, i)),
        ],
        out_shape=[
            jax.ShapeDtypeStruct((t, rows, width), BF16),
            jax.ShapeDtypeStruct((t, SSM_META_ROWS, width), BF16),
        ],
        scratch_shapes=[
            pltpu.VMEM((SSM_SUB, rows, tc), F32),
            pltpu.VMEM((SSM_SUB, rows, 4 * nstate), F32),
            pltpu.VMEM((SSM_SUB, rows, 4 * nstate), F32),
            pltpu.VMEM((SSM_SUB, rows, 4 * nstate), F32),
            pltpu.VMEM((t // (perm.shape[0] // LANES), rows, gtile * LANES), BF16),
            pltpu.VMEM((t // (perm.shape[0] // LANES), SSM_META_ROWS, gtile * LANES), BF16),
        ],
        compiler_params=_params("parallel"),
        name="s5_scan",
    )(u_main, u_meta, perm, me, f, a)

    y_meta = jnp.transpose(y_m[:, :batch], (1, 0, 2)).reshape(batch * N_META, width)
    return y, y_meta


def _mix_kernel(o_ref, y_ref, gna_ref, gssm_ref, wna_ref, wglu_ref, wssm_ref, out_ref, *stage, chunk_major):
    y_na = jnp.dot(o_ref[...], wna_ref[...], preferred_element_type=F32)
    if chunk_major:
        stage_ref, = stage
        t = y_ref.shape[0]
        ntile = stage_ref.shape[0]
        for k in range(t):
            for lt in range(ntile):
                stage_ref[lt, pl.ds(k, y_ref.shape[1], stride=t), :] = y_ref[
                    k, :, lt * LANES:(lt + 1) * LANES].astype(F32)
        y = jnp.concatenate([stage_ref[lt] for lt in range(ntile)], axis=1)
    else:
        y = y_ref[...].astype(F32)
    gl = _gelu(y)
    z = jnp.dot(gl.astype(BF16), wglu_ref[...], preferred_element_type=F32)
    t = (gl * _sigmoid(z)).astype(BF16)
    y_ssm = jnp.dot(t, wssm_ref[...], preferred_element_type=F32)
    mixed = _sigmoid(gna_ref[...].astype(F32)) * y_na + _sigmoid(gssm_ref[...].astype(F32)) * y_ssm
    out_ref[...] = mixed.astype(out_ref.dtype)


def mix_branches(o_na, y, proj, w_na, w_glu, w_ssm, gate_col):
    rows, na_width = o_na.shape
    ssm_width = y.shape[-1]
    d = w_na.shape[1]
    tm = _pick(rows, MID_ROW_TILES)
    assert gate_col % d == 0
    gblk = gate_col // d
    resident = dict(pipeline_mode=pl.Buffered(1))
    chunk_major = y.ndim == 3
    if chunk_major:
        t = y.shape[0]
        y_spec = pl.BlockSpec((t, tm // t, ssm_width), lambda i: (0, i, 0))
        scratch = [pltpu.VMEM((ssm_width // LANES, tm, LANES), F32)]
    else:
        y_spec = pl.BlockSpec((tm, ssm_width), lambda i: (i, 0))
        scratch = []
    return pl.pallas_call(
        functools.partial(_mix_kernel, chunk_major=chunk_major),
        grid=(rows // tm,),
        scratch_shapes=scratch,
        in_specs=[
            pl.BlockSpec((tm, na_width), lambda i: (i, 0)),
            y_spec,
            pl.BlockSpec((tm, d), lambda i: (i, gblk)),
            pl.BlockSpec((tm, d), lambda i: (i, gblk + 1)),
            pl.BlockSpec(w_na.shape, lambda i: (0, 0), **resident),
            pl.BlockSpec(w_glu.shape, lambda i: (0, 0), **resident),
            pl.BlockSpec(w_ssm.shape, lambda i: (0, 0), **resident),
        ],
        out_specs=pl.BlockSpec((tm, d), lambda i: (i, 0)),
        out_shape=jax.ShapeDtypeStruct((rows, d), BF16),
        compiler_params=_params("parallel"),
        name="mix_branches",
    )(o_na, y, proj, proj, w_na, w_glu, w_ssm)


def _residual_matmul_kernel(h_ref, a_ref, w_ref, g_ref, o_ref, n_ref):
    h1 = h_ref[...] + jnp.dot(a_ref[...], w_ref[...], preferred_element_type=F32)
    o_ref[...] = h1
    n_ref[...] = _rms(h1, g_ref[...]).astype(n_ref.dtype)


def residual_matmul(h, a, w, g):
    rows, d = h.shape
    tm = _pick(rows, MID_ROW_TILES)
    row_spec = pl.BlockSpec((tm, d), lambda i: (i, 0))
    return pl.pallas_call(
        _residual_matmul_kernel,
        grid=(rows // tm,),
        in_specs=[
            row_spec,
            pl.BlockSpec((tm, a.shape[1]), lambda i: (i, 0)),
            pl.BlockSpec(w.shape, lambda i: (0, 0), pipeline_mode=pl.Buffered(1)),
            pl.BlockSpec((1, d), lambda i: (0, 0)),
        ],
        out_specs=[row_spec, row_spec],
        out_shape=[jax.ShapeDtypeStruct((rows, d), F32), jax.ShapeDtypeStruct((rows, d), BF16)],
        compiler_params=_params("parallel"),
        name="residual_out_proj",
    )(h, a, w, g.reshape(1, d))


HALO = BF16_SUBLANES


def _ffn_up_kernel(h_ref, prev_ref, next_ref, wa_ref, wg_ref, cw_ref, cb_ref, o_ref, hn_ref, *, tm):
    @pl.when(pl.program_id(1) == 0)
    def _():
        hn_ref[0:HALO, :] = prev_ref[...]
        hn_ref[HALO:HALO + tm, :] = h_ref[...]
        hn_ref[HALO + tm:, :] = next_ref[...]

    ext = tm + 2 * HALO
    a = jnp.dot(hn_ref[...], wa_ref[...], preferred_element_type=F32)
    gate = jnp.dot(hn_ref[HALO:HALO + tm, :], wg_ref[...], preferred_element_type=F32)
    a_prev = pltpu.roll(a, 1, 0)[HALO:HALO + tm]
    a_next = pltpu.roll(a, ext - 1, 0)[HALO:HALO + tm]
    conv = a_prev * cw_ref[0:1, :] + a[HALO:HALO + tm] * cw_ref[1:2, :] + a_next * cw_ref[2:3, :] + cb_ref[...]
    o_ref[...] = (_gelu(conv) * gate).astype(o_ref.dtype)


def ffn_up(h, halo_prev, halo_next, w_up, conv_w, conv_b, tm):
    rows, d = h.shape
    dff = conv_b.shape[0]
    tn = _pick(dff, COL_TILES)
    nj = dff // tn
    kernel = functools.partial(_ffn_up_kernel, tm=tm)
    return pl.pallas_call(
        kernel,
        grid=(rows // tm, nj),
        in_specs=[
            pl.BlockSpec((tm, d), lambda i, j: (i, 0)),
            pl.BlockSpec((HALO, d), lambda i, j: (i, 0)),
            pl.BlockSpec((HALO, d), lambda i, j: (i, 0)),
            pl.BlockSpec((d, tn), lambda i, j: (0, j)),
            pl.BlockSpec((d, tn), lambda i, j: (0, nj + j)),
            pl.BlockSpec((conv_w.shape[0], tn), lambda i, j: (0, j)),
            pl.BlockSpec((1, tn), lambda i, j: (0, j)),
        ],
        out_specs=pl.BlockSpec((tm, tn), lambda i, j: (i, j)),
        out_shape=jax.ShapeDtypeStruct((rows, dff), BF16),
        scratch_shapes=[pltpu.VMEM((tm + 2 * HALO, d), BF16)],
        compiler_params=_params("parallel", "arbitrary"),
        name="ffn_up_conv_gate",
    )(h, halo_prev, halo_next, w_up, w_up, conv_w, conv_b.reshape(1, dff))


def _ffn_down_kernel(h_ref, a_ref, w_ref, g_ref, o_ref):
    k = pl.program_id(1)

    @pl.when(k == 0)
    def _():
        o_ref[...] = h_ref[...]

    o_ref[...] += jnp.dot(a_ref[...], w_ref[...], preferred_element_type=F32)

    @pl.when(k == pl.num_programs(1) - 1)
    def _():
        o_ref[...] = _rms(o_ref[...], g_ref[...])


def ffn_down(h, act, w_down, g):
    rows, d = h.shape
    dff = act.shape[1]
    tm = _pick(rows, ROW_TILES)
    tk = _pick(dff, COL_TILES)
    return pl.pallas_call(
        _ffn_down_kernel,
        grid=(rows // tm, dff // tk),
        in_specs=[
            pl.BlockSpec((tm, d), lambda i, k: (i, 0)),
            pl.BlockSpec((tm, tk), lambda i, k: (i, k)),
            pl.BlockSpec((tk, d), lambda i, k: (k, 0)),
            pl.BlockSpec((1, d), lambda i, k: (0, 0)),
        ],
        out_specs=pl.BlockSpec((tm, d), lambda i, k: (i, 0)),
        out_shape=jax.ShapeDtypeStruct((rows, d), F32),
        compiler_params=_params("parallel", "arbitrary"),
        name="ffn_down_final_norm",
    )(h, act, w_down, g.reshape(1, d))


def _conv_halos(h1_main, h1_meta, batch, seq, tm):
    d = h1_main.shape[1]
    per_seq = seq // tm
    hm = h1_main.reshape(batch, per_seq, tm, d)
    meta_tail = h1_meta.reshape(batch, 1, N_META, d)[:, :, N_META - HALO:]
    prev = jnp.concatenate([meta_tail, hm[:, :-1, tm - HALO:]], axis=1)
    nxt = jnp.concatenate([hm[:, 1:, :HALO], jnp.zeros((batch, 1, HALO, d), h1_main.dtype)], axis=1)
    return prev.reshape(batch * per_seq * HALO, d), nxt.reshape(batch * per_seq * HALO, d)


def kernel(x, meta_tokens, norm1_g, w_in, na_rpb, ssm_lam_re, ssm_lam_im, ssm_log_step, ssm_b_re, ssm_b_im,
           ssm_c_re, ssm_c_im, ssm_d, w_glu, w_proj_na, w_proj_ssm, w_out, norm2_g, w_up, conv_w, conv_b,
           w_down, final_g):
    batch, seq, d = x.shape
    depth = w_in.shape[0]
    na_width = w_proj_na.shape[1]
    ssm_width = w_proj_ssm.shape[1]
    assert depth == 1 and N_META >= HALO
    l = 0
    h_main = x.reshape(batch * seq, d)
    h_meta = jnp.broadcast_to(meta_tokens.astype(x.dtype)[None], (batch, N_META, d)).reshape(batch * N_META, d)

    w_in_b = w_in[l].astype(BF16)
    u_col = 3 * na_width
    proj_main, u_main = norm_matmul(h_main, norm1_g[l], w_in_b, chunk_major=(u_col, ssm_width))
    proj_meta = norm_matmul(h_meta, norm1_g[l], w_in_b)

    o_main, o_meta = neighbourhood_attention(proj_main, proj_meta, na_rpb[l], batch, seq, na_width)

    ops = _ssm_operators(ssm_lam_re[l], ssm_lam_im[l], ssm_log_step[l], ssm_b_re[l], ssm_b_im[l],
                         ssm_c_re[l], ssm_c_im[l], ssm_d[l])
    y_main, y_meta = s5_scan(u_main, proj_meta[:, u_col:u_col + ssm_width], ops, batch, seq)

    w_na_b, w_glu_b, w_ssm_b = w_proj_na[l].astype(BF16), w_glu[l].astype(BF16), w_proj_ssm[l].astype(BF16)
    w_out_b = w_out[l].astype(BF16)
    gate_col = u_col + ssm_width
    h1 = []
    for h, o, y, proj in ((h_main, o_main, y_main, proj_main), (h_meta, o_meta, y_meta, proj_meta)):
        mixed = mix_branches(o, y, proj, w_na_b, w_glu_b, w_ssm_b, gate_col)
        h1.append(residual_matmul(h, mixed, w_out_b, norm2_g[l]))
    (h1_main, hn_main), (_, hn_meta) = h1

    tm = _pick(seq, ROW_TILES)
    halo_prev, halo_next = _conv_halos(hn_main, hn_meta, batch, seq, tm)
    act = ffn_up(hn_main, halo_prev, halo_next, w_up[l].astype(BF16), conv_w[l], conv_b[l], tm)
    out = ffn_down(h1_main, act, w_down[l].astype(BF16), final_g)
    return out.reshape(batch, seq, d)
```

```python
import functools
import math

import jax
import jax.numpy as jnp
from jax import lax
from jax.experimental import pallas as pl
from jax.experimental.pallas import tpu as pltpu

F32 = jnp.float32
BF16 = jnp.bfloat16

N_META = 16
GRID_W = 64
ROW_WIN = 8
COL_WIN = 16
RMS_EPS = 1e-6
SSM_CHUNK = 16
MASK_BIAS = -1e30
NA_QROWS = 4
NA_KROWS = 12
NA_UNROLL = 8

VMEM_LIMIT_BYTES = 56 * 1024 * 1024
F32_SUBLANES = 8
BF16_SUBLANES = 16
LANES = 128

SSM_SUB = 2
SSM_SCAN_UNROLL = True
SSM_META_ROWS = BF16_SUBLANES

ROW_TILES = (1024, 512, 256, 128, 64)
MID_ROW_TILES = (512, 256, 128, 64)
COL_TILES = (512, 256, 128)
WIDE_COL_TILES = (1024, 512, 256, 128)


def _pick(dim, prefs):
    for t in prefs:
        if dim % t == 0:
            return t
    return dim


def _params(*sem):
    return pltpu.CompilerParams(dimension_semantics=sem, vmem_limit_bytes=VMEM_LIMIT_BYTES)


def _rms(x, g):
    ms = jnp.mean(x * x, axis=-1, keepdims=True)
    return x * lax.rsqrt(ms + RMS_EPS) * g


def _gelu(x):
    c = math.sqrt(2.0 / math.pi)
    return 0.5 * x * (1.0 + jnp.tanh(c * (x + 0.044715 * (x * x * x))))


def _sigmoid(x):
    return 0.5 * jnp.tanh(0.5 * x) + 0.5


def _norm_matmul_kernel(x_ref, g_ref, w_ref, o_ref, *rest, chunk_cols):
    hn_ref = rest[-1] if chunk_cols is None else rest[1]

    @pl.when(pl.program_id(1) == 0)
    def _():
        hn_ref[...] = _rms(x_ref[...], g_ref[...]).astype(BF16)

    r = jnp.dot(hn_ref[...], w_ref[...].astype(BF16), preferred_element_type=F32)
    o_ref[...] = r.astype(o_ref.dtype)

    if chunk_cols is not None:
        u3_ref, _, stage_ref = rest
        tile, off, width = chunk_cols
        t = SSM_CHUNK

        @pl.when(pl.program_id(1) == tile)
        def _():
            for lt in range(width // LANES):
                stage_ref[lt] = r[:, off + lt * LANES:off + (lt + 1) * LANES]
            for k in range(t):
                for lt in range(width // LANES):
                    u3_ref[k, :, lt * LANES:(lt + 1) * LANES] = stage_ref[
                        lt, pl.ds(k, stage_ref.shape[1] // t, stride=t), :].astype(u3_ref.dtype)


def norm_matmul(x, g, w, chunk_major=None):
    rows, d = x.shape
    n = w.shape[1]
    tm = _pick(rows, ROW_TILES)
    tn = _pick(n, WIDE_COL_TILES)
    out_specs = [pl.BlockSpec((tm, tn), lambda i, j: (i, j))]
    out_shape = [jax.ShapeDtypeStruct((rows, n), BF16)]
    scratch = [pltpu.VMEM((tm, d), BF16)]
    chunk_cols = None
    if chunk_major is not None:
        col, width = chunk_major
        t = SSM_CHUNK
        chunk_cols = (col // tn, col % tn, width)
        assert col % tn + width <= tn and tm % (t * F32_SUBLANES) == 0
        out_specs.append(pl.BlockSpec((t, tm // t, width), lambda i, j: (0, i, 0)))
        out_shape.append(jax.ShapeDtypeStruct((t, rows // t, width), BF16))
        scratch.append(pltpu.VMEM((width // LANES, tm, LANES), F32))
    out = pl.pallas_call(
        functools.partial(_norm_matmul_kernel, chunk_cols=chunk_cols),
        grid=(rows // tm, n // tn),
        in_specs=[
            pl.BlockSpec((tm, d), lambda i, j: (i, 0)),
            pl.BlockSpec((1, d), lambda i, j: (0, 0)),
            pl.BlockSpec((d, tn), lambda i, j: (0, j)),
        ],
        out_specs=out_specs,
        out_shape=out_shape,
        scratch_shapes=scratch,
        compiler_params=_params("parallel", "arbitrary"),
        name="norm_in_proj",
    )(x, g.reshape(1, d), w)
    return out[0] if chunk_major is None else out


def _na_kernel(q_ref, k_ref, v_ref, qm_ref, km_ref, vm_ref, tab_ref, o_ref, om_ref, bias_ref, sm_ref, acc_ref,
               *, rows, dh):
    lane = lax.broadcasted_iota(jnp.int32, (1, 2 * dh), 1)
    head_masks = (lane < dh, lane >= dh)
    scale = dh ** -0.5
    km = km_ref[...]
    vm = vm_ref[...]
    nt = (((1,), (1,)), ((), ()))

    masked = 2 * ROW_WIN - 1
    nblk = rows // NA_QROWS
    qblk = NA_QROWS * GRID_W
    kblk = NA_KROWS * GRID_W

    def rel_row(kind, j, i):
        if kind == 0:
            return i - j + (ROW_WIN - 1) if i < ROW_WIN else masked
        if kind == 1:
            return i - j + (ROW_WIN - 1 - ROW_WIN // 2) if j <= i < j + ROW_WIN else masked
        lo = NA_KROWS - ROW_WIN
        return i - j + (NA_QROWS - NA_KROWS) + (ROW_WIN - 1) if i >= lo else masked

    first_row = lane < GRID_W
    for kind in range(3):
        for hh in range(2):
            for j in range(NA_QROWS):
                r_lo = (hh * NA_QROWS + j) * GRID_W
                for i2 in range(NA_KROWS // 2):
                    bias_ref[kind, r_lo:r_lo + GRID_W, i2 * 2 * GRID_W:(i2 + 1) * 2 * GRID_W] = jnp.where(
                        first_row, tab_ref[hh, rel_row(kind, j, 2 * i2)], tab_ref[hh, rel_row(kind, j, 2 * i2 + 1)])

    def one_head(x, hh):
        return jnp.where(head_masks[hh], x, jnp.zeros_like(x))

    qm = (qm_ref[...] * scale).astype(BF16)
    om = None
    for hh in range(2):
        s = lax.dot_general(one_head(qm, hh), km, nt, preferred_element_type=F32)
        p = jnp.exp(s - jnp.max(s, axis=-1, keepdims=True))
        o_h = jnp.dot(p.astype(BF16), vm, preferred_element_type=F32) / jnp.sum(p, axis=-1, keepdims=True)
        om = o_h if om is None else jnp.where(head_masks[1], o_h, om)
    om_ref[...] = om.astype(om_ref.dtype)

    q_all = (q_ref[...] * scale).astype(BF16)
    for hh in range(2):
        sm_ref[hh] = lax.dot_general(one_head(q_all, hh), km, nt, preferred_element_type=F32)

    def body(blk, carry):
        r = blk * NA_QROWS
        k0 = jnp.clip(r - ROW_WIN // 2, 0, rows - NA_KROWS)
        kind = jnp.where(blk == 0, 0, jnp.where(blk == nblk - 1, 2, 1))
        qrows = pl.ds(pl.multiple_of(r * GRID_W, qblk), qblk)
        krows = pl.ds(pl.multiple_of(k0 * GRID_W, GRID_W), kblk)
        qs = (q_ref[qrows, :] * scale).astype(BF16)
        q2 = jnp.concatenate([one_head(qs, 0), one_head(qs, 1)], axis=0)
        s = lax.dot_general(q2, k_ref[krows, :], nt, preferred_element_type=F32) + bias_ref[kind]
        s_m = jnp.concatenate([sm_ref[0, qrows, :], sm_ref[1, qrows, :]], axis=0)
        m = jnp.maximum(jnp.max(s, axis=-1, keepdims=True), jnp.max(s_m, axis=-1, keepdims=True))
        p = jnp.exp(s - m)
        p_m = jnp.exp(s_m - m)
        inv = 1.0 / (jnp.sum(p, axis=-1, keepdims=True) + jnp.sum(p_m, axis=-1, keepdims=True))
        acc = jnp.dot(p.astype(BF16), v_ref[krows, :], preferred_element_type=F32) * inv
        acc_ref[qrows, :] = jnp.where(head_masks[1], acc[qblk:], acc[:qblk])
        p_m = p_m * inv
        sm_ref[0, qrows, :] = p_m[:qblk]
        sm_ref[1, qrows, :] = p_m[qblk:]
        return carry

    lax.fori_loop(0, nblk, body, 0, unroll=NA_UNROLL)

    o_meta = [jnp.dot(sm_ref[hh].astype(BF16), vm, preferred_element_type=F32) for hh in range(2)]
    o_ref[...] = (acc_ref[...] + jnp.where(head_masks[1], o_meta[1], o_meta[0])).astype(o_ref.dtype)


def _na_bias_table(rpb):
    c = jnp.arange(GRID_W)[:, None]
    kc = jnp.arange(GRID_W)[None, :]
    col_start = jnp.clip(c - COL_WIN // 2, 0, GRID_W - COL_WIN)
    valid = (kc >= col_start) & (kc < col_start + COL_WIN)
    dc = kc - c + (COL_WIN - 1)
    pick = (dc.reshape(-1)[None] == jnp.arange(2 * COL_WIN - 1)[:, None]).astype(F32)
    tab = jnp.dot(rpb.astype(F32).reshape(-1, 2 * COL_WIN - 1), pick, precision=lax.Precision.HIGHEST)
    tab = tab.reshape(rpb.shape[0], 2 * ROW_WIN - 1, GRID_W, GRID_W)
    tab = jnp.where(valid, tab, MASK_BIAS)
    tab = jnp.concatenate([tab, jnp.full_like(tab[:, :1], MASK_BIAS)], axis=1)
    return jnp.concatenate([tab, tab], axis=-1)


def neighbourhood_attention(proj_main, proj_meta, rpb, batch, seq, na_width):
    heads = rpb.shape[0]
    dh = na_width // heads
    rows = seq // GRID_W
    assert rows >= NA_KROWS and rows % NA_QROWS == 0 and heads % 2 == 0 and 2 * dh == LANES
    assert NA_KROWS % 2 == 0 and NA_KROWS >= ROW_WIN + NA_QROWS - 1 and NA_QROWS <= ROW_WIN // 2
    pairs = heads // 2
    bias = _na_bias_table(rpb)
    blk = (seq, 2 * dh)
    mblk = (N_META, 2 * dh)
    kernel = functools.partial(_na_kernel, rows=rows, dh=dh)
    return pl.pallas_call(
        kernel,
        grid=(batch, pairs),
        in_specs=[
            pl.BlockSpec(blk, lambda b, h: (b, h)),
            pl.BlockSpec(blk, lambda b, h: (b, pairs + h)),
            pl.BlockSpec(blk, lambda b, h: (b, 2 * pairs + h)),
            pl.BlockSpec(mblk, lambda b, h: (b, h)),
            pl.BlockSpec(mblk, lambda b, h: (b, pairs + h)),
            pl.BlockSpec(mblk, lambda b, h: (b, 2 * pairs + h)),
            pl.BlockSpec((2, 2 * ROW_WIN, GRID_W, 2 * GRID_W), lambda b, h: (h, 0, 0, 0)),
        ],
        out_specs=[
            pl.BlockSpec(blk, lambda b, h: (b, h)),
            pl.BlockSpec(mblk, lambda b, h: (b, h)),
        ],
        out_shape=[
            jax.ShapeDtypeStruct((batch * seq, na_width), BF16),
            jax.ShapeDtypeStruct((batch * N_META, na_width), BF16),
        ],
        scratch_shapes=[pltpu.VMEM((3, 2 * NA_QROWS * GRID_W, NA_KROWS * GRID_W), F32),
                        pltpu.VMEM((2, seq, N_META), F32),
                        pltpu.VMEM((seq, 2 * dh), F32)],
        compiler_params=_params("parallel", "parallel"),
        name="neighbourhood_attention",
    )(proj_main, proj_main, proj_main, proj_meta, proj_meta, proj_meta, bias)


def _chunk_ops_kernel(cp_ref, d_ref, shift_ref, pw_ref, bt_ref, me_ref, f_ref):
    gt, _, n, packed = cp_ref.shape
    c = bt_ref.shape[2]
    t, _, tc = shift_ref.shape
    w = pw_ref.shape[3]
    q = lax.broadcasted_iota(jnp.int32, (packed, tc), 1)
    p_lane = lax.broadcasted_iota(jnp.int32, (packed, tc), 0)
    k_row = lax.broadcasted_iota(jnp.int32, (c, tc), 0)

    def spread(parts, k, by_position):
        mat = (p_lane - k * c == (q // c if by_position else q % c)).astype(BF16)
        return sum(jnp.dot(x, mat, preferred_element_type=F32) for x in parts)

    lag_rows = [[None, None] for _ in range(gt)]
    for d in range(2):
        x = cp_ref[:, d].reshape(gt * n, packed)
        hi = x.astype(BF16)
        parts = (hi, (x - hi.astype(F32)).astype(BF16))
        cx_r, cx_i = spread(parts, 0, False), spread(parts, 1, False)
        e_r, e_i = spread(parts, 4, True), spread(parts, 5, True)
        f_ref[:, d * n:(d + 1) * n, :] = (cx_r * e_r - cx_i * e_i).reshape(gt, n, tc).astype(f_ref.dtype)
        f_ref[:, (2 + d) * n:(3 + d) * n, :] = (-(cx_r * e_i + cx_i * e_r)).reshape(gt, n, tc).astype(f_ref.dtype)
        e_r, e_i = spread(parts, 2, True), spread(parts, 3, True)
        z_r, z_i = cx_r * e_r - cx_i * e_i, cx_r * e_i + cx_i * e_r
        for g in range(gt):
            zr_g, zi_g = z_r[g * n:(g + 1) * n], z_i[g * n:(g + 1) * n]
            x_g = cp_ref[g, d]
            rows = jnp.zeros((c, tc), F32)
            for ci in range(c):
                b_r, b_i = x_g[:, 6 * c + ci:6 * c + ci + 1], x_g[:, 7 * c + ci:7 * c + ci + 1]
                row = jnp.sum(b_r * zr_g - b_i * zi_g, axis=0, keepdims=True)
                rows = jnp.where(k_row == ci, row, rows)
            lag_rows[g][d] = rows
    lag = jnp.concatenate([jnp.concatenate(lag_rows[g], axis=1) for g in range(gt)], axis=0).astype(BF16)

    lane = lax.broadcasted_iota(jnp.int32, (gt * c, tc), 1)
    c_in = lax.broadcasted_iota(jnp.int32, (gt * c, tc), 0) % c
    skip_gain = d_ref[...].reshape(gt * c, tc)
    b_r, b_i = bt_ref[:, 0], bt_ref[:, 1]
    for s in range(t):
        rows = slice(s * c, (s + 1) * c)
        blk = jnp.dot(lag, shift_ref[s], preferred_element_type=F32)
        blk = blk + jnp.where(lane == s * c + c_in, skip_gain, 0.0)
        me_ref[:, rows, 0:tc] = blk.reshape(gt, c, tc).astype(me_ref.dtype)
        e_r, e_i = pw_ref[:, 0, s:s + 1, :], pw_ref[:, 1, s:s + 1, :]
        me_ref[:, rows, tc:tc + w] = (e_r * b_r - e_i * b_i).astype(me_ref.dtype)
        me_ref[:, rows, tc + w:tc + 2 * w] = (e_r * b_i + e_i * b_r).astype(me_ref.dtype)


def _chunk_operators(cp, d_rows, pw, bt):
    g, _, n, packed = cp.shape
    t, c = pw.shape[2], bt.shape[2]
    tc = t * c
    w = pw.shape[-1]
    assert t == c and packed == 8 * c
    gt = _pick(g, (8, 4, 2, 1))
    p = jnp.arange(2 * tc)[None, :, None]
    q = jnp.arange(tc)[None, None, :]
    s = jnp.arange(t)[:, None, None]
    shift = jnp.where(p < tc, p == q - s * c, p - tc == q + (t - 1 - s) * c).astype(BF16)

    def per_group(shape):
        return pl.BlockSpec((gt,) + tuple(shape), lambda i: (i,) + (0,) * len(shape))

    return pl.pallas_call(
        _chunk_ops_kernel,
        grid=(g // gt,),
        in_specs=[
            per_group(cp.shape[1:]), per_group(d_rows.shape[1:]),
            pl.BlockSpec(shift.shape, lambda i: (0, 0, 0)),
            per_group(pw.shape[1:]), per_group(bt.shape[1:]),
        ],
        out_specs=[per_group((tc, tc + 2 * w)), per_group((4 * n, tc))],
        out_shape=[jax.ShapeDtypeStruct((g, tc, tc + 2 * w), BF16), jax.ShapeDtypeStruct((g, 4 * n, tc), BF16)],
        compiler_params=_params("parallel"),
        name="s5_chunk_operators",
    )(cp, d_rows, shift, pw, bt)


def _ssm_operators(lam_re, lam_im, log_step, b_re, b_im, c_re, c_im, d_skip):
    t = SSM_CHUNK
    lr, li = lam_re.astype(F32), lam_im.astype(F32)
    step = jnp.exp(log_step.astype(F32))[..., None]
    dt_r, dt_i = lr * step, li * step
    g, n = lr.shape[1], lr.shape[2]
    c = b_re.shape[-1]

    def lam_bar_pow(k):
        mag = jnp.exp(dt_r[:, :, None] * k)
        return mag * jnp.cos(dt_i[:, :, None] * k), mag * jnp.sin(dt_i[:, :, None] * k)

    p_r, p_i = lam_bar_pow(jnp.arange(t + 1, dtype=F32)[:, None])
    x_r, x_i = p_r[:, :, 1] - 1.0, p_i[:, :, 1]
    den = lr * lr + li * li
    q_r, q_i = (x_r * lr + x_i * li) / den, (x_i * lr - x_r * li) / den
    bt_r, bt_i = jnp.swapaxes(b_re.astype(F32), 2, 3), jnp.swapaxes(b_im.astype(F32), 2, 3)
    bb_r = q_r[:, :, None] * bt_r - q_i[:, :, None] * bt_i
    bb_i = q_r[:, :, None] * bt_i + q_i[:, :, None] * bt_r
    bn_r = q_r[..., None] * b_re.astype(F32) - q_i[..., None] * b_im.astype(F32)
    bn_i = q_r[..., None] * b_im.astype(F32) + q_i[..., None] * b_re.astype(F32)
    ct_r, ct_i = jnp.swapaxes(c_re.astype(F32), 2, 3), jnp.swapaxes(c_im.astype(F32), 2, 3)

    def cols(x):
        return jnp.swapaxes(x, 1, 2)

    factors = []
    for d, lag_sel, out_sel in ((0, lambda x: x[:, :t], lambda x: x[:, 1:]),
                                (1, lambda x: x[:, :t][:, ::-1], lambda x: x[:, ::-1][:, :t])):
        factors.append(jnp.concatenate([ct_r[d], ct_i[d], cols(lag_sel(p_r[d])), cols(lag_sel(p_i[d])),
                                        cols(out_sel(p_r[d])), cols(out_sel(p_i[d])), bn_r[d], bn_i[d]], axis=-1))
    cp = jnp.stack(factors, axis=1)
    d_rows = jnp.broadcast_to(d_skip.astype(F32).reshape(g, c, 1), (g, c, t * c))

    pw = jnp.stack([jnp.concatenate([p_r[0][:, ::-1][:, 1:], p_r[1][:, :t]], axis=-1),
                    jnp.concatenate([p_i[0][:, ::-1][:, 1:], p_i[1][:, :t]], axis=-1)], axis=1)
    bt = jnp.stack([jnp.concatenate([bb_r[0], bb_r[1]], axis=-1),
                    jnp.concatenate([bb_i[0], bb_i[1]], axis=-1)], axis=1)
    me, f = _chunk_operators(cp, d_rows, pw, bt)

    sub = F32_SUBLANES
    row = jnp.arange(sub)

    k_rows, keep = [], []
    for k in (1, 2, 4):
        k_rows.append(jnp.full((sub,), k))
        keep.append(row >= k)
    k_rows.append(row + 1)
    keep.append(row >= 0)
    for k in (1, 2, 4):
        k_rows.append(jnp.full((sub,), k))
        keep.append(row < sub - k)
    k_rows.append(sub - row)
    keep.append(row >= 0)
    k_all = jnp.concatenate(k_rows)
    keep_all = jnp.concatenate(keep)[None, :, None]
    d_r, d_i = lam_bar_pow((t * (row + 1)).astype(F32)[:, None])
    d_r = jnp.concatenate([d_r[0], d_r[1]], axis=-1)
    d_i = jnp.concatenate([d_i[0], d_i[1]], axis=-1)
    pick = (k_all[:, None] == row[None] + 1).astype(F32)
    a_r = (jnp.sum(pick[None, :, :, None] * d_r[:, None], axis=2) * keep_all).reshape(g, len(k_rows), 1, sub, 2 * n)
    a_i = (jnp.sum(pick[None, :, :, None] * d_i[:, None], axis=2) * keep_all).reshape(g, len(k_rows), 1, sub, 2 * n)
    a = jnp.concatenate([a_r, a_i], axis=2).reshape(g, 2 * len(k_rows), sub, 2 * n)

    toks = LANES // c
    grp = LANES // c
    src = jnp.transpose(jnp.arange(toks * grp * c).reshape(toks, grp, c), (1, 0, 2)).reshape(-1)
    perm = (jnp.arange(toks * grp * c)[:, None] == src[None, :]).astype(BF16)
    return perm, me.astype(BF16), f.astype(BF16), a.astype(F32)


def _ssm_kernel(*refs, batch, nchunks, nstate, gsub):
    t = SSM_CHUNK
    (u_ref, um_ref, p_ref, me_ref, f_ref, a_ref, y_ref, ym_ref,
     yi_ref, ee_ref, pf_ref, pr_ref, ycat_ref, ymcat_ref) = refs
    gtile = me_ref.shape[0]
    tc = f_ref.shape[2]
    toks = p_ref.shape[0] // LANES
    parts = t // toks
    sub = F32_SUBLANES
    w = 2 * nstate
    tiles = nchunks // sub
    perm = p_ref[...]
    row = lax.broadcasted_iota(jnp.int32, (sub, w), 0)
    lane = lax.broadcasted_iota(jnp.int32, (1, w), 1)
    rev_lane = lane >= nstate

    def group_major(token_refs):
        out = []
        for part in range(parts):
            x = jnp.concatenate([token_refs[part * toks + i] for i in range(toks)], axis=1)
            out.append(jnp.dot(x, perm, preferred_element_type=F32).astype(BF16))
        return out

    def group_chunk(regrouped, g):
        return jnp.concatenate([x[:, g * LANES:(g + 1) * LANES] for x in regrouped], axis=1)

    v_main = group_major(u_ref)
    v_meta = group_major(um_ref)

    def cmul_add(xr, xi, ar, ai, yr, yi):
        return xr + ar * yr - ai * yi, xi + ar * yi + ai * yr

    def bcast_row(x, r):
        return jnp.broadcast_to(x[r:r + 1, :], (sub, w))

    for sb in range(gtile // gsub):
        gs = [sb * gsub + gi for gi in range(gsub)]
        meta = []
        for gi, g in enumerate(gs):
            me = jnp.dot(group_chunk(v_main, g), me_ref[g], preferred_element_type=F32)
            yi_ref[gi] = me[:, :tc]
            ee_ref[gi] = me[:, tc:]
            meta.append(jnp.dot(group_chunk(v_meta, g), me_ref[g], preferred_element_type=F32))

        def body(j, carry, gs=gs):
            new = []
            for gi, g in enumerate(gs):
                for b in range(batch):
                    lr, li, fr, fi = carry[gi * batch + b]
                    fs = pl.ds(pl.multiple_of(b * nchunks + j * sub, sub), sub)
                    xr, xi = ee_ref[gi, fs, 0:w], ee_ref[gi, fs, w:2 * w]
                    for step, k in enumerate((1, 2, 4)):
                        xr, xi = cmul_add(xr, xi, a_ref[g, 2 * step], a_ref[g, 2 * step + 1],
                                          pltpu.roll(xr, k, 0), pltpu.roll(xi, k, 0))
                    sr, si = cmul_add(xr, xi, a_ref[g, 6], a_ref[g, 7], lr, li)
                    pf_ref[gi, fs, 0:w] = jnp.where(row == 0, lr, pltpu.roll(sr, 1, 0))
                    pf_ref[gi, fs, w:2 * w] = jnp.where(row == 0, li, pltpu.roll(si, 1, 0))
                    rs = pl.ds(pl.multiple_of(b * nchunks + (tiles - 1 - j) * sub, sub), sub)
                    xr, xi = ee_ref[gi, rs, 0:w], ee_ref[gi, rs, w:2 * w]
                    for step, k in enumerate((1, 2, 4)):
                        xr, xi = cmul_add(xr, xi, a_ref[g, 8 + 2 * step], a_ref[g, 9 + 2 * step],
                                          pltpu.roll(xr, sub - k, 0), pltpu.roll(xi, sub - k, 0))
                    rr, ri = cmul_add(xr, xi, a_ref[g, 14], a_ref[g, 15], fr, fi)
                    pr_ref[gi, rs, 0:w] = jnp.where(row == sub - 1, fr, pltpu.roll(rr, sub - 1, 0))
                    pr_ref[gi, rs, w:2 * w] = jnp.where(row == sub - 1, fi, pltpu.roll(ri, sub - 1, 0))
                    new.append((bcast_row(sr, sub - 1), bcast_row(si, sub - 1), bcast_row(rr, 0), bcast_row(ri, 0)))
            return tuple(new)

        zero = jnp.zeros((sub, w), F32)
        init = tuple((bcast_row(meta[gi][:, tc:tc + w], b), bcast_row(meta[gi][:, tc + w:tc + 2 * w], b), zero, zero)
                     for gi in range(gsub) for b in range(batch))
        final = lax.fori_loop(0, tiles, body, init, unroll=SSM_SCAN_UNROLL)

        for gi, g in enumerate(gs):
            prev = jnp.concatenate([jnp.where(rev_lane, pr_ref[gi, :, k * w:(k + 1) * w], pf_ref[gi, :, k * w:(k + 1) * w])
                                    for k in range(2)], axis=1).astype(BF16)
            y = (yi_ref[gi] + jnp.dot(prev, f_ref[g], preferred_element_type=F32)).astype(BF16)
            pm = []
            for k in (2, 3):
                x = zero
                for b in range(batch):
                    x = jnp.where((row == b) & rev_lane, final[gi * batch + b][k], x)
                pm.append(jnp.concatenate([x, jnp.zeros((SSM_META_ROWS - sub, w), F32)], axis=0))
            prev_m = jnp.concatenate(pm, axis=1).astype(BF16)
            y_m = (meta[gi][:, :tc] + jnp.dot(prev_m, f_ref[g], preferred_element_type=F32)).astype(BF16)
            for part in range(parts):
                ycat_ref[part, :, g * LANES:(g + 1) * LANES] = y[:, part * LANES:(part + 1) * LANES]
                ymcat_ref[part, :, g * LANES:(g + 1) * LANES] = y_m[:, part * LANES:(part + 1) * LANES]

    for part in range(parts):
        o = jnp.dot(ycat_ref[part], perm, preferred_element_type=F32).astype(y_ref.dtype)
        o_m = jnp.dot(ymcat_ref[part], perm, preferred_element_type=F32).astype(ym_ref.dtype)
        for i in range(toks):
            y_ref[part * toks + i] = o[:, i * LANES:(i + 1) * LANES]
            ym_ref[part * toks + i] = o_m[:, i * LANES:(i + 1) * LANES]


def s5_scan(u_main, u_meta_rows, ops, batch, seq):
    perm, me, f, a = ops
    g = me.shape[0]
    t = SSM_CHUNK
    tc = f.shape[2]
    c = tc // t
    width = g * c
    nstate = f.shape[1] // 4
    gtile = LANES // c
    nchunks = seq // t
    rows = batch * nchunks
    assert N_META == t and seq % (t * F32_SUBLANES) == 0 and batch <= F32_SUBLANES
    assert g % gtile == 0 and gtile % SSM_SUB == 0 and u_main.shape == (t, rows, width)

    u_meta = jnp.transpose(u_meta_rows.reshape(batch, t, width), (1, 0, 2))
    u_meta = jnp.pad(u_meta, ((0, 0), (0, SSM_META_ROWS - batch), (0, 0)))
    kernel = functools.partial(_ssm_kernel, batch=batch, nchunks=nchunks, nstate=nstate, gsub=SSM_SUB)
    y, y_m = pl.pallas_call(
        kernel,
        grid=(g // gtile,),
        in_specs=[
            pl.BlockSpec((t, rows, LANES), lambda i: (0, 0, i)),
            pl.BlockSpec((t, SSM_META_ROWS, LANES), lambda i: (0, 0, i)),
            pl.BlockSpec(perm.shape, lambda i: (0, 0)),
            pl.BlockSpec((gtile,) + me.shape[1:], lambda i: (i, 0, 0)),
            pl.BlockSpec((gtile,) + f.shape[1:], lambda i: (i, 0, 0)),
            pl.BlockSpec((gtile,) + a.shape[1:], lambda i: (i, 0, 0, 0)),
        ],
        out_specs=[
            pl.BlockSpec((t, rows, LANES), lambda i: (0, 0, i)),
            pl.BlockSpec((t, SSM_META_ROWS, LANES), lambda i: (0, 0, i)),
        ],
        out_shape=[
            jax.ShapeDtypeStruct((t, rows, width), BF16),
            jax.ShapeDtypeStruct((t, SSM_META_ROWS, width), BF16),
        ],
        scratch_shapes=[
            pltpu.VMEM((SSM_SUB, rows, tc), F32),
            pltpu.VMEM((SSM_SUB, rows, 4 * nstate), F32),
            pltpu.VMEM((SSM_SUB, rows, 4 * nstate), F32),
            pltpu.VMEM((SSM_SUB, rows, 4 * nstate), F32),
            pltpu.VMEM((t // (perm.shape[0] // LANES), rows, gtile * LANES), BF16),
            pltpu.VMEM((t // (perm.shape[0] // LANES), SSM_META_ROWS, gtile * LANES), BF16),
        ],
        compiler_params=_params("parallel"),
        name="s5_scan",
    )(u_main, u_meta, perm, me, f, a)

    y_meta = jnp.transpose(y_m[:, :batch], (1, 0, 2)).reshape(batch * N_META, width)
    return y, y_meta


def _mix_kernel(o_ref, y_ref, gna_ref, gssm_ref, wna_ref, wglu_ref, wssm_ref, out_ref, *stage, chunk_major):
    y_na = jnp.dot(o_ref[...], wna_ref[...], preferred_element_type=F32)
    if chunk_major:
        stage_ref, = stage
        t = y_ref.shape[0]
        ntile = stage_ref.shape[0]
        for k in range(t):
            for lt in range(ntile):
                stage_ref[lt, pl.ds(k, y_ref.shape[1], stride=t), :] = y_ref[
                    k, :, lt * LANES:(lt + 1) * LANES].astype(F32)
        y = jnp.concatenate([stage_ref[lt] for lt in range(ntile)], axis=1)
    else:
        y = y_ref[...].astype(F32)
    gl = _gelu(y)
    z = jnp.dot(gl.astype(BF16), wglu_ref[...], preferred_element_type=F32)
    t = (gl * _sigmoid(z)).astype(BF16)
    y_ssm = jnp.dot(t, wssm_ref[...], preferred_element_type=F32)
    mixed = _sigmoid(gna_ref[...].astype(F32)) * y_na + _sigmoid(gssm_ref[...].astype(F32)) * y_ssm
    out_ref[...] = mixed.astype(out_ref.dtype)


def mix_branches(o_na, y, proj, w_na, w_glu, w_ssm, gate_col):
    rows, na_width = o_na.shape
    ssm_width = y.shape[-1]
    d = w_na.shape[1]
    tm = _pick(rows, MID_ROW_TILES)
    assert gate_col % d == 0
    gblk = gate_col // d
    resident = dict(pipeline_mode=pl.Buffered(1))
    chunk_major = y.ndim == 3
    if chunk_major:
        t = y.shape[0]
        y_spec = pl.BlockSpec((t, tm // t, ssm_width), lambda i: (0, i, 0))
        scratch = [pltpu.VMEM((ssm_width // LANES, tm, LANES), F32)]
    else:
        y_spec = pl.BlockSpec((tm, ssm_width), lambda i: (i, 0))
        scratch = []
    return pl.pallas_call(
        functools.partial(_mix_kernel, chunk_major=chunk_major),
        grid=(rows // tm,),
        scratch_shapes=scratch,
        in_specs=[
            pl.BlockSpec((tm, na_width), lambda i: (i, 0)),
            y_spec,
            pl.BlockSpec((tm, d), lambda i: (i, gblk)),
            pl.BlockSpec((tm, d), lambda i: (i, gblk + 1)),
            pl.BlockSpec(w_na.shape, lambda i: (0, 0), **resident),
            pl.BlockSpec(w_glu.shape, lambda i: (0, 0), **resident),
            pl.BlockSpec(w_ssm.shape, lambda i: (0, 0), **resident),
        ],
        out_specs=pl.BlockSpec((tm, d), lambda i: (i, 0)),
        out_shape=jax.ShapeDtypeStruct((rows, d), BF16),
        compiler_params=_params("parallel"),
        name="mix_branches",
    )(o_na, y, proj, proj, w_na, w_glu, w_ssm)


def _residual_matmul_kernel(h_ref, a_ref, w_ref, g_ref, o_ref, n_ref):
    h1 = h_ref[...] + jnp.dot(a_ref[...], w_ref[...], preferred_element_type=F32)
    o_ref[...] = h1
    n_ref[...] = _rms(h1, g_ref[...]).astype(n_ref.dtype)


def residual_matmul(h, a, w, g):
    rows, d = h.shape
    tm = _pick(rows, MID_ROW_TILES)
    row_spec = pl.BlockSpec((tm, d), lambda i: (i, 0))
    return pl.pallas_call(
        _residual_matmul_kernel,
        grid=(rows // tm,),
        in_specs=[
            row_spec,
            pl.BlockSpec((tm, a.shape[1]), lambda i: (i, 0)),
            pl.BlockSpec(w.shape, lambda i: (0, 0), pipeline_mode=pl.Buffered(1)),
            pl.BlockSpec((1, d), lambda i: (0, 0)),
        ],
        out_specs=[row_spec, row_spec],
        out_shape=[jax.ShapeDtypeStruct((rows, d), F32), jax.ShapeDtypeStruct((rows, d), BF16)],
        compiler_params=_params("parallel"),
        name="residual_out_proj",
    )(h, a, w, g.reshape(1, d))


HALO = BF16_SUBLANES


def _ffn_up_kernel(h_ref, prev_ref, next_ref, wa_ref, wg_ref, cw_ref, cb_ref, o_ref, hn_ref, *, tm):
    @pl.when(pl.program_id(1) == 0)
    def _():
        hn_ref[0:HALO, :] = prev_ref[...]
        hn_ref[HALO:HALO + tm, :] = h_ref[...]
        hn_ref[HALO + tm:, :] = next_ref[...]

    ext = tm + 2 * HALO
    a = jnp.dot(hn_ref[...], wa_ref[...].astype(BF16), preferred_element_type=F32)
    gate = jnp.dot(hn_ref[HALO:HALO + tm, :], wg_ref[...].astype(BF16), preferred_element_type=F32)
    a_prev = pltpu.roll(a, 1, 0)[HALO:HALO + tm]
    a_next = pltpu.roll(a, ext - 1, 0)[HALO:HALO + tm]
    conv = a_prev * cw_ref[0:1, :] + a[HALO:HALO + tm] * cw_ref[1:2, :] + a_next * cw_ref[2:3, :] + cb_ref[...]
    o_ref[...] = (_gelu(conv) * gate).astype(o_ref.dtype)


def ffn_up(h, halo_prev, halo_next, w_up, conv_w, conv_b, tm):
    rows, d = h.shape
    dff = conv_b.shape[0]
    tn = _pick(dff, COL_TILES)
    nj = dff // tn
    kernel = functools.partial(_ffn_up_kernel, tm=tm)
    return pl.pallas_call(
        kernel,
        grid=(rows // tm, nj),
        in_specs=[
            pl.BlockSpec((tm, d), lambda i, j: (i, 0)),
            pl.BlockSpec((HALO, d), lambda i, j: (i, 0)),
            pl.BlockSpec((HALO, d), lambda i, j: (i, 0)),
            pl.BlockSpec((d, tn), lambda i, j: (0, j)),
            pl.BlockSpec((d, tn), lambda i, j: (0, nj + j)),
            pl.BlockSpec((conv_w.shape[0], tn), lambda i, j: (0, j)),
            pl.BlockSpec((1, tn), lambda i, j: (0, j)),
        ],
        out_specs=pl.BlockSpec((tm, tn), lambda i, j: (i, j)),
        out_shape=jax.ShapeDtypeStruct((rows, dff), BF16),
        scratch_shapes=[pltpu.VMEM((tm + 2 * HALO, d), BF16)],
        compiler_params=_params("parallel", "arbitrary"),
        name="ffn_up_conv_gate",
    )(h, halo_prev, halo_next, w_up, w_up, conv_w, conv_b.reshape(1, dff))


def _ffn_down_kernel(h_ref, a_ref, w_ref, g_ref, o_ref):
    k = pl.program_id(1)

    @pl.when(k == 0)
    def _():
        o_ref[...] = h_ref[...]

    o_ref[...] += jnp.dot(a_ref[...], w_ref[...].astype(BF16), preferred_element_type=F32)

    @pl.when(k == pl.num_programs(1) - 1)
    def _():
        o_ref[...] = _rms(o_ref[...], g_ref[...])


def ffn_down(h, act, w_down, g):
    rows, d = h.shape
    dff = act.shape[1]
    tm = _pick(rows, ROW_TILES)
    tk = _pick(dff, COL_TILES)
    return pl.pallas_call(
        _ffn_down_kernel,
        grid=(rows // tm, dff // tk),
        in_specs=[
            pl.BlockSpec((tm, d), lambda i, k: (i, 0)),
            pl.BlockSpec((tm, tk), lambda i, k: (i, k)),
            pl.BlockSpec((tk, d), lambda i, k: (k, 0)),
            pl.BlockSpec((1, d), lambda i, k: (0, 0)),
        ],
        out_specs=pl.BlockSpec((tm, d), lambda i, k: (i, 0)),
        out_shape=jax.ShapeDtypeStruct((rows, d), F32),
        compiler_params=_params("parallel", "arbitrary"),
        name="ffn_down_final_norm",
    )(h, act, w_down, g.reshape(1, d))


def _conv_halos(h1_main, h1_meta, batch, seq, tm):
    d = h1_main.shape[1]
    per_seq = seq // tm
    hm = h1_main.reshape(batch, per_seq, tm, d)
    meta_tail = h1_meta.reshape(batch, 1, N_META, d)[:, :, N_META - HALO:]
    prev = jnp.concatenate([meta_tail, hm[:, :-1, tm - HALO:]], axis=1)
    nxt = jnp.concatenate([hm[:, 1:, :HALO], jnp.zeros((batch, 1, HALO, d), h1_main.dtype)], axis=1)
    return prev.reshape(batch * per_seq * HALO, d), nxt.reshape(batch * per_seq * HALO, d)


def kernel(x, meta_tokens, norm1_g, w_in, na_rpb, ssm_lam_re, ssm_lam_im, ssm_log_step, ssm_b_re, ssm_b_im,
           ssm_c_re, ssm_c_im, ssm_d, w_glu, w_proj_na, w_proj_ssm, w_out, norm2_g, w_up, conv_w, conv_b,
           w_down, final_g):
    batch, seq, d = x.shape
    depth = w_in.shape[0]
    na_width = w_proj_na.shape[1]
    ssm_width = w_proj_ssm.shape[1]
    assert depth == 1 and N_META >= HALO
    l = 0
    h_main = x.reshape(batch * seq, d)
    h_meta = jnp.broadcast_to(meta_tokens.astype(x.dtype)[None], (batch, N_META, d)).reshape(batch * N_META, d)

    u_col = 3 * na_width
    proj_main, u_main = norm_matmul(h_main, norm1_g[l], w_in[l], chunk_major=(u_col, ssm_width))
    proj_meta = norm_matmul(h_meta, norm1_g[l], w_in[l])

    o_main, o_meta = neighbourhood_attention(proj_main, proj_meta, na_rpb[l], batch, seq, na_width)

    ops = _ssm_operators(ssm_lam_re[l], ssm_lam_im[l], ssm_log_step[l], ssm_b_re[l], ssm_b_im[l],
                         ssm_c_re[l], ssm_c_im[l], ssm_d[l])
    y_main, y_meta = s5_scan(u_main, proj_meta[:, u_col:u_col + ssm_width], ops, batch, seq)

    w_na_b, w_glu_b, w_ssm_b = w_proj_na[l].astype(BF16), w_glu[l].astype(BF16), w_proj_ssm[l].astype(BF16)
    w_out_b = w_out[l].astype(BF16)
    gate_col = u_col + ssm_width
    h1 = []
    for h, o, y, proj in ((h_main, o_main, y_main, proj_main), (h_meta, o_meta, y_meta, proj_meta)):
        mixed = mix_branches(o, y, proj, w_na_b, w_glu_b, w_ssm_b, gate_col)
        h1.append(residual_matmul(h, mixed, w_out_b, norm2_g[l]))
    (h1_main, hn_main), (_, hn_meta) = h1

    tm = _pick(seq, ROW_TILES)
    halo_prev, halo_next = _conv_halos(hn_main, hn_meta, batch, seq, tm)
    act = ffn_up(hn_main, halo_prev, halo_next, w_up[l], conv_w[l], conv_b[l], tm)
    out = ffn_down(h1_main, act, w_down[l], final_g)
    return out.reshape(batch, seq, d)
```

```python
import functools
import math

import jax
import jax.numpy as jnp
from jax import lax
from jax.experimental import pallas as pl
from jax.experimental.pallas import tpu as pltpu

F32 = jnp.float32
BF16 = jnp.bfloat16

N_META = 16
GRID_W = 64
ROW_WIN = 8
COL_WIN = 16
RMS_EPS = 1e-6
SSM_CHUNK = 16
MASK_BIAS = -1e30
NA_QROWS = 4
NA_KROWS = 12
NA_UNROLL = 8

VMEM_LIMIT_BYTES = 56 * 1024 * 1024
F32_SUBLANES = 8
BF16_SUBLANES = 16
LANES = 128

SSM_SUB = 2
SSM_SCAN_UNROLL = True
SSM_META_ROWS = BF16_SUBLANES

ROW_TILES = (1024, 512, 256, 128, 64)
MID_ROW_TILES = (512, 256, 128, 64)
COL_TILES = (512, 256, 128)
WIDE_COL_TILES = (1024, 512, 256, 128)


def _pick(dim, prefs):
    for t in prefs:
        if dim % t == 0:
            return t
    return dim


def _params(*sem):
    return pltpu.CompilerParams(dimension_semantics=sem, vmem_limit_bytes=VMEM_LIMIT_BYTES)


def _rms(x, g):
    ms = jnp.mean(x * x, axis=-1, keepdims=True)
    return x * lax.rsqrt(ms + RMS_EPS) * g


def _gelu(x):
    c = math.sqrt(2.0 / math.pi)
    return 0.5 * x * (1.0 + jnp.tanh(c * (x + 0.044715 * (x * x * x))))


def _sigmoid(x):
    return 0.5 * jnp.tanh(0.5 * x) + 0.5


def _norm_matmul_kernel(x_ref, g_ref, w_ref, o_ref, *rest, chunk_cols):
    hn_ref = rest[-1] if chunk_cols is None else rest[1]

    @pl.when(pl.program_id(1) == 0)
    def _():
        hn_ref[...] = _rms(x_ref[...], g_ref[...]).astype(BF16)

    r = jnp.dot(hn_ref[...], w_ref[...], preferred_element_type=F32)
    o_ref[...] = r.astype(o_ref.dtype)

    if chunk_cols is not None:
        u3_ref, _, stage_ref = rest
        tile, off, width = chunk_cols
        t = SSM_CHUNK

        @pl.when(pl.program_id(1) == tile)
        def _():
            for lt in range(width // LANES):
                stage_ref[lt] = r[:, off + lt * LANES:off + (lt + 1) * LANES]
            for k in range(t):
                for lt in range(width // LANES):
                    u3_ref[k, :, lt * LANES:(lt + 1) * LANES] = stage_ref[
                        lt, pl.ds(k, stage_ref.shape[1] // t, stride=t), :].astype(u3_ref.dtype)


def norm_matmul(x, g, w, chunk_major=None):
    rows, d = x.shape
    n = w.shape[1]
    tm = _pick(rows, ROW_TILES)
    tn = _pick(n, WIDE_COL_TILES)
    out_specs = [pl.BlockSpec((tm, tn), lambda i, j: (i, j))]
    out_shape = [jax.ShapeDtypeStruct((rows, n), BF16)]
    scratch = [pltpu.VMEM((tm, d), BF16)]
    chunk_cols = None
    if chunk_major is not None:
        col, width = chunk_major
        t = SSM_CHUNK
        chunk_cols = (col // tn, col % tn, width)
        assert col % tn + width <= tn and tm % (t * F32_SUBLANES) == 0
        out_specs.append(pl.BlockSpec((t, tm // t, width), lambda i, j: (0, i, 0)))
        out_shape.append(jax.ShapeDtypeStruct((t, rows // t, width), BF16))
        scratch.append(pltpu.VMEM((width // LANES, tm, LANES), F32))
    out = pl.pallas_call(
        functools.partial(_norm_matmul_kernel, chunk_cols=chunk_cols),
        grid=(rows // tm, n // tn),
        in_specs=[
            pl.BlockSpec((tm, d), lambda i, j: (i, 0)),
            pl.BlockSpec((1, d), lambda i, j: (0, 0)),
            pl.BlockSpec((d, tn), lambda i, j: (0, j)),
        ],
        out_specs=out_specs,
        out_shape=out_shape,
        scratch_shapes=scratch,
        compiler_params=_params("parallel", "arbitrary"),
        name="norm_in_proj",
    )(x, g.reshape(1, d), w)
    return out[0] if chunk_major is None else out


def _na_kernel(q_ref, k_ref, v_ref, qm_ref, km_ref, vm_ref, tab_ref, o_ref, om_ref, bias_ref, sm_ref, acc_ref,
               *, rows, dh):
    lane = lax.broadcasted_iota(jnp.int32, (1, 2 * dh), 1)
    head_masks = (lane < dh, lane >= dh)
    scale = dh ** -0.5
    km = km_ref[...]
    vm = vm_ref[...]
    nt = (((1,), (1,)), ((), ()))

    masked = 2 * ROW_WIN - 1
    nblk = rows // NA_QROWS
    qblk = NA_QROWS * GRID_W
    kblk = NA_KROWS * GRID_W

    def rel_row(kind, j, i):
        if kind == 0:
            return i - j + (ROW_WIN - 1) if i < ROW_WIN else masked
        if kind == 1:
            return i - j + (ROW_WIN - 1 - ROW_WIN // 2) if j <= i < j + ROW_WIN else masked
        lo = NA_KROWS - ROW_WIN
        return i - j + (NA_QROWS - NA_KROWS) + (ROW_WIN - 1) if i >= lo else masked

    first_row = lane < GRID_W
    for kind in range(3):
        for hh in range(2):
            for j in range(NA_QROWS):
                r_lo = (hh * NA_QROWS + j) * GRID_W
                for i2 in range(NA_KROWS // 2):
                    bias_ref[kind, r_lo:r_lo + GRID_W, i2 * 2 * GRID_W:(i2 + 1) * 2 * GRID_W] = jnp.where(
                        first_row, tab_ref[hh, rel_row(kind, j, 2 * i2)], tab_ref[hh, rel_row(kind, j, 2 * i2 + 1)])

    def one_head(x, hh):
        return jnp.where(head_masks[hh], x, jnp.zeros_like(x))

    qm = (qm_ref[...] * scale).astype(BF16)
    om = None
    for hh in range(2):
        s = lax.dot_general(one_head(qm, hh), km, nt, preferred_element_type=F32)
        p = jnp.exp(s - jnp.max(s, axis=-1, keepdims=True))
        o_h = jnp.dot(p.astype(BF16), vm, preferred_element_type=F32) / jnp.sum(p, axis=-1, keepdims=True)
        om = o_h if om is None else jnp.where(head_masks[1], o_h, om)
    om_ref[...] = om.astype(om_ref.dtype)

    q_all = (q_ref[...] * scale).astype(BF16)
    for hh in range(2):
        sm_ref[hh] = lax.dot_general(one_head(q_all, hh), km, nt, preferred_element_type=F32)

    def body(blk, carry):
        r = blk * NA_QROWS
        k0 = jnp.clip(r - ROW_WIN // 2, 0, rows - NA_KROWS)
        kind = jnp.where(blk == 0, 0, jnp.where(blk == nblk - 1, 2, 1))
        qrows = pl.ds(pl.multiple_of(r * GRID_W, qblk), qblk)
        krows = pl.ds(pl.multiple_of(k0 * GRID_W, GRID_W), kblk)
        qs = (q_ref[qrows, :] * scale).astype(BF16)
        q2 = jnp.concatenate([one_head(qs, 0), one_head(qs, 1)], axis=0)
        s = lax.dot_general(q2, k_ref[krows, :], nt, preferred_element_type=F32) + bias_ref[kind]
        s_m = jnp.concatenate([sm_ref[0, qrows, :], sm_ref[1, qrows, :]], axis=0)
        m = jnp.maximum(jnp.max(s, axis=-1, keepdims=True), jnp.max(s_m, axis=-1, keepdims=True))
        p = jnp.exp(s - m)
        p_m = jnp.exp(s_m - m)
        inv = 1.0 / (jnp.sum(p, axis=-1, keepdims=True) + jnp.sum(p_m, axis=-1, keepdims=True))
        acc = jnp.dot(p.astype(BF16), v_ref[krows, :], preferred_element_type=F32) * inv
        acc_ref[qrows, :] = jnp.where(head_masks[1], acc[qblk:], acc[:qblk])
        p_m = p_m * inv
        sm_ref[0, qrows, :] = p_m[:qblk]
        sm_ref[1, qrows, :] = p_m[qblk:]
        return carry

    lax.fori_loop(0, nblk, body, 0, unroll=NA_UNROLL)

    o_meta = [jnp.dot(sm_ref[hh].astype(BF16), vm, preferred_element_type=F32) for hh in range(2)]
    o_ref[...] = (acc_ref[...] + jnp.where(head_masks[1], o_meta[1], o_meta[0])).astype(o_ref.dtype)


def _na_bias_table(rpb):
    c = jnp.arange(GRID_W)[:, None]
    kc = jnp.arange(2 * GRID_W)[None, :] % GRID_W
    col_start = jnp.clip(c - COL_WIN // 2, 0, GRID_W - COL_WIN)
    valid = (kc >= col_start) & (kc < col_start + COL_WIN)
    dc = kc - c + (COL_WIN - 1)
    pick = (dc.reshape(-1)[None] == jnp.arange(2 * COL_WIN - 1)[:, None]).astype(F32)
    tab = jnp.dot(rpb.astype(F32).reshape(-1, 2 * COL_WIN - 1), pick, precision=lax.Precision.HIGHEST)
    tab = tab.reshape(rpb.shape[0], 2 * ROW_WIN - 1, GRID_W, 2 * GRID_W)
    tab = jnp.where(valid, tab, MASK_BIAS)
    return jnp.concatenate([tab, jnp.full_like(tab[:, :1], MASK_BIAS)], axis=1)


def neighbourhood_attention(proj_main, proj_meta, rpb, batch, seq, na_width):
    heads = rpb.shape[0]
    dh = na_width // heads
    rows = seq // GRID_W
    assert rows >= NA_KROWS and rows % NA_QROWS == 0 and heads % 2 == 0 and 2 * dh == LANES
    assert NA_KROWS % 2 == 0 and NA_KROWS >= ROW_WIN + NA_QROWS - 1 and NA_QROWS <= ROW_WIN // 2
    pairs = heads // 2
    bias = _na_bias_table(rpb)
    blk = (seq, 2 * dh)
    mblk = (N_META, 2 * dh)
    kernel = functools.partial(_na_kernel, rows=rows, dh=dh)
    return pl.pallas_call(
        kernel,
        grid=(batch, pairs),
        in_specs=[
            pl.BlockSpec(blk, lambda b, h: (b, h)),
            pl.BlockSpec(blk, lambda b, h: (b, pairs + h)),
            pl.BlockSpec(blk, lambda b, h: (b, 2 * pairs + h)),
            pl.BlockSpec(mblk, lambda b, h: (b, h)),
            pl.BlockSpec(mblk, lambda b, h: (b, pairs + h)),
            pl.BlockSpec(mblk, lambda b, h: (b, 2 * pairs + h)),
            pl.BlockSpec((2, 2 * ROW_WIN, GRID_W, 2 * GRID_W), lambda b, h: (h, 0, 0, 0)),
        ],
        out_specs=[
            pl.BlockSpec(blk, lambda b, h: (b, h)),
            pl.BlockSpec(mblk, lambda b, h: (b, h)),
        ],
        out_shape=[
            jax.ShapeDtypeStruct((batch * seq, na_width), BF16),
            jax.ShapeDtypeStruct((batch * N_META, na_width), BF16),
        ],
        scratch_shapes=[pltpu.VMEM((3, 2 * NA_QROWS * GRID_W, NA_KROWS * GRID_W), F32),
                        pltpu.VMEM((2, seq, N_META), F32),
                        pltpu.VMEM((seq, 2 * dh), F32)],
        compiler_params=_params("parallel", "parallel"),
        name="neighbourhood_attention",
    )(proj_main, proj_main, proj_main, proj_meta, proj_meta, proj_meta, bias)


def _chunk_ops_kernel(cp_ref, d_ref, shift_ref, pw_ref, bt_ref, me_ref, f_ref):
    gt, _, n, packed = cp_ref.shape
    c = bt_ref.shape[2]
    t, _, tc = shift_ref.shape
    w = pw_ref.shape[3]
    q = lax.broadcasted_iota(jnp.int32, (packed, tc), 1)
    p_lane = lax.broadcasted_iota(jnp.int32, (packed, tc), 0)
    k_row = lax.broadcasted_iota(jnp.int32, (c, tc), 0)

    def spread(parts, k, by_position):
        mat = (p_lane - k * c == (q // c if by_position else q % c)).astype(BF16)
        return sum(jnp.dot(x, mat, preferred_element_type=F32) for x in parts)

    lag_rows = [[None, None] for _ in range(gt)]
    for d in range(2):
        x = cp_ref[:, d].reshape(gt * n, packed)
        hi = x.astype(BF16)
        parts = (hi, (x - hi.astype(F32)).astype(BF16))
        cx_r, cx_i = spread(parts, 0, False), spread(parts, 1, False)
        e_r, e_i = spread(parts, 4, True), spread(parts, 5, True)
        f_ref[:, d * n:(d + 1) * n, :] = (cx_r * e_r - cx_i * e_i).reshape(gt, n, tc).astype(f_ref.dtype)
        f_ref[:, (2 + d) * n:(3 + d) * n, :] = (-(cx_r * e_i + cx_i * e_r)).reshape(gt, n, tc).astype(f_ref.dtype)
        e_r, e_i = spread(parts, 2, True), spread(parts, 3, True)
        z_r, z_i = cx_r * e_r - cx_i * e_i, cx_r * e_i + cx_i * e_r
        for g in range(gt):
            zr_g, zi_g = z_r[g * n:(g + 1) * n], z_i[g * n:(g + 1) * n]
            x_g = cp_ref[g, d]
            rows = jnp.zeros((c, tc), F32)
            for ci in range(c):
                b_r, b_i = x_g[:, 6 * c + ci:6 * c + ci + 1], x_g[:, 7 * c + ci:7 * c + ci + 1]
                row = jnp.sum(b_r * zr_g - b_i * zi_g, axis=0, keepdims=True)
                rows = jnp.where(k_row == ci, row, rows)
            lag_rows[g][d] = rows
    lag = jnp.concatenate([jnp.concatenate(lag_rows[g], axis=1) for g in range(gt)], axis=0).astype(BF16)

    lane = lax.broadcasted_iota(jnp.int32, (gt * c, tc), 1)
    c_in = lax.broadcasted_iota(jnp.int32, (gt * c, tc), 0) % c
    skip_gain = d_ref[...].reshape(gt * c, tc)
    b_r, b_i = bt_ref[:, 0], bt_ref[:, 1]
    for s in range(t):
        rows = slice(s * c, (s + 1) * c)
        blk = jnp.dot(lag, shift_ref[s], preferred_element_type=F32)
        blk = blk + jnp.where(lane == s * c + c_in, skip_gain, 0.0)
        me_ref[:, rows, 0:tc] = blk.reshape(gt, c, tc).astype(me_ref.dtype)
        e_r, e_i = pw_ref[:, 0, s:s + 1, :], pw_ref[:, 1, s:s + 1, :]
        me_ref[:, rows, tc:tc + w] = (e_r * b_r - e_i * b_i).astype(me_ref.dtype)
        me_ref[:, rows, tc + w:tc + 2 * w] = (e_r * b_i + e_i * b_r).astype(me_ref.dtype)


def _chunk_operators(cp, d_rows, pw, bt):
    g, _, n, packed = cp.shape
    t, c = pw.shape[2], bt.shape[2]
    tc = t * c
    w = pw.shape[-1]
    assert t == c and packed == 8 * c
    gt = _pick(g, (8, 4, 2, 1))
    p = jnp.arange(2 * tc)[None, :, None]
    q = jnp.arange(tc)[None, None, :]
    s = jnp.arange(t)[:, None, None]
    shift = jnp.where(p < tc, p == q - s * c, p - tc == q + (t - 1 - s) * c).astype(BF16)

    def per_group(shape):
        return pl.BlockSpec((gt,) + tuple(shape), lambda i: (i,) + (0,) * len(shape))

    return pl.pallas_call(
        _chunk_ops_kernel,
        grid=(g // gt,),
        in_specs=[
            per_group(cp.shape[1:]), per_group(d_rows.shape[1:]),
            pl.BlockSpec(shift.shape, lambda i: (0, 0, 0)),
            per_group(pw.shape[1:]), per_group(bt.shape[1:]),
        ],
        out_specs=[per_group((tc, tc + 2 * w)), per_group((4 * n, tc))],
        out_shape=[jax.ShapeDtypeStruct((g, tc, tc + 2 * w), BF16), jax.ShapeDtypeStruct((g, 4 * n, tc), BF16)],
        compiler_params=_params("parallel"),
        name="s5_chunk_operators",
    )(cp, d_rows, shift, pw, bt)


def _ssm_operators(lam_re, lam_im, log_step, b_re, b_im, c_re, c_im, d_skip):
    t = SSM_CHUNK
    lr, li = lam_re.astype(F32), lam_im.astype(F32)
    step = jnp.exp(log_step.astype(F32))[..., None]
    dt_r, dt_i = lr * step, li * step
    g, n = lr.shape[1], lr.shape[2]
    c = b_re.shape[-1]

    def lam_bar_pow(k):
        mag = jnp.exp(dt_r[:, :, None] * k)
        return mag * jnp.cos(dt_i[:, :, None] * k), mag * jnp.sin(dt_i[:, :, None] * k)

    p_r, p_i = lam_bar_pow(jnp.arange(t + 1, dtype=F32)[:, None])
    x_r, x_i = p_r[:, :, 1] - 1.0, p_i[:, :, 1]
    den = lr * lr + li * li
    q_r, q_i = (x_r * lr + x_i * li) / den, (x_i * lr - x_r * li) / den
    bt_r, bt_i = jnp.swapaxes(b_re.astype(F32), 2, 3), jnp.swapaxes(b_im.astype(F32), 2, 3)
    bb_r = q_r[:, :, None] * bt_r - q_i[:, :, None] * bt_i
    bb_i = q_r[:, :, None] * bt_i + q_i[:, :, None] * bt_r
    bn_r = q_r[..., None] * b_re.astype(F32) - q_i[..., None] * b_im.astype(F32)
    bn_i = q_r[..., None] * b_im.astype(F32) + q_i[..., None] * b_re.astype(F32)
    ct_r, ct_i = jnp.swapaxes(c_re.astype(F32), 2, 3), jnp.swapaxes(c_im.astype(F32), 2, 3)

    def cols(x):
        return jnp.swapaxes(x, 1, 2)

    factors = []
    for d, lag_sel, out_sel in ((0, lambda x: x[:, :t], lambda x: x[:, 1:]),
                                (1, lambda x: x[:, :t][:, ::-1], lambda x: x[:, ::-1][:, :t])):
        factors.append(jnp.concatenate([ct_r[d], ct_i[d], cols(lag_sel(p_r[d])), cols(lag_sel(p_i[d])),
                                        cols(out_sel(p_r[d])), cols(out_sel(p_i[d])), bn_r[d], bn_i[d]], axis=-1))
    cp = jnp.stack(factors, axis=1)
    d_rows = jnp.broadcast_to(d_skip.astype(F32).reshape(g, c, 1), (g, c, t * c))

    pw = jnp.stack([jnp.concatenate([p_r[0][:, ::-1][:, 1:], p_r[1][:, :t]], axis=-1),
                    jnp.concatenate([p_i[0][:, ::-1][:, 1:], p_i[1][:, :t]], axis=-1)], axis=1)
    bt = jnp.stack([jnp.concatenate([bb_r[0], bb_r[1]], axis=-1),
                    jnp.concatenate([bb_i[0], bb_i[1]], axis=-1)], axis=1)
    me, f = _chunk_operators(cp, d_rows, pw, bt)

    sub = F32_SUBLANES
    row = jnp.arange(sub)

    d_r, d_i = lam_bar_pow((t * (row + 1)).astype(F32)[:, None])
    d_r = jnp.concatenate([d_r[0], d_r[1]], axis=-1)
    d_i = jnp.concatenate([d_i[0], d_i[1]], axis=-1)

    def fixed(x, k, keep):
        return jnp.where(keep[None, :, None], x[:, k - 1:k], 0.0)

    slots = []
    for k in (1, 2, 4):
        slots += [fixed(d_r, k, row >= k), fixed(d_i, k, row >= k)]
    slots += [d_r, d_i]
    for k in (1, 2, 4):
        slots += [fixed(d_r, k, row < sub - k), fixed(d_i, k, row < sub - k)]
    slots += [d_r[:, ::-1], d_i[:, ::-1]]
    a = jnp.concatenate(slots, axis=1).reshape(g, len(slots), sub, 2 * n)

    toks = LANES // c
    grp = LANES // c
    src = jnp.transpose(jnp.arange(toks * grp * c).reshape(toks, grp, c), (1, 0, 2)).reshape(-1)
    perm = (jnp.arange(toks * grp * c)[:, None] == src[None, :]).astype(BF16)
    return perm, me.astype(BF16), f.astype(BF16), a.astype(F32)


def _ssm_kernel(*refs, batch, nchunks, nstate, gsub):
    t = SSM_CHUNK
    (u_ref, um_ref, p_ref, me_ref, f_ref, a_ref, y_ref, ym_ref,
     yi_ref, ee_ref, pf_ref, pr_ref, ycat_ref, ymcat_ref) = refs
    gtile = me_ref.shape[0]
    tc = f_ref.shape[2]
    toks = p_ref.shape[0] // LANES
    parts = t // toks
    sub = F32_SUBLANES
    w = 2 * nstate
    tiles = nchunks // sub
    perm = p_ref[...]
    row = lax.broadcasted_iota(jnp.int32, (sub, w), 0)
    lane = lax.broadcasted_iota(jnp.int32, (1, w), 1)
    rev_lane = lane >= nstate

    def group_major(token_refs):
        out = []
        for part in range(parts):
            x = jnp.concatenate([token_refs[part * toks + i] for i in range(toks)], axis=1)
            out.append(jnp.dot(x, perm, preferred_element_type=F32).astype(BF16))
        return out

    def group_chunk(regrouped, g):
        return jnp.concatenate([x[:, g * LANES:(g + 1) * LANES] for x in regrouped], axis=1)

    v_main = group_major(u_ref)
    v_meta = group_major(um_ref)

    def cmul_add(xr, xi, ar, ai, yr, yi):
        return xr + ar * yr - ai * yi, xi + ar * yi + ai * yr

    def bcast_row(x, r):
        return jnp.broadcast_to(x[r:r + 1, :], (sub, w))

    for sb in range(gtile // gsub):
        gs = [sb * gsub + gi for gi in range(gsub)]
        meta = []
        for gi, g in enumerate(gs):
            me = jnp.dot(group_chunk(v_main, g), me_ref[g], preferred_element_type=F32)
            yi_ref[gi] = me[:, :tc]
            ee_ref[gi] = me[:, tc:]
            meta.append(jnp.dot(group_chunk(v_meta, g), me_ref[g], preferred_element_type=F32))

        def body(j, carry, gs=gs):
            new = []
            for gi, g in enumerate(gs):
                for b in range(batch):
                    lr, li, fr, fi = carry[gi * batch + b]
                    fs = pl.ds(pl.multiple_of(b * nchunks + j * sub, sub), sub)
                    xr, xi = ee_ref[gi, fs, 0:w], ee_ref[gi, fs, w:2 * w]
                    for step, k in enumerate((1, 2, 4)):
                        xr, xi = cmul_add(xr, xi, a_ref[g, 2 * step], a_ref[g, 2 * step + 1],
                                          pltpu.roll(xr, k, 0), pltpu.roll(xi, k, 0))
                    sr, si = cmul_add(xr, xi, a_ref[g, 6], a_ref[g, 7], lr, li)
                    pf_ref[gi, fs, 0:w] = jnp.where(row == 0, lr, pltpu.roll(sr, 1, 0))
                    pf_ref[gi, fs, w:2 * w] = jnp.where(row == 0, li, pltpu.roll(si, 1, 0))
                    rs = pl.ds(pl.multiple_of(b * nchunks + (tiles - 1 - j) * sub, sub), sub)
                    xr, xi = ee_ref[gi, rs, 0:w], ee_ref[gi, rs, w:2 * w]
                    for step, k in enumerate((1, 2, 4)):
                        xr, xi = cmul_add(xr, xi, a_ref[g, 8 + 2 * step], a_ref[g, 9 + 2 * step],
                                          pltpu.roll(xr, sub - k, 0), pltpu.roll(xi, sub - k, 0))
                    rr, ri = cmul_add(xr, xi, a_ref[g, 14], a_ref[g, 15], fr, fi)
                    pr_ref[gi, rs, 0:w] = jnp.where(row == sub - 1, fr, pltpu.roll(rr, sub - 1, 0))
                    pr_ref[gi, rs, w:2 * w] = jnp.where(row == sub - 1, fi, pltpu.roll(ri, sub - 1, 0))
                    new.append((bcast_row(sr, sub - 1), bcast_row(si, sub - 1), bcast_row(rr, 0), bcast_row(ri, 0)))
            return tuple(new)

        zero = jnp.zeros((sub, w), F32)
        init = tuple((bcast_row(meta[gi][:, tc:tc + w], b), bcast_row(meta[gi][:, tc + w:tc + 2 * w], b), zero, zero)
                     for gi in range(gsub) for b in range(batch))
        final = lax.fori_loop(0, tiles, body, init, unroll=SSM_SCAN_UNROLL)

        for gi, g in enumerate(gs):
            prev = jnp.concatenate([jnp.where(rev_lane, pr_ref[gi, :, k * w:(k + 1) * w], pf_ref[gi, :, k * w:(k + 1) * w])
                                    for k in range(2)], axis=1).astype(BF16)
            y = (yi_ref[gi] + jnp.dot(prev, f_ref[g], preferred_element_type=F32)).astype(BF16)
            pm = []
            for k in (2, 3):
                x = zero
                for b in range(batch):
                    x = jnp.where((row == b) & rev_lane, final[gi * batch + b][k], x)
                pm.append(jnp.concatenate([x, jnp.zeros((SSM_META_ROWS - sub, w), F32)], axis=0))
            prev_m = jnp.concatenate(pm, axis=1).astype(BF16)
            y_m = (meta[gi][:, :tc] + jnp.dot(prev_m, f_ref[g], preferred_element_type=F32)).astype(BF16)
            for part in range(parts):
                ycat_ref[part, :, g * LANES:(g + 1) * LANES] = y[:, part * LANES:(part + 1) * LANES]
                ymcat_ref[part, :, g * LANES:(g + 1) * LANES] = y_m[:, part * LANES:(part + 1) * LANES]

    for part in range(parts):
        o = jnp.dot(ycat_ref[part], perm, preferred_element_type=F32).astype(y_ref.dtype)
        o_m = jnp.dot(ymcat_ref[part], perm, preferred_element_type=F32).astype(ym_ref.dtype)
        for i in range(toks):
            y_ref[part * toks + i] = o[:, i * LANES:(i + 1) * LANES]
            ym_ref[part * toks + i] = o_m[:, i * LANES:(i + 1) * LANES]


def s5_scan(u_main, u_meta_rows, ops, batch, seq):
    perm, me, f, a = ops
    g = me.shape[0]
    t = SSM_CHUNK
    tc = f.shape[2]
    c = tc // t
    width = g * c
    nstate = f.shape[1] // 4
    gtile = LANES // c
    nchunks = seq // t
    rows = batch * nchunks
    assert N_META == t and seq % (t * F32_SUBLANES) == 0 and batch <= F32_SUBLANES
    assert g % gtile == 0 and gtile % SSM_SUB == 0 and u_main.shape == (t, rows, width)

    u_meta = jnp.transpose(u_meta_rows.reshape(batch, t, width), (1, 0, 2))
    u_meta = jnp.pad(u_meta, ((0, 0), (0, SSM_META_ROWS - batch), (0, 0)))
    kernel = functools.partial(_ssm_kernel, batch=batch, nchunks=nchunks, nstate=nstate, gsub=SSM_SUB)
    y, y_m = pl.pallas_call(
        kernel,
        grid=(g // gtile,),
        in_specs=[
            pl.BlockSpec((t, rows, LANES), lambda i: (0, 0, i)),
            pl.BlockSpec((t, SSM_META_ROWS, LANES), lambda i: (0, 0, i)),
            pl.BlockSpec(perm.shape, lambda i: (0, 0)),
            pl.BlockSpec((gtile,) + me.shape[1:], lambda i: (i, 0, 0)),
            pl.BlockSpec((gtile,) + f.shape[1:], lambda i: (i, 0, 0)),
            pl.BlockSpec((gtile,) + a.shape[1:], lambda i: (i, 0, 0, 0)),
        ],
        out_specs=[
            pl.BlockSpec((t, rows, LANES), lambda i: (0, 0, i)),
            pl.BlockSpec((t, SSM_META_ROWS, LANES), lambda i: (0, 0, i)),
        ],
        out_shape=[
            jax.ShapeDtypeStruct((t, rows, width), BF16),
            jax.ShapeDtypeStruct((t, SSM_META_ROWS, width), BF16),
        ],
        scratch_shapes=[
            pltpu.VMEM((SSM_SUB, rows, tc), F32),
            pltpu.VMEM((SSM_SUB, rows, 4 * nstate), F32),
            pltpu.VMEM((SSM_SUB, rows, 4 * nstate), F32),
            pltpu.VMEM((SSM_SUB, rows, 4 * nstate), F32),
            pltpu.VMEM((t // (perm.shape[0] // LANES), rows, gtile * LANES), BF16),
            pltpu.VMEM((t // (perm.shape[0] // LANES), SSM_META_ROWS, gtile * LANES), BF16),
        ],
        compiler_params=_params("parallel"),
        name="s5_scan",
    )(u_main, u_meta, perm, me, f, a)

    y_meta = jnp.transpose(y_m[:, :batch], (1, 0, 2)).reshape(batch * N_META, width)
    return y, y_meta


def _mix_kernel(o_ref, y_ref, gna_ref, gssm_ref, wna_ref, wglu_ref, wssm_ref, out_ref, *stage, chunk_major):
    y_na = jnp.dot(o_ref[...], wna_ref[...], preferred_element_type=F32)
    if chunk_major:
        stage_ref, = stage
        t = y_ref.shape[0]
        ntile = stage_ref.shape[0]
        for k in range(t):
            for lt in range(ntile):
                stage_ref[lt, pl.ds(k, y_ref.shape[1], stride=t), :] = y_ref[
                    k, :, lt * LANES:(lt + 1) * LANES].astype(F32)
        y = jnp.concatenate([stage_ref[lt] for lt in range(ntile)], axis=1)
    else:
        y = y_ref[...].astype(F32)
    gl = _gelu(y)
    z = jnp.dot(gl.astype(BF16), wglu_ref[...], preferred_element_type=F32)
    t = (gl * _sigmoid(z)).astype(BF16)
    y_ssm = jnp.dot(t, wssm_ref[...], preferred_element_type=F32)
    mixed = _sigmoid(gna_ref[...].astype(F32)) * y_na + _sigmoid(gssm_ref[...].astype(F32)) * y_ssm
    out_ref[...] = mixed.astype(out_ref.dtype)


def mix_branches(o_na, y, proj, w_na, w_glu, w_ssm, gate_col):
    rows, na_width = o_na.shape
    ssm_width = y.shape[-1]
    d = w_na.shape[1]
    tm = _pick(rows, MID_ROW_TILES)
    assert gate_col % d == 0
    gblk = gate_col // d
    resident = dict(pipeline_mode=pl.Buffered(1))
    chunk_major = y.ndim == 3
    if chunk_major:
        t = y.shape[0]
        y_spec = pl.BlockSpec((t, tm // t, ssm_width), lambda i: (0, i, 0))
        scratch = [pltpu.VMEM((ssm_width // LANES, tm, LANES), F32)]
    else:
        y_spec = pl.BlockSpec((tm, ssm_width), lambda i: (i, 0))
        scratch = []
    return pl.pallas_call(
        functools.partial(_mix_kernel, chunk_major=chunk_major),
        grid=(rows // tm,),
        scratch_shapes=scratch,
        in_specs=[
            pl.BlockSpec((tm, na_width), lambda i: (i, 0)),
            y_spec,
            pl.BlockSpec((tm, d), lambda i: (i, gblk)),
            pl.BlockSpec((tm, d), lambda i: (i, gblk + 1)),
            pl.BlockSpec(w_na.shape, lambda i: (0, 0), **resident),
            pl.BlockSpec(w_glu.shape, lambda i: (0, 0), **resident),
            pl.BlockSpec(w_ssm.shape, lambda i: (0, 0), **resident),
        ],
        out_specs=pl.BlockSpec((tm, d), lambda i: (i, 0)),
        out_shape=jax.ShapeDtypeStruct((rows, d), BF16),
        compiler_params=_params("parallel"),
        name="mix_branches",
    )(o_na, y, proj, proj, w_na, w_glu, w_ssm)


def _residual_matmul_kernel(h_ref, a_ref, w_ref, g_ref, o_ref, n_ref):
    h1 = h_ref[...] + jnp.dot(a_ref[...], w_ref[...], preferred_element_type=F32)
    o_ref[...] = h1
    n_ref[...] = _rms(h1, g_ref[...]).astype(n_ref.dtype)


def residual_matmul(h, a, w, g):
    rows, d = h.shape
    tm = _pick(rows, MID_ROW_TILES)
    row_spec = pl.BlockSpec((tm, d), lambda i: (i, 0))
    return pl.pallas_call(
        _residual_matmul_kernel,
        grid=(rows // tm,),
        in_specs=[
            row_spec,
            pl.BlockSpec((tm, a.shape[1]), lambda i: (i, 0)),
            pl.BlockSpec(w.shape, lambda i: (0, 0), pipeline_mode=pl.Buffered(1)),
            pl.BlockSpec((1, d), lambda i: (0, 0)),
        ],
        out_specs=[row_spec, row_spec],
        out_shape=[jax.ShapeDtypeStruct((rows, d), F32), jax.ShapeDtypeStruct((rows, d), BF16)],
        compiler_params=_params("parallel"),
        name="residual_out_proj",
    )(h, a, w, g.reshape(1, d))


HALO = BF16_SUBLANES


def _ffn_up_kernel(h_ref, prev_ref, next_ref, wa_ref, wg_ref, cw_ref, cb_ref, o_ref, hn_ref, *, tm):
    @pl.when(pl.program_id(1) == 0)
    def _():
        hn_ref[0:HALO, :] = prev_ref[...]
        hn_ref[HALO:HALO + tm, :] = h_ref[...]
        hn_ref[HALO + tm:, :] = next_ref[...]

    ext = tm + 2 * HALO
    a = jnp.dot(hn_ref[...], wa_ref[...].astype(BF16), preferred_element_type=F32)
    gate = jnp.dot(hn_ref[HALO:HALO + tm, :], wg_ref[...].astype(BF16), preferred_element_type=F32)
    a_prev = pltpu.roll(a, 1, 0)[HALO:HALO + tm]
    a_next = pltpu.roll(a, ext - 1, 0)[HALO:HALO + tm]
    conv = a_prev * cw_ref[0:1, :] + a[HALO:HALO + tm] * cw_ref[1:2, :] + a_next * cw_ref[2:3, :] + cb_ref[...]
    o_ref[...] = (_gelu(conv) * gate).astype(o_ref.dtype)


def ffn_up(h, halo_prev, halo_next, w_up, conv_w, conv_b, tm):
    rows, d = h.shape
    dff = conv_b.shape[0]
    tn = _pick(dff, COL_TILES)
    nj = dff // tn
    kernel = functools.partial(_ffn_up_kernel, tm=tm)
    return pl.pallas_call(
        kernel,
        grid=(rows // tm, nj),
        in_specs=[
            pl.BlockSpec((tm, d), lambda i, j: (i, 0)),
            pl.BlockSpec((HALO, d), lambda i, j: (i, 0)),
            pl.BlockSpec((HALO, d), lambda i, j: (i, 0)),
            pl.BlockSpec((d, tn), lambda i, j: (0, j)),
            pl.BlockSpec((d, tn), lambda i, j: (0, nj + j)),
            pl.BlockSpec((conv_w.shape[0], tn), lambda i, j: (0, j)),
            pl.BlockSpec((1, tn), lambda i, j: (0, j)),
        ],
        out_specs=pl.BlockSpec((tm, tn), lambda i, j: (i, j)),
        out_shape=jax.ShapeDtypeStruct((rows, dff), BF16),
        scratch_shapes=[pltpu.VMEM((tm + 2 * HALO, d), BF16)],
        compiler_params=_params("parallel", "arbitrary"),
        name="ffn_up_conv_gate",
    )(h, halo_prev, halo_next, w_up, w_up, conv_w, conv_b.reshape(1, dff))


def _ffn_down_kernel(h_ref, a_ref, w_ref, g_ref, o_ref):
    k = pl.program_id(1)

    @pl.when(k == 0)
    def _():
        o_ref[...] = h_ref[...]

    o_ref[...] += jnp.dot(a_ref[...], w_ref[...], preferred_element_type=F32)

    @pl.when(k == pl.num_programs(1) - 1)
    def _():
        o_ref[...] = _rms(o_ref[...], g_ref[...])


def ffn_down(h, act, w_down, g):
    rows, d = h.shape
    dff = act.shape[1]
    tm = _pick(rows, ROW_TILES)
    tk = _pick(dff, COL_TILES)
    return pl.pallas_call(
        _ffn_down_kernel,
        grid=(rows // tm, dff // tk),
        in_specs=[
            pl.BlockSpec((tm, d), lambda i, k: (i, 0)),
            pl.BlockSpec((tm, tk), lambda i, k: (i, k)),
            pl.BlockSpec((tk, d), lambda i, k: (k, 0)),
            pl.BlockSpec((1, d), lambda i, k: (0, 0)),
        ],
        out_specs=pl.BlockSpec((tm, d), lambda i, k: (i, 0)),
        out_shape=jax.ShapeDtypeStruct((rows, d), F32),
        compiler_params=_params("parallel", "arbitrary"),
        name="ffn_down_final_norm",
    )(h, act, w_down, g.reshape(1, d))


def _conv_halos(h1_main, h1_meta, batch, seq, tm):
    d = h1_main.shape[1]
    per_seq = seq // tm
    hm = h1_main.reshape(batch, per_seq, tm, d)
    meta_tail = h1_meta.reshape(batch, 1, N_META, d)[:, :, N_META - HALO:]
    prev = jnp.concatenate([meta_tail, hm[:, :-1, tm - HALO:]], axis=1)
    nxt = jnp.concatenate([hm[:, 1:, :HALO], jnp.zeros((batch, 1, HALO, d), h1_main.dtype)], axis=1)
    return prev.reshape(batch * per_seq * HALO, d), nxt.reshape(batch * per_seq * HALO, d)


def kernel(x, meta_tokens, norm1_g, w_in, na_rpb, ssm_lam_re, ssm_lam_im, ssm_log_step, ssm_b_re, ssm_b_im,
           ssm_c_re, ssm_c_im, ssm_d, w_glu, w_proj_na, w_proj_ssm, w_out, norm2_g, w_up, conv_w, conv_b,
           w_down, final_g):
    batch, seq, d = x.shape
    depth = w_in.shape[0]
    na_width = w_proj_na.shape[1]
    ssm_width = w_proj_ssm.shape[1]
    assert depth == 1 and N_META >= HALO
    l = 0
    h_main = x.reshape(batch * seq, d)
    h_meta = jnp.broadcast_to(meta_tokens.astype(x.dtype)[None], (batch, N_META, d)).reshape(batch * N_META, d)

    w_in_b = w_in[l].astype(BF16)
    u_col = 3 * na_width
    proj_main, u_main = norm_matmul(h_main, norm1_g[l], w_in_b, chunk_major=(u_col, ssm_width))
    proj_meta = norm_matmul(h_meta, norm1_g[l], w_in_b)

    o_main, o_meta = neighbourhood_attention(proj_main, proj_meta, na_rpb[l], batch, seq, na_width)

    ops = _ssm_operators(ssm_lam_re[l], ssm_lam_im[l], ssm_log_step[l], ssm_b_re[l], ssm_b_im[l],
                         ssm_c_re[l], ssm_c_im[l], ssm_d[l])
    y_main, y_meta = s5_scan(u_main, proj_meta[:, u_col:u_col + ssm_width], ops, batch, seq)

    w_na_b, w_glu_b, w_ssm_b = w_proj_na[l].astype(BF16), w_glu[l].astype(BF16), w_proj_ssm[l].astype(BF16)
    w_out_b = w_out[l].astype(BF16)
    gate_col = u_col + ssm_width
    h1 = []
    for h, o, y, proj in ((h_main, o_main, y_main, proj_main), (h_meta, o_meta, y_meta, proj_meta)):
        mixed = mix_branches(o, y, proj, w_na_b, w_glu_b, w_ssm_b, gate_col)
        h1.append(residual_matmul(h, mixed, w_out_b, norm2_g[l]))
    (h1_main, hn_main), (_, hn_meta) = h1

    tm = _pick(seq, ROW_TILES)
    halo_prev, halo_next = _conv_halos(hn_main, hn_meta, batch, seq, tm)
    act = ffn_up(hn_main, halo_prev, halo_next, w_up[l], conv_w[l], conv_b[l], tm)
    out = ffn_down(h1_main, act, w_down[l].astype(BF16), final_g)
    return out.reshape(batch, seq, d)
```

```python
import functools
import math

import jax
import jax.numpy as jnp
from jax import lax
from jax.experimental import pallas as pl
from jax.experimental.pallas import tpu as pltpu

F32 = jnp.float32
BF16 = jnp.bfloat16

N_META = 16
GRID_W = 64
ROW_WIN = 8
COL_WIN = 16
RMS_EPS = 1e-6

VMEM_LIMIT_BYTES = 56 * 1024 * 1024
F32_SUBLANES = 8
BF16_SUBLANES = 16
LANES = 128

MASK_BIAS = -1e30
NA_QROWS = 4
NA_KROWS = 12
NA_UNROLL = 8

SSM_CHUNK = 16
SSM_SUB = 2
SSM_SCAN_UNROLL = True
SSM_META_ROWS = BF16_SUBLANES

ROW_TILES = (1024, 512, 256, 128, 64)
MID_ROW_TILES = (512, 256, 128, 64)
COL_TILES = (512, 256, 128)
WIDE_COL_TILES = (1024, 512, 256, 128)


def _pick(dim, prefs):
    for t in prefs:
        if dim % t == 0:
            return t
    return dim


def _params(*sem):
    return pltpu.CompilerParams(dimension_semantics=sem, vmem_limit_bytes=VMEM_LIMIT_BYTES)


def _rms(x, g):
    ms = jnp.mean(x * x, axis=-1, keepdims=True)
    return x * lax.rsqrt(ms + RMS_EPS) * g


def _gelu(x):
    c = math.sqrt(2.0 / math.pi)
    return 0.5 * x * (1.0 + jnp.tanh(c * (x + 0.044715 * (x * x * x))))


def _sigmoid(x):
    return 0.5 * jnp.tanh(0.5 * x) + 0.5


def _norm_matmul_kernel(x_ref, g_ref, w_ref, o_ref, *rest, chunk_cols):
    hn_ref = rest[-1] if chunk_cols is None else rest[1]

    @pl.when(pl.program_id(1) == 0)
    def _():
        hn_ref[...] = _rms(x_ref[...], g_ref[...]).astype(BF16)

    r = jnp.dot(hn_ref[...], w_ref[...], preferred_element_type=F32)
    o_ref[...] = r.astype(o_ref.dtype)

    if chunk_cols is not None:
        u3_ref, _, stage_ref = rest
        tile, off, width = chunk_cols
        t = SSM_CHUNK

        @pl.when(pl.program_id(1) == tile)
        def _():
            for lt in range(width // LANES):
                stage_ref[lt] = r[:, off + lt * LANES:off + (lt + 1) * LANES]
            for k in range(t):
                for lt in range(width // LANES):
                    u3_ref[k, :, lt * LANES:(lt + 1) * LANES] = stage_ref[
                        lt, pl.ds(k, stage_ref.shape[1] // t, stride=t), :].astype(u3_ref.dtype)


def norm_matmul(x, g, w, chunk_major=None):
    rows, d = x.shape
    n = w.shape[1]
    tm = _pick(rows, ROW_TILES)
    tn = _pick(n, WIDE_COL_TILES)
    out_specs = [pl.BlockSpec((tm, tn), lambda i, j: (i, j))]
    out_shape = [jax.ShapeDtypeStruct((rows, n), BF16)]
    scratch = [pltpu.VMEM((tm, d), BF16)]
    chunk_cols = None
    if chunk_major is not None:
        col, width = chunk_major
        t = SSM_CHUNK
        chunk_cols = (col // tn, col % tn, width)
        assert col % tn + width <= tn and tm % (t * F32_SUBLANES) == 0
        out_specs.append(pl.BlockSpec((t, tm // t, width), lambda i, j: (0, i, 0)))
        out_shape.append(jax.ShapeDtypeStruct((t, rows // t, width), BF16))
        scratch.append(pltpu.VMEM((width // LANES, tm, LANES), F32))
    out = pl.pallas_call(
        functools.partial(_norm_matmul_kernel, chunk_cols=chunk_cols),
        grid=(rows // tm, n // tn),
        in_specs=[
            pl.BlockSpec((tm, d), lambda i, j: (i, 0)),
            pl.BlockSpec((1, d), lambda i, j: (0, 0)),
            pl.BlockSpec((d, tn), lambda i, j: (0, j)),
        ],
        out_specs=out_specs,
        out_shape=out_shape,
        scratch_shapes=scratch,
        compiler_params=_params("parallel", "arbitrary"),
        name="norm_in_proj",
    )(x, g.reshape(1, d), w)
    return out[0] if chunk_major is None else out


def _na_kernel(q_ref, k_ref, v_ref, qm_ref, km_ref, vm_ref, tab_ref, o_ref, om_ref, bias_ref, sm_ref, acc_ref,
               *, rows, dh):
    lane = lax.broadcasted_iota(jnp.int32, (1, 2 * dh), 1)
    head_masks = (lane < dh, lane >= dh)
    scale = dh ** -0.5
    km = km_ref[...]
    vm = vm_ref[...]
    nt = (((1,), (1,)), ((), ()))

    masked = 2 * ROW_WIN - 1
    nblk = rows // NA_QROWS
    qblk = NA_QROWS * GRID_W
    kblk = NA_KROWS * GRID_W

    def rel_row(kind, j, i):
        if kind == 0:
            return i - j + (ROW_WIN - 1) if i < ROW_WIN else masked
        if kind == 1:
            return i - j + (ROW_WIN - 1 - ROW_WIN // 2) if j <= i < j + ROW_WIN else masked
        lo = NA_KROWS - ROW_WIN
        return i - j + (NA_QROWS - NA_KROWS) + (ROW_WIN - 1) if i >= lo else masked

    first_row = lane < GRID_W
    for kind in range(3):
        for hh in range(2):
            for j in range(NA_QROWS):
                r_lo = (hh * NA_QROWS + j) * GRID_W
                for i2 in range(NA_KROWS // 2):
                    bias_ref[kind, r_lo:r_lo + GRID_W, i2 * 2 * GRID_W:(i2 + 1) * 2 * GRID_W] = jnp.where(
                        first_row, tab_ref[hh, rel_row(kind, j, 2 * i2)], tab_ref[hh, rel_row(kind, j, 2 * i2 + 1)])

    def one_head(x, hh):
        return jnp.where(head_masks[hh], x, jnp.zeros_like(x))

    qm = (qm_ref[...] * scale).astype(BF16)
    om = None
    for hh in range(2):
        s = lax.dot_general(one_head(qm, hh), km, nt, preferred_element_type=F32)
        p = jnp.exp(s - jnp.max(s, axis=-1, keepdims=True))
        o_h = jnp.dot(p.astype(BF16), vm, preferred_element_type=F32) / jnp.sum(p, axis=-1, keepdims=True)
        om = o_h if om is None else jnp.where(head_masks[1], o_h, om)
    om_ref[...] = om.astype(om_ref.dtype)

    q_all = (q_ref[...] * scale).astype(BF16)
    for hh in range(2):
        sm_ref[hh] = lax.dot_general(one_head(q_all, hh), km, nt, preferred_element_type=F32)

    def body(blk, carry):
        r = blk * NA_QROWS
        k0 = jnp.clip(r - ROW_WIN // 2, 0, rows - NA_KROWS)
        kind = jnp.where(blk == 0, 0, jnp.where(blk == nblk - 1, 2, 1))
        qrows = pl.ds(pl.multiple_of(r * GRID_W, qblk), qblk)
        krows = pl.ds(pl.multiple_of(k0 * GRID_W, GRID_W), kblk)
        qs = (q_ref[qrows, :] * scale).astype(BF16)
        q2 = jnp.concatenate([one_head(qs, 0), one_head(qs, 1)], axis=0)
        s = lax.dot_general(q2, k_ref[krows, :], nt, preferred_element_type=F32) + bias_ref[kind]
        s_m = jnp.concatenate([sm_ref[0, qrows, :], sm_ref[1, qrows, :]], axis=0)
        m = jnp.maximum(jnp.max(s, axis=-1, keepdims=True), jnp.max(s_m, axis=-1, keepdims=True))
        p = jnp.exp(s - m)
        p_m = jnp.exp(s_m - m)
        inv = 1.0 / (jnp.sum(p, axis=-1, keepdims=True) + jnp.sum(p_m, axis=-1, keepdims=True))
        acc = jnp.dot(p.astype(BF16), v_ref[krows, :], preferred_element_type=F32) * inv
        acc_ref[qrows, :] = jnp.where(head_masks[1], acc[qblk:], acc[:qblk])
        p_m = p_m * inv
        sm_ref[0, qrows, :] = p_m[:qblk]
        sm_ref[1, qrows, :] = p_m[qblk:]
        return carry

    lax.fori_loop(0, nblk, body, 0, unroll=NA_UNROLL)

    o_meta = [jnp.dot(sm_ref[hh].astype(BF16), vm, preferred_element_type=F32) for hh in range(2)]
    o_ref[...] = (acc_ref[...] + jnp.where(head_masks[1], o_meta[1], o_meta[0])).astype(o_ref.dtype)


def _na_bias_table(rpb):
    c = jnp.arange(GRID_W)[:, None]
    kc = jnp.arange(2 * GRID_W)[None, :] % GRID_W
    col_start = jnp.clip(c - COL_WIN // 2, 0, GRID_W - COL_WIN)
    valid = (kc >= col_start) & (kc < col_start + COL_WIN)
    dc = kc - c + (COL_WIN - 1)
    pick = (dc.reshape(-1)[None] == jnp.arange(2 * COL_WIN - 1)[:, None]).astype(F32)
    tab = jnp.dot(rpb.astype(F32).reshape(-1, 2 * COL_WIN - 1), pick, precision=lax.Precision.HIGHEST)
    tab = tab.reshape(rpb.shape[0], 2 * ROW_WIN - 1, GRID_W, 2 * GRID_W)
    tab = jnp.where(valid, tab, MASK_BIAS)
    return jnp.concatenate([tab, jnp.full_like(tab[:, :1], MASK_BIAS)], axis=1)


def neighbourhood_attention(proj_main, proj_meta, rpb, batch, seq, na_width):
    heads = rpb.shape[0]
    dh = na_width // heads
    rows = seq // GRID_W
    assert rows >= NA_KROWS and rows % NA_QROWS == 0 and heads % 2 == 0 and 2 * dh == LANES
    assert NA_KROWS % 2 == 0 and NA_KROWS >= ROW_WIN + NA_QROWS - 1 and NA_QROWS <= ROW_WIN // 2
    pairs = heads // 2
    bias = _na_bias_table(rpb)
    blk = (seq, 2 * dh)
    mblk = (N_META, 2 * dh)
    kernel = functools.partial(_na_kernel, rows=rows, dh=dh)
    return pl.pallas_call(
        kernel,
        grid=(batch, pairs),
        in_specs=[
            pl.BlockSpec(blk, lambda b, h: (b, h)),
            pl.BlockSpec(blk, lambda b, h: (b, pairs + h)),
            pl.BlockSpec(blk, lambda b, h: (b, 2 * pairs + h)),
            pl.BlockSpec(mblk, lambda b, h: (b, h)),
            pl.BlockSpec(mblk, lambda b, h: (b, pairs + h)),
            pl.BlockSpec(mblk, lambda b, h: (b, 2 * pairs + h)),
            pl.BlockSpec((2, 2 * ROW_WIN, GRID_W, 2 * GRID_W), lambda b, h: (h, 0, 0, 0)),
        ],
        out_specs=[
            pl.BlockSpec(blk, lambda b, h: (b, h)),
            pl.BlockSpec(mblk, lambda b, h: (b, h)),
        ],
        out_shape=[
            jax.ShapeDtypeStruct((batch * seq, na_width), BF16),
            jax.ShapeDtypeStruct((batch * N_META, na_width), BF16),
        ],
        scratch_shapes=[pltpu.VMEM((3, 2 * NA_QROWS * GRID_W, NA_KROWS * GRID_W), F32),
                        pltpu.VMEM((2, seq, N_META), F32),
                        pltpu.VMEM((seq, 2 * dh), F32)],
        compiler_params=_params("parallel", "parallel"),
        name="neighbourhood_attention",
    )(proj_main, proj_main, proj_main, proj_meta, proj_meta, proj_meta, bias)


def _chunk_ops_kernel(cp_ref, d_ref, shift_ref, pw_ref, bt_ref, me_ref, f_ref):
    gt, _, n, packed = cp_ref.shape
    c = bt_ref.shape[2]
    t, _, tc = shift_ref.shape
    w = pw_ref.shape[3]
    q = lax.broadcasted_iota(jnp.int32, (packed, tc), 1)
    p_lane = lax.broadcasted_iota(jnp.int32, (packed, tc), 0)
    k_row = lax.broadcasted_iota(jnp.int32, (c, tc), 0)

    def spread(parts, k, by_position):
        mat = (p_lane - k * c == (q // c if by_position else q % c)).astype(BF16)
        return sum(jnp.dot(x, mat, preferred_element_type=F32) for x in parts)

    lag_rows = [[None, None] for _ in range(gt)]
    for d in range(2):
        x = cp_ref[:, d].reshape(gt * n, packed)
        hi = x.astype(BF16)
        parts = (hi, (x - hi.astype(F32)).astype(BF16))
        cx_r, cx_i = spread(parts, 0, False), spread(parts, 1, False)
        e_r, e_i = spread(parts, 4, True), spread(parts, 5, True)
        f_ref[:, d * n:(d + 1) * n, :] = (cx_r * e_r - cx_i * e_i).reshape(gt, n, tc).astype(f_ref.dtype)
        f_ref[:, (2 + d) * n:(3 + d) * n, :] = (-(cx_r * e_i + cx_i * e_r)).reshape(gt, n, tc).astype(f_ref.dtype)
        e_r, e_i = spread(parts, 2, True), spread(parts, 3, True)
        z_r, z_i = cx_r * e_r - cx_i * e_i, cx_r * e_i + cx_i * e_r
        for g in range(gt):
            zr_g, zi_g = z_r[g * n:(g + 1) * n], z_i[g * n:(g + 1) * n]
            x_g = cp_ref[g, d]
            rows = jnp.zeros((c, tc), F32)
            for ci in range(c):
                b_r, b_i = x_g[:, 6 * c + ci:6 * c + ci + 1], x_g[:, 7 * c + ci:7 * c + ci + 1]
                row = jnp.sum(b_r * zr_g - b_i * zi_g, axis=0, keepdims=True)
                rows = jnp.where(k_row == ci, row, rows)
            lag_rows[g][d] = rows
    lag = jnp.concatenate([jnp.concatenate(lag_rows[g], axis=1) for g in range(gt)], axis=0).astype(BF16)

    lane = lax.broadcasted_iota(jnp.int32, (gt * c, tc), 1)
    c_in = lax.broadcasted_iota(jnp.int32, (gt * c, tc), 0) % c
    skip_gain = d_ref[...].reshape(gt * c, tc)
    b_r, b_i = bt_ref[:, 0], bt_ref[:, 1]
    for s in range(t):
        rows = slice(s * c, (s + 1) * c)
        blk = jnp.dot(lag, shift_ref[s], preferred_element_type=F32)
        blk = blk + jnp.where(lane == s * c + c_in, skip_gain, 0.0)
        me_ref[:, rows, 0:tc] = blk.reshape(gt, c, tc).astype(me_ref.dtype)
        e_r, e_i = pw_ref[:, 0, s:s + 1, :], pw_ref[:, 1, s:s + 1, :]
        me_ref[:, rows, tc:tc + w] = (e_r * b_r - e_i * b_i).astype(me_ref.dtype)
        me_ref[:, rows, tc + w:tc + 2 * w] = (e_r * b_i + e_i * b_r).astype(me_ref.dtype)


def _chunk_operators(cp, d_rows, pw, bt):
    g, _, n, packed = cp.shape
    t, c = pw.shape[2], bt.shape[2]
    tc = t * c
    w = pw.shape[-1]
    assert t == c and packed == 8 * c
    gt = _pick(g, (8, 4, 2, 1))
    p = jnp.arange(2 * tc)[None, :, None]
    q = jnp.arange(tc)[None, None, :]
    s = jnp.arange(t)[:, None, None]
    shift = jnp.where(p < tc, p == q - s * c, p - tc == q + (t - 1 - s) * c).astype(BF16)

    def per_group(shape):
        return pl.BlockSpec((gt,) + tuple(shape), lambda i: (i,) + (0,) * len(shape))

    return pl.pallas_call(
        _chunk_ops_kernel,
        grid=(g // gt,),
        in_specs=[
            per_group(cp.shape[1:]), per_group(d_rows.shape[1:]),
            pl.BlockSpec(shift.shape, lambda i: (0, 0, 0)),
            per_group(pw.shape[1:]), per_group(bt.shape[1:]),
        ],
        out_specs=[per_group((tc, tc + 2 * w)), per_group((4 * n, tc))],
        out_shape=[jax.ShapeDtypeStruct((g, tc, tc + 2 * w), BF16), jax.ShapeDtypeStruct((g, 4 * n, tc), BF16)],
        compiler_params=_params("parallel"),
        name="s5_chunk_operators",
    )(cp, d_rows, shift, pw, bt)


def _ssm_operators(lam_re, lam_im, log_step, b_re, b_im, c_re, c_im, d_skip):
    t = SSM_CHUNK
    lr, li = lam_re.astype(F32), lam_im.astype(F32)
    step = jnp.exp(log_step.astype(F32))[..., None]
    dt_r, dt_i = lr * step, li * step
    g, n = lr.shape[1], lr.shape[2]
    c = b_re.shape[-1]

    def lam_bar_pow(k):
        mag = jnp.exp(dt_r[:, :, None] * k)
        return mag * jnp.cos(dt_i[:, :, None] * k), mag * jnp.sin(dt_i[:, :, None] * k)

    p_r, p_i = lam_bar_pow(jnp.arange(t + 1, dtype=F32)[:, None])
    x_r, x_i = p_r[:, :, 1] - 1.0, p_i[:, :, 1]
    den = lr * lr + li * li
    q_r, q_i = (x_r * lr + x_i * li) / den, (x_i * lr - x_r * li) / den
    bt_r, bt_i = jnp.swapaxes(b_re.astype(F32), 2, 3), jnp.swapaxes(b_im.astype(F32), 2, 3)
    bb_r = q_r[:, :, None] * bt_r - q_i[:, :, None] * bt_i
    bb_i = q_r[:, :, None] * bt_i + q_i[:, :, None] * bt_r
    bn_r = q_r[..., None] * b_re.astype(F32) - q_i[..., None] * b_im.astype(F32)
    bn_i = q_r[..., None] * b_im.astype(F32) + q_i[..., None] * b_re.astype(F32)
    ct_r, ct_i = jnp.swapaxes(c_re.astype(F32), 2, 3), jnp.swapaxes(c_im.astype(F32), 2, 3)

    def cols(x):
        return jnp.swapaxes(x, 1, 2)

    factors = []
    for d, lag_sel, out_sel in ((0, lambda x: x[:, :t], lambda x: x[:, 1:]),
                                (1, lambda x: x[:, :t][:, ::-1], lambda x: x[:, ::-1][:, :t])):
        factors.append(jnp.concatenate([ct_r[d], ct_i[d], cols(lag_sel(p_r[d])), cols(lag_sel(p_i[d])),
                                        cols(out_sel(p_r[d])), cols(out_sel(p_i[d])), bn_r[d], bn_i[d]], axis=-1))
    cp = jnp.stack(factors, axis=1)
    d_rows = jnp.broadcast_to(d_skip.astype(F32).reshape(g, c, 1), (g, c, t * c))

    pw = jnp.stack([jnp.concatenate([p_r[0][:, ::-1][:, 1:], p_r[1][:, :t]], axis=-1),
                    jnp.concatenate([p_i[0][:, ::-1][:, 1:], p_i[1][:, :t]], axis=-1)], axis=1)
    bt = jnp.stack([jnp.concatenate([bb_r[0], bb_r[1]], axis=-1),
                    jnp.concatenate([bb_i[0], bb_i[1]], axis=-1)], axis=1)
    me, f = _chunk_operators(cp, d_rows, pw, bt)

    sub = F32_SUBLANES
    row = jnp.arange(sub)

    d_r, d_i = lam_bar_pow((t * (row + 1)).astype(F32)[:, None])
    d_r = jnp.concatenate([d_r[0], d_r[1]], axis=-1)
    d_i = jnp.concatenate([d_i[0], d_i[1]], axis=-1)

    def fixed(x, k, keep):
        return jnp.where(keep[None, :, None], x[:, k - 1:k], 0.0)

    slots = []
    for k in (1, 2, 4):
        slots += [fixed(d_r, k, row >= k), fixed(d_i, k, row >= k)]
    slots += [d_r, d_i]
    for k in (1, 2, 4):
        slots += [fixed(d_r, k, row < sub - k), fixed(d_i, k, row < sub - k)]
    slots += [d_r[:, ::-1], d_i[:, ::-1]]
    a = jnp.concatenate(slots, axis=1).reshape(g, len(slots), sub, 2 * n)

    toks = LANES // c
    grp = LANES // c
    src = jnp.transpose(jnp.arange(toks * grp * c).reshape(toks, grp, c), (1, 0, 2)).reshape(-1)
    perm = (jnp.arange(toks * grp * c)[:, None] == src[None, :]).astype(BF16)
    return perm, me.astype(BF16), f.astype(BF16), a.astype(F32)


def _ssm_kernel(*refs, batch, nchunks, nstate, gsub):
    t = SSM_CHUNK
    (u_ref, um_ref, p_ref, me_ref, f_ref, a_ref, y_ref, ym_ref,
     yi_ref, ee_ref, pf_ref, pr_ref, ycat_ref, ymcat_ref) = refs
    gtile = me_ref.shape[0]
    tc = f_ref.shape[2]
    toks = p_ref.shape[0] // LANES
    parts = t // toks
    sub = F32_SUBLANES
    w = 2 * nstate
    tiles = nchunks // sub
    perm = p_ref[...]
    row = lax.broadcasted_iota(jnp.int32, (sub, w), 0)
    lane = lax.broadcasted_iota(jnp.int32, (1, w), 1)
    rev_lane = lane >= nstate

    def group_major(token_refs):
        out = []
        for part in range(parts):
            x = jnp.concatenate([token_refs[part * toks + i] for i in range(toks)], axis=1)
            out.append(jnp.dot(x, perm, preferred_element_type=F32).astype(BF16))
        return out

    def group_chunk(regrouped, g):
        return jnp.concatenate([x[:, g * LANES:(g + 1) * LANES] for x in regrouped], axis=1)

    v_main = group_major(u_ref)
    v_meta = group_major(um_ref)

    def cmul_add(xr, xi, ar, ai, yr, yi):
        return xr + ar * yr - ai * yi, xi + ar * yi + ai * yr

    def bcast_row(x, r):
        return jnp.broadcast_to(x[r:r + 1, :], (sub, w))

    for sb in range(gtile // gsub):
        gs = [sb * gsub + gi for gi in range(gsub)]
        meta = []
        for gi, g in enumerate(gs):
            me = jnp.dot(group_chunk(v_main, g), me_ref[g], preferred_element_type=F32)
            yi_ref[gi] = me[:, :tc]
            ee_ref[gi] = me[:, tc:]
            meta.append(jnp.dot(group_chunk(v_meta, g), me_ref[g], preferred_element_type=F32))

        def body(j, carry, gs=gs):
            new = []
            for gi, g in enumerate(gs):
                for b in range(batch):
                    lr, li, fr, fi = carry[gi * batch + b]
                    fs = pl.ds(pl.multiple_of(b * nchunks + j * sub, sub), sub)
                    xr, xi = ee_ref[gi, fs, 0:w], ee_ref[gi, fs, w:2 * w]
                    for step, k in enumerate((1, 2, 4)):
                        xr, xi = cmul_add(xr, xi, a_ref[g, 2 * step], a_ref[g, 2 * step + 1],
                                          pltpu.roll(xr, k, 0), pltpu.roll(xi, k, 0))
                    sr, si = cmul_add(xr, xi, a_ref[g, 6], a_ref[g, 7], lr, li)
                    pf_ref[gi, fs, 0:w] = jnp.where(row == 0, lr, pltpu.roll(sr, 1, 0))
                    pf_ref[gi, fs, w:2 * w] = jnp.where(row == 0, li, pltpu.roll(si, 1, 0))
                    rs = pl.ds(pl.multiple_of(b * nchunks + (tiles - 1 - j) * sub, sub), sub)
                    xr, xi = ee_ref[gi, rs, 0:w], ee_ref[gi, rs, w:2 * w]
                    for step, k in enumerate((1, 2, 4)):
                        xr, xi = cmul_add(xr, xi, a_ref[g, 8 + 2 * step], a_ref[g, 9 + 2 * step],
                                          pltpu.roll(xr, sub - k, 0), pltpu.roll(xi, sub - k, 0))
                    rr, ri = cmul_add(xr, xi, a_ref[g, 14], a_ref[g, 15], fr, fi)
                    pr_ref[gi, rs, 0:w] = jnp.where(row == sub - 1, fr, pltpu.roll(rr, sub - 1, 0))
                    pr_ref[gi, rs, w:2 * w] = jnp.where(row == sub - 1, fi, pltpu.roll(ri, sub - 1, 0))
                    new.append((bcast_row(sr, sub - 1), bcast_row(si, sub - 1), bcast_row(rr, 0), bcast_row(ri, 0)))
            return tuple(new)

        zero = jnp.zeros((sub, w), F32)
        init = tuple((bcast_row(meta[gi][:, tc:tc + w], b), bcast_row(meta[gi][:, tc + w:tc + 2 * w], b), zero, zero)
                     for gi in range(gsub) for b in range(batch))
        final = lax.fori_loop(0, tiles, body, init, unroll=SSM_SCAN_UNROLL)

        for gi, g in enumerate(gs):
            prev = jnp.concatenate([jnp.where(rev_lane, pr_ref[gi, :, k * w:(k + 1) * w], pf_ref[gi, :, k * w:(k + 1) * w])
                                    for k in range(2)], axis=1).astype(BF16)
            y = (yi_ref[gi] + jnp.dot(prev, f_ref[g], preferred_element_type=F32)).astype(BF16)
            pm = []
            for k in (2, 3):
                x = zero
                for b in range(batch):
                    x = jnp.where((row == b) & rev_lane, final[gi * batch + b][k], x)
                pm.append(jnp.concatenate([x, jnp.zeros((SSM_META_ROWS - sub, w), F32)], axis=0))
            prev_m = jnp.concatenate(pm, axis=1).astype(BF16)
            y_m = (meta[gi][:, :tc] + jnp.dot(prev_m, f_ref[g], preferred_element_type=F32)).astype(BF16)
            for part in range(parts):
                ycat_ref[part, :, g * LANES:(g + 1) * LANES] = y[:, part * LANES:(part + 1) * LANES]
                ymcat_ref[part, :, g * LANES:(g + 1) * LANES] = y_m[:, part * LANES:(part + 1) * LANES]

    for part in range(parts):
        o = jnp.dot(ycat_ref[part], perm, preferred_element_type=F32).astype(y_ref.dtype)
        o_m = jnp.dot(ymcat_ref[part], perm, preferred_element_type=F32).astype(ym_ref.dtype)
        for i in range(toks):
            y_ref[part * toks + i] = o[:, i * LANES:(i + 1) * LANES]
            ym_ref[part * toks + i] = o_m[:, i * LANES:(i + 1) * LANES]


def s5_scan(u_main, u_meta_rows, ops, batch, seq):
    perm, me, f, a = ops
    g = me.shape[0]
    t = SSM_CHUNK
    tc = f.shape[2]
    c = tc // t
    width = g * c
    nstate = f.shape[1] // 4
    gtile = LANES // c
    nchunks = seq // t
    rows = batch * nchunks
    assert N_META == t and seq % (t * F32_SUBLANES) == 0 and batch <= F32_SUBLANES
    assert g % gtile == 0 and gtile % SSM_SUB == 0 and u_main.shape == (t, rows, width)

    u_meta = jnp.transpose(u_meta_rows.reshape(batch, t, width), (1, 0, 2))
    u_meta = jnp.pad(u_meta, ((0, 0), (0, SSM_META_ROWS - batch), (0, 0)))
    kernel = functools.partial(_ssm_kernel, batch=batch, nchunks=nchunks, nstate=nstate, gsub=SSM_SUB)
    y, y_m = pl.pallas_call(
        kernel,
        grid=(g // gtile,),
        in_specs=[
            pl.BlockSpec((t, rows, LANES), lambda i: (0, 0, i)),
            pl.BlockSpec((t, SSM_META_ROWS, LANES), lambda i: (0, 0, i)),
            pl.BlockSpec(perm.shape, lambda i: (0, 0)),
            pl.BlockSpec((gtile,) + me.shape[1:], lambda i: (i, 0, 0)),
            pl.BlockSpec((gtile,) + f.shape[1:], lambda i: (i, 0, 0)),
            pl.BlockSpec((gtile,) + a.shape[1:], lambda i: (i, 0, 0, 0)),
        ],
        out_specs=[
            pl.BlockSpec((t, rows, LANES), lambda i: (0, 0, i)),
            pl.BlockSpec((t, SSM_META_ROWS, LANES), lambda i: (0, 0, i)),
        ],
        out_shape=[
            jax.ShapeDtypeStruct((t, rows, width), BF16),
            jax.ShapeDtypeStruct((t, SSM_META_ROWS, width), BF16),
        ],
        scratch_shapes=[
            pltpu.VMEM((SSM_SUB, rows, tc), F32),
            pltpu.VMEM((SSM_SUB, rows, 4 * nstate), F32),
            pltpu.VMEM((SSM_SUB, rows, 4 * nstate), F32),
            pltpu.VMEM((SSM_SUB, rows, 4 * nstate), F32),
            pltpu.VMEM((t // (perm.shape[0] // LANES), rows, gtile * LANES), BF16),
            pltpu.VMEM((t // (perm.shape[0] // LANES), SSM_META_ROWS, gtile * LANES), BF16),
        ],
        compiler_params=_params("parallel"),
        name="s5_scan",
    )(u_main, u_meta, perm, me, f, a)

    y_meta = jnp.transpose(y_m[:, :batch], (1, 0, 2)).reshape(batch * N_META, width)
    return y, y_meta


def _mix_kernel(o_ref, y_ref, gna_ref, gssm_ref, wna_ref, wglu_ref, wssm_ref, out_ref, *stage, chunk_major):
    y_na = jnp.dot(o_ref[...], wna_ref[...], preferred_element_type=F32)
    if chunk_major:
        stage_ref, = stage
        t = y_ref.shape[0]
        ntile = stage_ref.shape[0]
        for k in range(t):
            for lt in range(ntile):
                stage_ref[lt, pl.ds(k, y_ref.shape[1], stride=t), :] = y_ref[
                    k, :, lt * LANES:(lt + 1) * LANES].astype(F32)
        y = jnp.concatenate([stage_ref[lt] for lt in range(ntile)], axis=1)
    else:
        y = y_ref[...].astype(F32)
    gl = _gelu(y)
    z = jnp.dot(gl.astype(BF16), wglu_ref[...], preferred_element_type=F32)
    t = (gl * _sigmoid(z)).astype(BF16)
    y_ssm = jnp.dot(t, wssm_ref[...], preferred_element_type=F32)
    mixed = _sigmoid(gna_ref[...].astype(F32)) * y_na + _sigmoid(gssm_ref[...].astype(F32)) * y_ssm
    out_ref[...] = mixed.astype(out_ref.dtype)


def mix_branches(o_na, y, proj, w_na, w_glu, w_ssm, gate_col):
    rows, na_width = o_na.shape
    ssm_width = y.shape[-1]
    d = w_na.shape[1]
    tm = _pick(rows, MID_ROW_TILES)
    assert gate_col % d == 0
    gblk = gate_col // d
    resident = dict(pipeline_mode=pl.Buffered(1))
    chunk_major = y.ndim == 3
    if chunk_major:
        t = y.shape[0]
        y_spec = pl.BlockSpec((t, tm // t, ssm_width), lambda i: (0, i, 0))
        scratch = [pltpu.VMEM((ssm_width // LANES, tm, LANES), F32)]
    else:
        y_spec = pl.BlockSpec((tm, ssm_width), lambda i: (i, 0))
        scratch = []
    return pl.pallas_call(
        functools.partial(_mix_kernel, chunk_major=chunk_major),
        grid=(rows // tm,),
        scratch_shapes=scratch,
        in_specs=[
            pl.BlockSpec((tm, na_width), lambda i: (i, 0)),
            y_spec,
            pl.BlockSpec((tm, d), lambda i: (i, gblk)),
            pl.BlockSpec((tm, d), lambda i: (i, gblk + 1)),
            pl.BlockSpec(w_na.shape, lambda i: (0, 0), **resident),
            pl.BlockSpec(w_glu.shape, lambda i: (0, 0), **resident),
            pl.BlockSpec(w_ssm.shape, lambda i: (0, 0), **resident),
        ],
        out_specs=pl.BlockSpec((tm, d), lambda i: (i, 0)),
        out_shape=jax.ShapeDtypeStruct((rows, d), BF16),
        compiler_params=_params("parallel"),
        name="mix_branches",
    )(o_na, y, proj, proj, w_na, w_glu, w_ssm)


def _residual_matmul_kernel(h_ref, a_ref, w_ref, g_ref, o_ref, n_ref):
    h1 = h_ref[...] + jnp.dot(a_ref[...], w_ref[...], preferred_element_type=F32)
    o_ref[...] = h1
    n_ref[...] = _rms(h1, g_ref[...]).astype(n_ref.dtype)


def residual_matmul(h, a, w, g):
    rows, d = h.shape
    tm = _pick(rows, MID_ROW_TILES)
    row_spec = pl.BlockSpec((tm, d), lambda i: (i, 0))
    return pl.pallas_call(
        _residual_matmul_kernel,
        grid=(rows // tm,),
        in_specs=[
            row_spec,
            pl.BlockSpec((tm, a.shape[1]), lambda i: (i, 0)),
            pl.BlockSpec(w.shape, lambda i: (0, 0), pipeline_mode=pl.Buffered(1)),
            pl.BlockSpec((1, d), lambda i: (0, 0)),
        ],
        out_specs=[row_spec, row_spec],
        out_shape=[jax.ShapeDtypeStruct((rows, d), F32), jax.ShapeDtypeStruct((rows, d), BF16)],
        compiler_params=_params("parallel"),
        name="residual_out_proj",
    )(h, a, w, g.reshape(1, d))


HALO = BF16_SUBLANES


def _ffn_up_kernel(h_ref, prev_ref, next_ref, wa_ref, wg_ref, cw_ref, cb_ref, o_ref, hn_ref, *, tm):
    @pl.when(pl.program_id(1) == 0)
    def _():
        hn_ref[0:HALO, :] = prev_ref[...]
        hn_ref[HALO:HALO + tm, :] = h_ref[...]
        hn_ref[HALO + tm:, :] = next_ref[...]

    ext = tm + 2 * HALO
    a = jnp.dot(hn_ref[...], wa_ref[...].astype(BF16), preferred_element_type=F32)
    gate = jnp.dot(hn_ref[HALO:HALO + tm, :], wg_ref[...].astype(BF16), preferred_element_type=F32)
    a_prev = pltpu.roll(a, 1, 0)[HALO:HALO + tm]
    a_next = pltpu.roll(a, ext - 1, 0)[HALO:HALO + tm]
    conv = a_prev * cw_ref[0:1, :] + a[HALO:HALO + tm] * cw_ref[1:2, :] + a_next * cw_ref[2:3, :] + cb_ref[...]
    o_ref[...] = (_gelu(conv) * gate).astype(o_ref.dtype)


def ffn_up(h, halo_prev, halo_next, w_up, conv_w, conv_b, tm):
    rows, d = h.shape
    dff = conv_b.shape[0]
    tn = _pick(dff, COL_TILES)
    nj = dff // tn
    kernel = functools.partial(_ffn_up_kernel, tm=tm)
    return pl.pallas_call(
        kernel,
        grid=(rows // tm, nj),
        in_specs=[
            pl.BlockSpec((tm, d), lambda i, j: (i, 0)),
            pl.BlockSpec((HALO, d), lambda i, j: (i, 0)),
            pl.BlockSpec((HALO, d), lambda i, j: (i, 0)),
            pl.BlockSpec((d, tn), lambda i, j: (0, j)),
            pl.BlockSpec((d, tn), lambda i, j: (0, nj + j)),
            pl.BlockSpec((conv_w.shape[0], tn), lambda i, j: (0, j)),
            pl.BlockSpec((1, tn), lambda i, j: (0, j)),
        ],
        out_specs=pl.BlockSpec((tm, tn), lambda i, j: (i, j)),
        out_shape=jax.ShapeDtypeStruct((rows, dff), BF16),
        scratch_shapes=[pltpu.VMEM((tm + 2 * HALO, d), BF16)],
        compiler_params=_params("parallel", "arbitrary"),
        name="ffn_up_conv_gate",
    )(h, halo_prev, halo_next, w_up, w_up, conv_w, conv_b.reshape(1, dff))


def _ffn_down_kernel(h_ref, a_ref, w_ref, g_ref, o_ref):
    k = pl.program_id(1)

    @pl.when(k == 0)
    def _():
        o_ref[...] = h_ref[...]

    o_ref[...] += jnp.dot(a_ref[...], w_ref[...], preferred_element_type=F32)

    @pl.when(k == pl.num_programs(1) - 1)
    def _():
        o_ref[...] = _rms(o_ref[...], g_ref[...])


def ffn_down(h, act, w_down, g):
    rows, d = h.shape
    dff = act.shape[1]
    tm = _pick(rows, ROW_TILES)
    tk = _pick(dff, COL_TILES)
    return pl.pallas_call(
        _ffn_down_kernel,
        grid=(rows // tm, dff // tk),
        in_specs=[
            pl.BlockSpec((tm, d), lambda i, k: (i, 0)),
            pl.BlockSpec((tm, tk), lambda i, k: (i, k)),
            pl.BlockSpec((tk, d), lambda i, k: (k, 0)),
            pl.BlockSpec((1, d), lambda i, k: (0, 0)),
        ],
        out_specs=pl.BlockSpec((tm, d), lambda i, k: (i, 0)),
        out_shape=jax.ShapeDtypeStruct((rows, d), F32),
        compiler_params=_params("parallel", "arbitrary"),
        name="ffn_down_final_norm",
    )(h, act, w_down, g.reshape(1, d))


def _conv_halos(hn_main, hn_meta, batch, seq, tm):
    d = hn_main.shape[1]
    per_seq = seq // tm
    hm = hn_main.reshape(batch, per_seq, tm, d)
    meta_tail = hn_meta.reshape(batch, 1, N_META, d)[:, :, N_META - HALO:]
    prev = jnp.concatenate([meta_tail, hm[:, :-1, tm - HALO:]], axis=1)
    nxt = jnp.concatenate([hm[:, 1:, :HALO], jnp.zeros((batch, 1, HALO, d), hn_main.dtype)], axis=1)
    return prev.reshape(batch * per_seq * HALO, d), nxt.reshape(batch * per_seq * HALO, d)


def kernel(x, meta_tokens, norm1_g, w_in, na_rpb, ssm_lam_re, ssm_lam_im, ssm_log_step, ssm_b_re, ssm_b_im,
           ssm_c_re, ssm_c_im, ssm_d, w_glu, w_proj_na, w_proj_ssm, w_out, norm2_g, w_up, conv_w, conv_b,
           w_down, final_g):
    batch, seq, d = x.shape
    depth = w_in.shape[0]
    na_width = w_proj_na.shape[1]
    ssm_width = w_proj_ssm.shape[1]
    assert depth == 1 and N_META >= HALO
    l = 0
    h_main = x.reshape(batch * seq, d)
    h_meta = jnp.broadcast_to(meta_tokens.astype(x.dtype)[None], (batch, N_META, d)).reshape(batch * N_META, d)

    w_in_b = w_in[l].astype(BF16)
    u_col = 3 * na_width
    proj_main, u_main = norm_matmul(h_main, norm1_g[l], w_in_b, chunk_major=(u_col, ssm_width))
    proj_meta = norm_matmul(h_meta, norm1_g[l], w_in_b)

    o_main, o_meta = neighbourhood_attention(proj_main, proj_meta, na_rpb[l], batch, seq, na_width)

    ops = _ssm_operators(ssm_lam_re[l], ssm_lam_im[l], ssm_log_step[l], ssm_b_re[l], ssm_b_im[l],
                         ssm_c_re[l], ssm_c_im[l], ssm_d[l])
    y_main, y_meta = s5_scan(u_main, proj_meta[:, u_col:u_col + ssm_width], ops, batch, seq)

    w_na_b, w_glu_b, w_ssm_b = w_proj_na[l].astype(BF16), w_glu[l].astype(BF16), w_proj_ssm[l].astype(BF16)
    w_out_b = w_out[l].astype(BF16)
    gate_col = u_col + ssm_width
    h1 = []
    for h, o, y, proj in ((h_main, o_main, y_main, proj_main), (h_meta, o_meta, y_meta, proj_meta)):
        mixed = mix_branches(o, y, proj, w_na_b, w_glu_b, w_ssm_b, gate_col)
        h1.append(residual_matmul(h, mixed, w_out_b, norm2_g[l]))
    (h1_main, hn_main), (_, hn_meta) = h1

    tm = _pick(seq, ROW_TILES)
    halo_prev, halo_next = _conv_halos(hn_main, hn_meta, batch, seq, tm)
    act = ffn_up(hn_main, halo_prev, halo_next, w_up[l], conv_w[l], conv_b[l], tm)
    out = ffn_down(h1_main, act, w_down[l].astype(BF16), final_g)
    return out.reshape(batch, seq, d)
```

```python
import functools
import math

import jax
import jax.numpy as jnp
from jax import lax
from jax.experimental import pallas as pl
from jax.experimental.pallas import tpu as pltpu

F32 = jnp.float32
BF16 = jnp.bfloat16

N_META = 16
GRID_W = 64
ROW_WIN = 8
COL_WIN = 16
RMS_EPS = 1e-6

VMEM_LIMIT_BYTES = 56 * 1024 * 1024
F32_SUBLANES = 8
BF16_SUBLANES = 16
LANES = 128

MASK_BIAS = -1e30
NA_QROWS = 4
NA_KROWS = 12
NA_UNROLL = 16

SSM_CHUNK = 16
SSM_SUB = 2
SSM_SCAN_UNROLL = True
SSM_META_ROWS = BF16_SUBLANES

ROW_TILES = (1024, 512, 256, 128, 64)
MID_ROW_TILES = (512, 256, 128, 64)
COL_TILES = (512, 256, 128)
WIDE_COL_TILES = (1024, 512, 256, 128)


def _pick(dim, prefs):
    for t in prefs:
        if dim % t == 0:
            return t
    return dim


def _params(*sem):
    return pltpu.CompilerParams(dimension_semantics=sem, vmem_limit_bytes=VMEM_LIMIT_BYTES)


def _rms(x, g):
    ms = jnp.mean(x * x, axis=-1, keepdims=True)
    return x * lax.rsqrt(ms + RMS_EPS) * g


def _gelu(x):
    c = math.sqrt(2.0 / math.pi)
    return 0.5 * x * (1.0 + jnp.tanh(c * (x + 0.044715 * (x * x * x))))


def _sigmoid(x):
    return 0.5 * jnp.tanh(0.5 * x) + 0.5


def _norm_matmul_kernel(x_ref, g_ref, w_ref, o_ref, *rest, chunk_cols):
    hn_ref = rest[-1] if chunk_cols is None else rest[1]

    @pl.when(pl.program_id(1) == 0)
    def _():
        hn_ref[...] = _rms(x_ref[...], g_ref[...]).astype(BF16)

    r = jnp.dot(hn_ref[...], w_ref[...], preferred_element_type=F32)
    o_ref[...] = r.astype(o_ref.dtype)

    if chunk_cols is not None:
        u3_ref, _, stage_ref = rest
        tile, off, width = chunk_cols
        t = SSM_CHUNK

        @pl.when(pl.program_id(1) == tile)
        def _():
            for lt in range(width // LANES):
                stage_ref[lt] = r[:, off + lt * LANES:off + (lt + 1) * LANES]
            for k in range(t):
                for lt in range(width // LANES):
                    u3_ref[k, :, lt * LANES:(lt + 1) * LANES] = stage_ref[
                        lt, pl.ds(k, stage_ref.shape[1] // t, stride=t), :].astype(u3_ref.dtype)


def norm_matmul(x, g, w, chunk_major=None):
    rows, d = x.shape
    n = w.shape[1]
    tm = _pick(rows, ROW_TILES)
    tn = _pick(n, WIDE_COL_TILES)
    out_specs = [pl.BlockSpec((tm, tn), lambda i, j: (i, j))]
    out_shape = [jax.ShapeDtypeStruct((rows, n), BF16)]
    scratch = [pltpu.VMEM((tm, d), BF16)]
    chunk_cols = None
    if chunk_major is not None:
        col, width = chunk_major
        t = SSM_CHUNK
        chunk_cols = (col // tn, col % tn, width)
        assert col % tn + width <= tn and tm % (t * F32_SUBLANES) == 0
        out_specs.append(pl.BlockSpec((t, tm // t, width), lambda i, j: (0, i, 0)))
        out_shape.append(jax.ShapeDtypeStruct((t, rows // t, width), BF16))
        scratch.append(pltpu.VMEM((width // LANES, tm, LANES), F32))
    out = pl.pallas_call(
        functools.partial(_norm_matmul_kernel, chunk_cols=chunk_cols),
        grid=(rows // tm, n // tn),
        in_specs=[
            pl.BlockSpec((tm, d), lambda i, j: (i, 0)),
            pl.BlockSpec((1, d), lambda i, j: (0, 0)),
            pl.BlockSpec((d, tn), lambda i, j: (0, j)),
        ],
        out_specs=out_specs,
        out_shape=out_shape,
        scratch_shapes=scratch,
        compiler_params=_params("parallel", "arbitrary"),
        name="norm_in_proj",
    )(x, g.reshape(1, d), w)
    return out[0] if chunk_major is None else out


def _na_kernel(q_ref, k_ref, v_ref, qm_ref, km_ref, vm_ref, tab_ref, o_ref, om_ref, bias_ref, sm_ref, acc_ref,
               *, rows, dh):
    lane = lax.broadcasted_iota(jnp.int32, (1, 2 * dh), 1)
    head_masks = (lane < dh, lane >= dh)
    scale = dh ** -0.5
    km = km_ref[...]
    vm = vm_ref[...]
    nt = (((1,), (1,)), ((), ()))

    masked = 2 * ROW_WIN - 1
    nblk = rows // NA_QROWS
    qblk = NA_QROWS * GRID_W
    kblk = NA_KROWS * GRID_W

    def rel_row(kind, j, i):
        if kind == 0:
            return i - j + (ROW_WIN - 1) if i < ROW_WIN else masked
        if kind == 1:
            return i - j + (ROW_WIN - 1 - ROW_WIN // 2) if j <= i < j + ROW_WIN else masked
        lo = NA_KROWS - ROW_WIN
        return i - j + (NA_QROWS - NA_KROWS) + (ROW_WIN - 1) if i >= lo else masked

    first_row = lane < GRID_W
    for kind in range(3):
        for hh in range(2):
            for j in range(NA_QROWS):
                r_lo = (hh * NA_QROWS + j) * GRID_W
                for i2 in range(NA_KROWS // 2):
                    bias_ref[kind, r_lo:r_lo + GRID_W, i2 * 2 * GRID_W:(i2 + 1) * 2 * GRID_W] = jnp.where(
                        first_row, tab_ref[hh, rel_row(kind, j, 2 * i2)], tab_ref[hh, rel_row(kind, j, 2 * i2 + 1)])

    def one_head(x, hh):
        return jnp.where(head_masks[hh], x, jnp.zeros_like(x))

    qm = (qm_ref[...] * scale).astype(BF16)
    om = None
    for hh in range(2):
        s = lax.dot_general(one_head(qm, hh), km, nt, preferred_element_type=F32)
        p = jnp.exp(s - jnp.max(s, axis=-1, keepdims=True))
        o_h = jnp.dot(p.astype(BF16), vm, preferred_element_type=F32) / jnp.sum(p, axis=-1, keepdims=True)
        om = o_h if om is None else jnp.where(head_masks[1], o_h, om)
    om_ref[...] = om.astype(om_ref.dtype)

    q_all = (q_ref[...] * scale).astype(BF16)
    for hh in range(2):
        sm_ref[hh] = lax.dot_general(one_head(q_all, hh), km, nt, preferred_element_type=F32)

    def body(blk, carry):
        r = blk * NA_QROWS
        k0 = jnp.clip(r - ROW_WIN // 2, 0, rows - NA_KROWS)
        kind = jnp.where(blk == 0, 0, jnp.where(blk == nblk - 1, 2, 1))
        qrows = pl.ds(pl.multiple_of(r * GRID_W, qblk), qblk)
        krows = pl.ds(pl.multiple_of(k0 * GRID_W, GRID_W), kblk)
        qs = (q_ref[qrows, :] * scale).astype(BF16)
        q2 = jnp.concatenate([one_head(qs, 0), one_head(qs, 1)], axis=0)
        s = lax.dot_general(q2, k_ref[krows, :], nt, preferred_element_type=F32) + bias_ref[kind]
        s_m = jnp.concatenate([sm_ref[0, qrows, :], sm_ref[1, qrows, :]], axis=0)
        m = jnp.maximum(jnp.max(s, axis=-1, keepdims=True), jnp.max(s_m, axis=-1, keepdims=True))
        p = jnp.exp(s - m)
        p_m = jnp.exp(s_m - m)
        inv = 1.0 / (jnp.sum(p, axis=-1, keepdims=True) + jnp.sum(p_m, axis=-1, keepdims=True))
        acc = jnp.dot(p.astype(BF16), v_ref[krows, :], preferred_element_type=F32) * inv
        acc_ref[qrows, :] = jnp.where(head_masks[1], acc[qblk:], acc[:qblk])
        p_m = p_m * inv
        sm_ref[0, qrows, :] = p_m[:qblk]
        sm_ref[1, qrows, :] = p_m[qblk:]
        return carry

    lax.fori_loop(0, nblk, body, 0, unroll=NA_UNROLL)

    o_meta = [jnp.dot(sm_ref[hh].astype(BF16), vm, preferred_element_type=F32) for hh in range(2)]
    o_ref[...] = (acc_ref[...] + jnp.where(head_masks[1], o_meta[1], o_meta[0])).astype(o_ref.dtype)


def _na_bias_table(rpb):
    c = jnp.arange(GRID_W)[:, None]
    kc = jnp.arange(2 * GRID_W)[None, :] % GRID_W
    col_start = jnp.clip(c - COL_WIN // 2, 0, GRID_W - COL_WIN)
    valid = (kc >= col_start) & (kc < col_start + COL_WIN)
    dc = kc - c + (COL_WIN - 1)
    pick = (dc.reshape(-1)[None] == jnp.arange(2 * COL_WIN - 1)[:, None]).astype(F32)
    tab = jnp.dot(rpb.astype(F32).reshape(-1, 2 * COL_WIN - 1), pick, precision=lax.Precision.HIGHEST)
    tab = tab.reshape(rpb.shape[0], 2 * ROW_WIN - 1, GRID_W, 2 * GRID_W)
    tab = jnp.where(valid, tab, MASK_BIAS)
    return jnp.concatenate([tab, jnp.full_like(tab[:, :1], MASK_BIAS)], axis=1)


def neighbourhood_attention(proj_main, proj_meta, rpb, batch, seq, na_width):
    heads = rpb.shape[0]
    dh = na_width // heads
    rows = seq // GRID_W
    assert rows >= NA_KROWS and rows % NA_QROWS == 0 and heads % 2 == 0 and 2 * dh == LANES
    assert NA_KROWS % 2 == 0 and NA_KROWS >= ROW_WIN + NA_QROWS - 1 and NA_QROWS <= ROW_WIN // 2
    pairs = heads // 2
    bias = _na_bias_table(rpb)
    blk = (seq, 2 * dh)
    mblk = (N_META, 2 * dh)
    kernel = functools.partial(_na_kernel, rows=rows, dh=dh)
    return pl.pallas_call(
        kernel,
        grid=(batch, pairs),
        in_specs=[
            pl.BlockSpec(blk, lambda b, h: (b, h)),
            pl.BlockSpec(blk, lambda b, h: (b, pairs + h)),
            pl.BlockSpec(blk, lambda b, h: (b, 2 * pairs + h)),
            pl.BlockSpec(mblk, lambda b, h: (b, h)),
            pl.BlockSpec(mblk, lambda b, h: (b, pairs + h)),
            pl.BlockSpec(mblk, lambda b, h: (b, 2 * pairs + h)),
            pl.BlockSpec((2, 2 * ROW_WIN, GRID_W, 2 * GRID_W), lambda b, h: (h, 0, 0, 0)),
        ],
        out_specs=[
            pl.BlockSpec(blk, lambda b, h: (b, h)),
            pl.BlockSpec(mblk, lambda b, h: (b, h)),
        ],
        out_shape=[
            jax.ShapeDtypeStruct((batch * seq, na_width), BF16),
            jax.ShapeDtypeStruct((batch * N_META, na_width), BF16),
        ],
        scratch_shapes=[pltpu.VMEM((3, 2 * NA_QROWS * GRID_W, NA_KROWS * GRID_W), F32),
                        pltpu.VMEM((2, seq, N_META), F32),
                        pltpu.VMEM((seq, 2 * dh), F32)],
        compiler_params=_params("parallel", "parallel"),
        name="neighbourhood_attention",
    )(proj_main, proj_main, proj_main, proj_meta, proj_meta, proj_meta, bias)


def _chunk_ops_kernel(cp_ref, d_ref, shift_ref, pw_ref, bt_ref, me_ref, f_ref):
    gt, _, n, packed = cp_ref.shape
    c = bt_ref.shape[2]
    t, _, tc = shift_ref.shape
    w = pw_ref.shape[3]
    q = lax.broadcasted_iota(jnp.int32, (packed, tc), 1)
    p_lane = lax.broadcasted_iota(jnp.int32, (packed, tc), 0)
    k_row = lax.broadcasted_iota(jnp.int32, (c, tc), 0)

    def spread(parts, k, by_position):
        mat = (p_lane - k * c == (q // c if by_position else q % c)).astype(BF16)
        return sum(jnp.dot(x, mat, preferred_element_type=F32) for x in parts)

    lag_rows = [[None, None] for _ in range(gt)]
    for d in range(2):
        x = cp_ref[:, d].reshape(gt * n, packed)
        hi = x.astype(BF16)
        parts = (hi, (x - hi.astype(F32)).astype(BF16))
        cx_r, cx_i = spread(parts, 0, False), spread(parts, 1, False)
        e_r, e_i = spread(parts, 4, True), spread(parts, 5, True)
        f_ref[:, d * n:(d + 1) * n, :] = (cx_r * e_r - cx_i * e_i).reshape(gt, n, tc).astype(f_ref.dtype)
        f_ref[:, (2 + d) * n:(3 + d) * n, :] = (-(cx_r * e_i + cx_i * e_r)).reshape(gt, n, tc).astype(f_ref.dtype)
        e_r, e_i = spread(parts, 2, True), spread(parts, 3, True)
        z_r, z_i = cx_r * e_r - cx_i * e_i, cx_r * e_i + cx_i * e_r
        for g in range(gt):
            zr_g, zi_g = z_r[g * n:(g + 1) * n], z_i[g * n:(g + 1) * n]
            x_g = cp_ref[g, d]
            rows = jnp.zeros((c, tc), F32)
            for ci in range(c):
                b_r, b_i = x_g[:, 6 * c + ci:6 * c + ci + 1], x_g[:, 7 * c + ci:7 * c + ci + 1]
                row = jnp.sum(b_r * zr_g - b_i * zi_g, axis=0, keepdims=True)
                rows = jnp.where(k_row == ci, row, rows)
            lag_rows[g][d] = rows
    lag = jnp.concatenate([jnp.concatenate(lag_rows[g], axis=1) for g in range(gt)], axis=0).astype(BF16)

    lane = lax.broadcasted_iota(jnp.int32, (gt * c, tc), 1)
    c_in = lax.broadcasted_iota(jnp.int32, (gt * c, tc), 0) % c
    skip_gain = d_ref[...].reshape(gt * c, tc)
    b_r, b_i = bt_ref[:, 0], bt_ref[:, 1]
    for s in range(t):
        rows = slice(s * c, (s + 1) * c)
        blk = jnp.dot(lag, shift_ref[s], preferred_element_type=F32)
        blk = blk + jnp.where(lane == s * c + c_in, skip_gain, 0.0)
        me_ref[:, rows, 0:tc] = blk.reshape(gt, c, tc).astype(me_ref.dtype)
        e_r, e_i = pw_ref[:, 0, s:s + 1, :], pw_ref[:, 1, s:s + 1, :]
        me_ref[:, rows, tc:tc + w] = (e_r * b_r - e_i * b_i).astype(me_ref.dtype)
        me_ref[:, rows, tc + w:tc + 2 * w] = (e_r * b_i + e_i * b_r).astype(me_ref.dtype)


def _chunk_operators(cp, d_rows, pw, bt):
    g, _, n, packed = cp.shape
    t, c = pw.shape[2], bt.shape[2]
    tc = t * c
    w = pw.shape[-1]
    assert t == c and packed == 8 * c
    gt = _pick(g, (8, 4, 2, 1))
    p = jnp.arange(2 * tc)[None, :, None]
    q = jnp.arange(tc)[None, None, :]
    s = jnp.arange(t)[:, None, None]
    shift = jnp.where(p < tc, p == q - s * c, p - tc == q + (t - 1 - s) * c).astype(BF16)

    def per_group(shape):
        return pl.BlockSpec((gt,) + tuple(shape), lambda i: (i,) + (0,) * len(shape))

    return pl.pallas_call(
        _chunk_ops_kernel,
        grid=(g // gt,),
        in_specs=[
            per_group(cp.shape[1:]), per_group(d_rows.shape[1:]),
            pl.BlockSpec(shift.shape, lambda i: (0, 0, 0)),
            per_group(pw.shape[1:]), per_group(bt.shape[1:]),
        ],
        out_specs=[per_group((tc, tc + 2 * w)), per_group((4 * n, tc))],
        out_shape=[jax.ShapeDtypeStruct((g, tc, tc + 2 * w), BF16), jax.ShapeDtypeStruct((g, 4 * n, tc), BF16)],
        compiler_params=_params("parallel"),
        name="s5_chunk_operators",
    )(cp, d_rows, shift, pw, bt)


def _ssm_operators(lam_re, lam_im, log_step, b_re, b_im, c_re, c_im, d_skip):
    t = SSM_CHUNK
    lr, li = lam_re.astype(F32), lam_im.astype(F32)
    step = jnp.exp(log_step.astype(F32))[..., None]
    dt_r, dt_i = lr * step, li * step
    g, n = lr.shape[1], lr.shape[2]
    c = b_re.shape[-1]

    def lam_bar_pow(k):
        mag = jnp.exp(dt_r[:, :, None] * k)
        return mag * jnp.cos(dt_i[:, :, None] * k), mag * jnp.sin(dt_i[:, :, None] * k)

    p_r, p_i = lam_bar_pow(jnp.arange(t + 1, dtype=F32)[:, None])
    x_r, x_i = p_r[:, :, 1] - 1.0, p_i[:, :, 1]
    den = lr * lr + li * li
    q_r, q_i = (x_r * lr + x_i * li) / den, (x_i * lr - x_r * li) / den
    bt_r, bt_i = jnp.swapaxes(b_re.astype(F32), 2, 3), jnp.swapaxes(b_im.astype(F32), 2, 3)
    bb_r = q_r[:, :, None] * bt_r - q_i[:, :, None] * bt_i
    bb_i = q_r[:, :, None] * bt_i + q_i[:, :, None] * bt_r
    bn_r = q_r[..., None] * b_re.astype(F32) - q_i[..., None] * b_im.astype(F32)
    bn_i = q_r[..., None] * b_im.astype(F32) + q_i[..., None] * b_re.astype(F32)
    ct_r, ct_i = jnp.swapaxes(c_re.astype(F32), 2, 3), jnp.swapaxes(c_im.astype(F32), 2, 3)

    def cols(x):
        return jnp.swapaxes(x, 1, 2)

    factors = []
    for d, lag_sel, out_sel in ((0, lambda x: x[:, :t], lambda x: x[:, 1:]),
                                (1, lambda x: x[:, :t][:, ::-1], lambda x: x[:, ::-1][:, :t])):
        factors.append(jnp.concatenate([ct_r[d], ct_i[d], cols(lag_sel(p_r[d])), cols(lag_sel(p_i[d])),
                                        cols(out_sel(p_r[d])), cols(out_sel(p_i[d])), bn_r[d], bn_i[d]], axis=-1))
    cp = jnp.stack(factors, axis=1)
    d_rows = jnp.broadcast_to(d_skip.astype(F32).reshape(g, c, 1), (g, c, t * c))

    pw = jnp.stack([jnp.concatenate([p_r[0][:, ::-1][:, 1:], p_r[1][:, :t]], axis=-1),
                    jnp.concatenate([p_i[0][:, ::-1][:, 1:], p_i[1][:, :t]], axis=-1)], axis=1)
    bt = jnp.stack([jnp.concatenate([bb_r[0], bb_r[1]], axis=-1),
                    jnp.concatenate([bb_i[0], bb_i[1]], axis=-1)], axis=1)
    me, f = _chunk_operators(cp, d_rows, pw, bt)

    sub = F32_SUBLANES
    row = jnp.arange(sub)

    d_r, d_i = lam_bar_pow((t * (row + 1)).astype(F32)[:, None])
    d_r = jnp.concatenate([d_r[0], d_r[1]], axis=-1)
    d_i = jnp.concatenate([d_i[0], d_i[1]], axis=-1)

    def fixed(x, k, keep):
        return jnp.where(keep[None, :, None], x[:, k - 1:k], 0.0)

    slots = []
    for k in (1, 2, 4):
        slots += [fixed(d_r, k, row >= k), fixed(d_i, k, row >= k)]
    slots += [d_r, d_i]
    for k in (1, 2, 4):
        slots += [fixed(d_r, k, row < sub - k), fixed(d_i, k, row < sub - k)]
    slots += [d_r[:, ::-1], d_i[:, ::-1]]
    a = jnp.concatenate(slots, axis=1).reshape(g, len(slots), sub, 2 * n)

    toks = LANES // c
    grp = LANES // c
    src = jnp.transpose(jnp.arange(toks * grp * c).reshape(toks, grp, c), (1, 0, 2)).reshape(-1)
    perm = (jnp.arange(toks * grp * c)[:, None] == src[None, :]).astype(BF16)
    return perm, me.astype(BF16), f.astype(BF16), a.astype(F32)


def _ssm_kernel(*refs, batch, nchunks, nstate, gsub):
    t = SSM_CHUNK
    (u_ref, um_ref, p_ref, me_ref, f_ref, a_ref, y_ref, ym_ref,
     yi_ref, ee_ref, pf_ref, pr_ref, ycat_ref, ymcat_ref) = refs
    gtile = me_ref.shape[0]
    tc = f_ref.shape[2]
    toks = p_ref.shape[0] // LANES
    parts = t // toks
    sub = F32_SUBLANES
    w = 2 * nstate
    tiles = nchunks // sub
    perm = p_ref[...]
    row = lax.broadcasted_iota(jnp.int32, (sub, w), 0)
    lane = lax.broadcasted_iota(jnp.int32, (1, w), 1)
    rev_lane = lane >= nstate

    def group_major(token_refs):
        out = []
        for part in range(parts):
            x = jnp.concatenate([token_refs[part * toks + i] for i in range(toks)], axis=1)
            out.append(jnp.dot(x, perm, preferred_element_type=F32).astype(BF16))
        return out

    def group_chunk(regrouped, g):
        return jnp.concatenate([x[:, g * LANES:(g + 1) * LANES] for x in regrouped], axis=1)

    v_main = group_major(u_ref)
    v_meta = group_major(um_ref)

    def cmul_add(xr, xi, ar, ai, yr, yi):
        return xr + ar * yr - ai * yi, xi + ar * yi + ai * yr

    def bcast_row(x, r):
        return jnp.broadcast_to(x[r:r + 1, :], (sub, w))

    for sb in range(gtile // gsub):
        gs = [sb * gsub + gi for gi in range(gsub)]
        meta = []
        for gi, g in enumerate(gs):
            me = jnp.dot(group_chunk(v_main, g), me_ref[g], preferred_element_type=F32)
            yi_ref[gi] = me[:, :tc]
            ee_ref[gi] = me[:, tc:]
            meta.append(jnp.dot(group_chunk(v_meta, g), me_ref[g], preferred_element_type=F32))

        def body(j, carry, gs=gs):
            new = []
            for gi, g in enumerate(gs):
                for b in range(batch):
                    lr, li, fr, fi = carry[gi * batch + b]
                    fs = pl.ds(pl.multiple_of(b * nchunks + j * sub, sub), sub)
                    xr, xi = ee_ref[gi, fs, 0:w], ee_ref[gi, fs, w:2 * w]
                    for step, k in enumerate((1, 2, 4)):
                        xr, xi = cmul_add(xr, xi, a_ref[g, 2 * step], a_ref[g, 2 * step + 1],
                                          pltpu.roll(xr, k, 0), pltpu.roll(xi, k, 0))
                    sr, si = cmul_add(xr, xi, a_ref[g, 6], a_ref[g, 7], lr, li)
                    pf_ref[gi, fs, 0:w] = jnp.where(row == 0, lr, pltpu.roll(sr, 1, 0))
                    pf_ref[gi, fs, w:2 * w] = jnp.where(row == 0, li, pltpu.roll(si, 1, 0))
                    rs = pl.ds(pl.multiple_of(b * nchunks + (tiles - 1 - j) * sub, sub), sub)
                    xr, xi = ee_ref[gi, rs, 0:w], ee_ref[gi, rs, w:2 * w]
                    for step, k in enumerate((1, 2, 4)):
                        xr, xi = cmul_add(xr, xi, a_ref[g, 8 + 2 * step], a_ref[g, 9 + 2 * step],
                                          pltpu.roll(xr, sub - k, 0), pltpu.roll(xi, sub - k, 0))
                    rr, ri = cmul_add(xr, xi, a_ref[g, 14], a_ref[g, 15], fr, fi)
                    pr_ref[gi, rs, 0:w] = jnp.where(row == sub - 1, fr, pltpu.roll(rr, sub - 1, 0))
                    pr_ref[gi, rs, w:2 * w] = jnp.where(row == sub - 1, fi, pltpu.roll(ri, sub - 1, 0))
                    new.append((bcast_row(sr, sub - 1), bcast_row(si, sub - 1), bcast_row(rr, 0), bcast_row(ri, 0)))
            return tuple(new)

        zero = jnp.zeros((sub, w), F32)
        init = tuple((bcast_row(meta[gi][:, tc:tc + w], b), bcast_row(meta[gi][:, tc + w:tc + 2 * w], b), zero, zero)
                     for gi in range(gsub) for b in range(batch))
        final = lax.fori_loop(0, tiles, body, init, unroll=SSM_SCAN_UNROLL)

        for gi, g in enumerate(gs):
            prev = jnp.concatenate([jnp.where(rev_lane, pr_ref[gi, :, k * w:(k + 1) * w], pf_ref[gi, :, k * w:(k + 1) * w])
                                    for k in range(2)], axis=1).astype(BF16)
            y = (yi_ref[gi] + jnp.dot(prev, f_ref[g], preferred_element_type=F32)).astype(BF16)
            pm = []
            for k in (2, 3):
                x = zero
                for b in range(batch):
                    x = jnp.where((row == b) & rev_lane, final[gi * batch + b][k], x)
                pm.append(jnp.concatenate([x, jnp.zeros((SSM_META_ROWS - sub, w), F32)], axis=0))
            prev_m = jnp.concatenate(pm, axis=1).astype(BF16)
            y_m = (meta[gi][:, :tc] + jnp.dot(prev_m, f_ref[g], preferred_element_type=F32)).astype(BF16)
            for part in range(parts):
                ycat_ref[part, :, g * LANES:(g + 1) * LANES] = y[:, part * LANES:(part + 1) * LANES]
                ymcat_ref[part, :, g * LANES:(g + 1) * LANES] = y_m[:, part * LANES:(part + 1) * LANES]

    for part in range(parts):
        o = jnp.dot(ycat_ref[part], perm, preferred_element_type=F32).astype(y_ref.dtype)
        o_m = jnp.dot(ymcat_ref[part], perm, preferred_element_type=F32).astype(ym_ref.dtype)
        for i in range(toks):
            y_ref[part * toks + i] = o[:, i * LANES:(i + 1) * LANES]
            ym_ref[part * toks + i] = o_m[:, i * LANES:(i + 1) * LANES]


def s5_scan(u_main, u_meta_rows, ops, batch, seq):
    perm, me, f, a = ops
    g = me.shape[0]
    t = SSM_CHUNK
    tc = f.shape[2]
    c = tc // t
    width = g * c
    nstate = f.shape[1] // 4
    gtile = LANES // c
    nchunks = seq // t
    rows = batch * nchunks
    assert N_META == t and seq % (t * F32_SUBLANES) == 0 and batch <= F32_SUBLANES
    assert g % gtile == 0 and gtile % SSM_SUB == 0 and u_main.shape == (t, rows, width)

    u_meta = jnp.transpose(u_meta_rows.reshape(batch, t, width), (1, 0, 2))
    u_meta = jnp.pad(u_meta, ((0, 0), (0, SSM_META_ROWS - batch), (0, 0)))
    kernel = functools.partial(_ssm_kernel, batch=batch, nchunks=nchunks, nstate=nstate, gsub=SSM_SUB)
    y, y_m = pl.pallas_call(
        kernel,
        grid=(g // gtile,),
        in_specs=[
            pl.BlockSpec((t, rows, LANES), lambda i: (0, 0, i)),
            pl.BlockSpec((t, SSM_META_ROWS, LANES), lambda i: (0, 0, i)),
            pl.BlockSpec(perm.shape, lambda i: (0, 0)),
            pl.BlockSpec((gtile,) + me.shape[1:], lambda i: (i, 0, 0)),
            pl.BlockSpec((gtile,) + f.shape[1:], lambda i: (i, 0, 0)),
            pl.BlockSpec((gtile,) + a.shape[1:], lambda i: (i, 0, 0, 0)),
        ],
        out_specs=[
            pl.BlockSpec((t, rows, LANES), lambda i: (0, 0, i)),
            pl.BlockSpec((t, SSM_META_ROWS, LANES), lambda i: (0, 0, i)),
        ],
        out_shape=[
            jax.ShapeDtypeStruct((t, rows, width), BF16),
            jax.ShapeDtypeStruct((t, SSM_META_ROWS, width), BF16),
        ],
        scratch_shapes=[
            pltpu.VMEM((SSM_SUB, rows, tc), F32),
            pltpu.VMEM((SSM_SUB, rows, 4 * nstate), F32),
            pltpu.VMEM((SSM_SUB, rows, 4 * nstate), F32),
            pltpu.VMEM((SSM_SUB, rows, 4 * nstate), F32),
            pltpu.VMEM((t // (perm.shape[0] // LANES), rows, gtile * LANES), BF16),
            pltpu.VMEM((t // (perm.shape[0] // LANES), SSM_META_ROWS, gtile * LANES), BF16),
        ],
        compiler_params=_params("parallel"),
        name="s5_scan",
    )(u_main, u_meta, perm, me, f, a)

    y_meta = jnp.transpose(y_m[:, :batch], (1, 0, 2)).reshape(batch * N_META, width)
    return y, y_meta


def _mix_kernel(o_ref, y_ref, gna_ref, gssm_ref, wna_ref, wglu_ref, wssm_ref, *rest, chunk_major):
    out_ref = rest[-1]
    y_na = jnp.dot(o_ref[...], wna_ref[...], preferred_element_type=F32)
    if chunk_major:
        rowperm_ref = rest[0]
        t, nch, width = y_ref.shape
        y = jnp.dot(rowperm_ref[...], y_ref[...].reshape(t * nch, width), preferred_element_type=F32)
    else:
        y = y_ref[...].astype(F32)
    gl = _gelu(y)
    z = jnp.dot(gl.astype(BF16), wglu_ref[...], preferred_element_type=F32)
    t = (gl * _sigmoid(z)).astype(BF16)
    y_ssm = jnp.dot(t, wssm_ref[...], preferred_element_type=F32)
    mixed = _sigmoid(gna_ref[...].astype(F32)) * y_na + _sigmoid(gssm_ref[...].astype(F32)) * y_ssm
    out_ref[...] = mixed.astype(out_ref.dtype)


def mix_branches(o_na, y, proj, w_na, w_glu, w_ssm, gate_col):
    rows, na_width = o_na.shape
    ssm_width = y.shape[-1]
    d = w_na.shape[1]
    tm = _pick(rows, MID_ROW_TILES)
    assert gate_col % d == 0
    gblk = gate_col // d
    resident = dict(pipeline_mode=pl.Buffered(1))
    chunk_major = y.ndim == 3
    extra, extra_specs = [], []
    if chunk_major:
        t = y.shape[0]
        y_spec = pl.BlockSpec((t, tm // t, ssm_width), lambda i: (0, i, 0))
        tok = jnp.arange(tm)
        src = (tok % t) * (tm // t) + tok // t
        extra = [(src[:, None] == jnp.arange(tm)[None, :]).astype(BF16)]
        extra_specs = [pl.BlockSpec((tm, tm), lambda i: (0, 0), **resident)]
    else:
        y_spec = pl.BlockSpec((tm, ssm_width), lambda i: (i, 0))
    return pl.pallas_call(
        functools.partial(_mix_kernel, chunk_major=chunk_major),
        grid=(rows // tm,),
        in_specs=[
            pl.BlockSpec((tm, na_width), lambda i: (i, 0)),
            y_spec,
            pl.BlockSpec((tm, d), lambda i: (i, gblk)),
            pl.BlockSpec((tm, d), lambda i: (i, gblk + 1)),
            pl.BlockSpec(w_na.shape, lambda i: (0, 0), **resident),
            pl.BlockSpec(w_glu.shape, lambda i: (0, 0), **resident),
            pl.BlockSpec(w_ssm.shape, lambda i: (0, 0), **resident),
        ] + extra_specs,
        out_specs=pl.BlockSpec((tm, d), lambda i: (i, 0)),
        out_shape=jax.ShapeDtypeStruct((rows, d), BF16),
        compiler_params=_params("parallel"),
        name="mix_branches",
    )(o_na, y, proj, proj, w_na, w_glu, w_ssm, *extra)


def _residual_matmul_kernel(h_ref, a_ref, w_ref, g_ref, o_ref, n_ref):
    h1 = h_ref[...] + jnp.dot(a_ref[...], w_ref[...], preferred_element_type=F32)
    o_ref[...] = h1
    n_ref[...] = _rms(h1, g_ref[...]).astype(n_ref.dtype)


def residual_matmul(h, a, w, g):
    rows, d = h.shape
    tm = _pick(rows, MID_ROW_TILES)
    row_spec = pl.BlockSpec((tm, d), lambda i: (i, 0))
    return pl.pallas_call(
        _residual_matmul_kernel,
        grid=(rows // tm,),
        in_specs=[
            row_spec,
            pl.BlockSpec((tm, a.shape[1]), lambda i: (i, 0)),
            pl.BlockSpec(w.shape, lambda i: (0, 0), pipeline_mode=pl.Buffered(1)),
            pl.BlockSpec((1, d), lambda i: (0, 0)),
        ],
        out_specs=[row_spec, row_spec],
        out_shape=[jax.ShapeDtypeStruct((rows, d), F32), jax.ShapeDtypeStruct((rows, d), BF16)],
        compiler_params=_params("parallel"),
        name="residual_out_proj",
    )(h, a, w, g.reshape(1, d))


HALO = BF16_SUBLANES


def _ffn_up_kernel(h_ref, prev_ref, next_ref, wa_ref, wg_ref, cw_ref, cb_ref, o_ref, hn_ref, *, tm):
    @pl.when(pl.program_id(1) == 0)
    def _():
        hn_ref[0:HALO, :] = prev_ref[...]
        hn_ref[HALO:HALO + tm, :] = h_ref[...]
        hn_ref[HALO + tm:, :] = next_ref[...]

    ext = tm + 2 * HALO
    a = jnp.dot(hn_ref[...], wa_ref[...].astype(BF16), preferred_element_type=F32)
    gate = jnp.dot(hn_ref[HALO:HALO + tm, :], wg_ref[...].astype(BF16), preferred_element_type=F32)
    a_prev = pltpu.roll(a, 1, 0)[HALO:HALO + tm]
    a_next = pltpu.roll(a, ext - 1, 0)[HALO:HALO + tm]
    conv = a_prev * cw_ref[0:1, :] + a[HALO:HALO + tm] * cw_ref[1:2, :] + a_next * cw_ref[2:3, :] + cb_ref[...]
    o_ref[...] = (_gelu(conv) * gate).astype(o_ref.dtype)


def ffn_up(h, halo_prev, halo_next, w_up, conv_w, conv_b, tm):
    rows, d = h.shape
    dff = conv_b.shape[0]
    tn = _pick(dff, COL_TILES)
    nj = dff // tn
    kernel = functools.partial(_ffn_up_kernel, tm=tm)
    return pl.pallas_call(
        kernel,
        grid=(rows // tm, nj),
        in_specs=[
            pl.BlockSpec((tm, d), lambda i, j: (i, 0)),
            pl.BlockSpec((HALO, d), lambda i, j: (i, 0)),
            pl.BlockSpec((HALO, d), lambda i, j: (i, 0)),
            pl.BlockSpec((d, tn), lambda i, j: (0, j)),
            pl.BlockSpec((d, tn), lambda i, j: (0, nj + j)),
            pl.BlockSpec((conv_w.shape[0], tn), lambda i, j: (0, j)),
            pl.BlockSpec((1, tn), lambda i, j: (0, j)),
        ],
        out_specs=pl.BlockSpec((tm, tn), lambda i, j: (i, j)),
        out_shape=jax.ShapeDtypeStruct((rows, dff), BF16),
        scratch_shapes=[pltpu.VMEM((tm + 2 * HALO, d), BF16)],
        compiler_params=_params("parallel", "arbitrary"),
        name="ffn_up_conv_gate",
    )(h, halo_prev, halo_next, w_up, w_up, conv_w, conv_b.reshape(1, dff))


def _ffn_down_kernel(h_ref, a_ref, w_ref, g_ref, o_ref):
    k = pl.program_id(1)

    @pl.when(k == 0)
    def _():
        o_ref[...] = h_ref[...]

    o_ref[...] += jnp.dot(a_ref[...], w_ref[...], preferred_element_type=F32)

    @pl.when(k == pl.num_programs(1) - 1)
    def _():
        o_ref[...] = _rms(o_ref[...], g_ref[...])


def ffn_down(h, act, w_down, g):
    rows, d = h.shape
    dff = act.shape[1]
    tm = _pick(rows, ROW_TILES)
    tk = _pick(dff, COL_TILES)
    return pl.pallas_call(
        _ffn_down_kernel,
        grid=(rows // tm, dff // tk),
        in_specs=[
            pl.BlockSpec((tm, d), lambda i, k: (i, 0)),
            pl.BlockSpec((tm, tk), lambda i, k: (i, k)),
            pl.BlockSpec((tk, d), lambda i, k: (k, 0)),
            pl.BlockSpec((1, d), lambda i, k: (0, 0)),
        ],
        out_specs=pl.BlockSpec((tm, d), lambda i, k: (i, 0)),
        out_shape=jax.ShapeDtypeStruct((rows, d), F32),
        compiler_params=_params("parallel", "arbitrary"),
        name="ffn_down_final_norm",
    )(h, act, w_down, g.reshape(1, d))


def _conv_halos(hn_main, hn_meta, batch, seq, tm):
    d = hn_main.shape[1]
    per_seq = seq // tm
    hm = hn_main.reshape(batch, per_seq, tm, d)
    meta_tail = hn_meta.reshape(batch, 1, N_META, d)[:, :, N_META - HALO:]
    prev = jnp.concatenate([meta_tail, hm[:, :-1, tm - HALO:]], axis=1)
    nxt = jnp.concatenate([hm[:, 1:, :HALO], jnp.zeros((batch, 1, HALO, d), hn_main.dtype)], axis=1)
    return prev.reshape(batch * per_seq * HALO, d), nxt.reshape(batch * per_seq * HALO, d)


def kernel(x, meta_tokens, norm1_g, w_in, na_rpb, ssm_lam_re, ssm_lam_im, ssm_log_step, ssm_b_re, ssm_b_im,
           ssm_c_re, ssm_c_im, ssm_d, w_glu, w_proj_na, w_proj_ssm, w_out, norm2_g, w_up, conv_w, conv_b,
           w_down, final_g):
    batch, seq, d = x.shape
    depth = w_in.shape[0]
    na_width = w_proj_na.shape[1]
    ssm_width = w_proj_ssm.shape[1]
    assert depth == 1 and N_META >= HALO
    l = 0
    h_main = x.reshape(batch * seq, d)
    h_meta = jnp.broadcast_to(meta_tokens.astype(x.dtype)[None], (batch, N_META, d)).reshape(batch * N_META, d)

    w_in_b = w_in[l].astype(BF16)
    u_col = 3 * na_width
    proj_main, u_main = norm_matmul(h_main, norm1_g[l], w_in_b, chunk_major=(u_col, ssm_width))
    proj_meta = norm_matmul(h_meta, norm1_g[l], w_in_b)

    o_main, o_meta = neighbourhood_attention(proj_main, proj_meta, na_rpb[l], batch, seq, na_width)

    ops = _ssm_operators(ssm_lam_re[l], ssm_lam_im[l], ssm_log_step[l], ssm_b_re[l], ssm_b_im[l],
                         ssm_c_re[l], ssm_c_im[l], ssm_d[l])
    y_main, y_meta = s5_scan(u_main, proj_meta[:, u_col:u_col + ssm_width], ops, batch, seq)

    w_na_b, w_glu_b, w_ssm_b = w_proj_na[l].astype(BF16), w_glu[l].astype(BF16), w_proj_ssm[l].astype(BF16)
    w_out_b = w_out[l].astype(BF16)
    gate_col = u_col + ssm_width
    h1 = []
    for h, o, y, proj in ((h_main, o_main, y_main, proj_main), (h_meta, o_meta, y_meta, proj_meta)):
        mixed = mix_branches(o, y, proj, w_na_b, w_glu_b, w_ssm_b, gate_col)
        h1.append(residual_matmul(h, mixed, w_out_b, norm2_g[l]))
    (h1_main, hn_main), (_, hn_meta) = h1

    tm = _pick(seq, ROW_TILES)
    halo_prev, halo_next = _conv_halos(hn_main, hn_meta, batch, seq, tm)
    act = ffn_up(hn_main, halo_prev, halo_next, w_up[l], conv_w[l], conv_b[l], tm)
    out = ffn_down(h1_main, act, w_down[l].astype(BF16), final_g)
    return out.reshape(batch, seq, d)
```

```python
import functools
import math

import jax
import jax.numpy as jnp
from jax import lax
from jax.experimental import pallas as pl
from jax.experimental.pallas import tpu as pltpu

F32 = jnp.float32
BF16 = jnp.bfloat16

N_META = 16
GRID_W = 64
ROW_WIN = 8
COL_WIN = 16
RMS_EPS = 1e-6

VMEM_LIMIT_BYTES = 56 * 1024 * 1024
F32_SUBLANES = 8
BF16_SUBLANES = 16
LANES = 128

MASK_BIAS = -1e30
NA_QROWS = 4
NA_KROWS = 12
NA_UNROLL = 16

SSM_CHUNK = 16
SSM_SUB = 2
SSM_SCAN_UNROLL = True
SSM_META_ROWS = BF16_SUBLANES

ROW_TILES = (1024, 512, 256, 128, 64)
MID_ROW_TILES = (512, 256, 128, 64)
COL_TILES = (512, 256, 128)
WIDE_COL_TILES = (1024, 512, 256, 128)


def _pick(dim, prefs):
    for t in prefs:
        if dim % t == 0:
            return t
    return dim


def _params(*sem):
    return pltpu.CompilerParams(dimension_semantics=sem, vmem_limit_bytes=VMEM_LIMIT_BYTES)


def _rms(x, g):
    ms = jnp.mean(x * x, axis=-1, keepdims=True)
    return x * lax.rsqrt(ms + RMS_EPS) * g


def _gelu(x):
    c = math.sqrt(2.0 / math.pi)
    return 0.5 * x * (1.0 + jnp.tanh(c * (x + 0.044715 * (x * x * x))))


def _sigmoid(x):
    return 0.5 * jnp.tanh(0.5 * x) + 0.5


def _norm_matmul_kernel(x_ref, g_ref, w_ref, o_ref, *rest, chunk_cols):
    hn_ref = rest[-1] if chunk_cols is None else rest[1]

    @pl.when(pl.program_id(1) == 0)
    def _():
        hn_ref[...] = _rms(x_ref[...], g_ref[...]).astype(BF16)

    r = jnp.dot(hn_ref[...], w_ref[...], preferred_element_type=F32)
    o_ref[...] = r.astype(o_ref.dtype)

    if chunk_cols is not None:
        u3_ref, _, stage_ref = rest
        tile, off, width = chunk_cols
        t = SSM_CHUNK

        @pl.when(pl.program_id(1) == tile)
        def _():
            for lt in range(width // LANES):
                stage_ref[lt] = r[:, off + lt * LANES:off + (lt + 1) * LANES]
            for k in range(t):
                for lt in range(width // LANES):
                    u3_ref[k, :, lt * LANES:(lt + 1) * LANES] = stage_ref[
                        lt, pl.ds(k, stage_ref.shape[1] // t, stride=t), :].astype(u3_ref.dtype)


def norm_matmul(x, g, w, chunk_major=None):
    rows, d = x.shape
    n = w.shape[1]
    tm = _pick(rows, ROW_TILES)
    tn = _pick(n, WIDE_COL_TILES)
    out_specs = [pl.BlockSpec((tm, tn), lambda i, j: (i, j))]
    out_shape = [jax.ShapeDtypeStruct((rows, n), BF16)]
    scratch = [pltpu.VMEM((tm, d), BF16)]
    chunk_cols = None
    if chunk_major is not None:
        col, width = chunk_major
        t = SSM_CHUNK
        chunk_cols = (col // tn, col % tn, width)
        assert col % tn + width <= tn and tm % (t * F32_SUBLANES) == 0
        out_specs.append(pl.BlockSpec((t, tm // t, width), lambda i, j: (0, i, 0)))
        out_shape.append(jax.ShapeDtypeStruct((t, rows // t, width), BF16))
        scratch.append(pltpu.VMEM((width // LANES, tm, LANES), F32))
    out = pl.pallas_call(
        functools.partial(_norm_matmul_kernel, chunk_cols=chunk_cols),
        grid=(rows // tm, n // tn),
        in_specs=[
            pl.BlockSpec((tm, d), lambda i, j: (i, 0)),
            pl.BlockSpec((1, d), lambda i, j: (0, 0)),
            pl.BlockSpec((d, tn), lambda i, j: (0, j)),
        ],
        out_specs=out_specs,
        out_shape=out_shape,
        scratch_shapes=scratch,
        compiler_params=_params("parallel", "arbitrary"),
        name="norm_in_proj",
    )(x, g.reshape(1, d), w)
    return out[0] if chunk_major is None else out


def _na_kernel(q_ref, k_ref, v_ref, qm_ref, km_ref, vm_ref, tab_ref, o_ref, om_ref, bias_ref, sm_ref, acc_ref,
               *, rows, dh):
    lane = lax.broadcasted_iota(jnp.int32, (1, 2 * dh), 1)
    head_masks = (lane < dh, lane >= dh)
    scale = dh ** -0.5
    km = km_ref[...]
    vm = vm_ref[...]
    nt = (((1,), (1,)), ((), ()))

    masked = 2 * ROW_WIN - 1
    nblk = rows // NA_QROWS
    qblk = NA_QROWS * GRID_W
    kblk = NA_KROWS * GRID_W

    def rel_row(kind, j, i):
        if kind == 0:
            return i - j + (ROW_WIN - 1) if i < ROW_WIN else masked
        if kind == 1:
            return i - j + (ROW_WIN - 1 - ROW_WIN // 2) if j <= i < j + ROW_WIN else masked
        lo = NA_KROWS - ROW_WIN
        return i - j + (NA_QROWS - NA_KROWS) + (ROW_WIN - 1) if i >= lo else masked

    first_row = lane < GRID_W
    for kind in range(3):
        for hh in range(2):
            for j in range(NA_QROWS):
                r_lo = (hh * NA_QROWS + j) * GRID_W
                for i2 in range(NA_KROWS // 2):
                    bias_ref[kind, r_lo:r_lo + GRID_W, i2 * 2 * GRID_W:(i2 + 1) * 2 * GRID_W] = jnp.where(
                        first_row, tab_ref[hh, rel_row(kind, j, 2 * i2)], tab_ref[hh, rel_row(kind, j, 2 * i2 + 1)])

    def one_head(x, hh):
        return jnp.where(head_masks[hh], x, jnp.zeros_like(x))

    qm = (qm_ref[...] * scale).astype(BF16)
    om = None
    for hh in range(2):
        s = lax.dot_general(one_head(qm, hh), km, nt, preferred_element_type=F32)
        p = jnp.exp(s - jnp.max(s, axis=-1, keepdims=True))
        o_h = jnp.dot(p.astype(BF16), vm, preferred_element_type=F32) / jnp.sum(p, axis=-1, keepdims=True)
        om = o_h if om is None else jnp.where(head_masks[1], o_h, om)
    om_ref[...] = om.astype(om_ref.dtype)

    q_all = (q_ref[...] * scale).astype(BF16)
    for hh in range(2):
        sm_ref[hh] = lax.dot_general(one_head(q_all, hh), km, nt, preferred_element_type=F32)

    def body(blk, carry):
        r = blk * NA_QROWS
        k0 = jnp.clip(r - ROW_WIN // 2, 0, rows - NA_KROWS)
        kind = jnp.where(blk == 0, 0, jnp.where(blk == nblk - 1, 2, 1))
        qrows = pl.ds(pl.multiple_of(r * GRID_W, qblk), qblk)
        krows = pl.ds(pl.multiple_of(k0 * GRID_W, GRID_W), kblk)
        qs = (q_ref[qrows, :] * scale).astype(BF16)
        q2 = jnp.concatenate([one_head(qs, 0), one_head(qs, 1)], axis=0)
        s = lax.dot_general(q2, k_ref[krows, :], nt, preferred_element_type=F32) + bias_ref[kind]
        s_m = jnp.concatenate([sm_ref[0, qrows, :], sm_ref[1, qrows, :]], axis=0)
        m = jnp.maximum(jnp.max(s, axis=-1, keepdims=True), jnp.max(s_m, axis=-1, keepdims=True))
        p = jnp.exp(s - m)
        p_m = jnp.exp(s_m - m)
        inv = 1.0 / (jnp.sum(p, axis=-1, keepdims=True) + jnp.sum(p_m, axis=-1, keepdims=True))
        acc = jnp.dot(p.astype(BF16), v_ref[krows, :], preferred_element_type=F32) * inv
        acc_ref[qrows, :] = jnp.where(head_masks[1], acc[qblk:], acc[:qblk])
        p_m = p_m * inv
        sm_ref[0, qrows, :] = p_m[:qblk]
        sm_ref[1, qrows, :] = p_m[qblk:]
        return carry

    lax.fori_loop(0, nblk, body, 0, unroll=NA_UNROLL)

    o_meta = [jnp.dot(sm_ref[hh].astype(BF16), vm, preferred_element_type=F32) for hh in range(2)]
    o_ref[...] = (acc_ref[...] + jnp.where(head_masks[1], o_meta[1], o_meta[0])).astype(o_ref.dtype)


def _na_bias_table(rpb):
    c = jnp.arange(GRID_W)[:, None]
    kc = jnp.arange(2 * GRID_W)[None, :] % GRID_W
    col_start = jnp.clip(c - COL_WIN // 2, 0, GRID_W - COL_WIN)
    valid = (kc >= col_start) & (kc < col_start + COL_WIN)
    dc = kc - c + (COL_WIN - 1)
    pick = (dc.reshape(-1)[None] == jnp.arange(2 * COL_WIN - 1)[:, None]).astype(F32)
    tab = jnp.dot(rpb.astype(F32).reshape(-1, 2 * COL_WIN - 1), pick, precision=lax.Precision.HIGHEST)
    tab = tab.reshape(rpb.shape[0], 2 * ROW_WIN - 1, GRID_W, 2 * GRID_W)
    tab = jnp.where(valid, tab, MASK_BIAS)
    return jnp.concatenate([tab, jnp.full_like(tab[:, :1], MASK_BIAS)], axis=1)


def neighbourhood_attention(proj_main, proj_meta, rpb, batch, seq, na_width):
    heads = rpb.shape[0]
    dh = na_width // heads
    rows = seq // GRID_W
    assert rows >= NA_KROWS and rows % NA_QROWS == 0 and heads % 2 == 0 and 2 * dh == LANES
    assert NA_KROWS % 2 == 0 and NA_KROWS >= ROW_WIN + NA_QROWS - 1 and NA_QROWS <= ROW_WIN // 2
    pairs = heads // 2
    bias = _na_bias_table(rpb)
    blk = (seq, 2 * dh)
    mblk = (N_META, 2 * dh)
    kernel = functools.partial(_na_kernel, rows=rows, dh=dh)
    return pl.pallas_call(
        kernel,
        grid=(batch, pairs),
        in_specs=[
            pl.BlockSpec(blk, lambda b, h: (b, h)),
            pl.BlockSpec(blk, lambda b, h: (b, pairs + h)),
            pl.BlockSpec(blk, lambda b, h: (b, 2 * pairs + h)),
            pl.BlockSpec(mblk, lambda b, h: (b, h)),
            pl.BlockSpec(mblk, lambda b, h: (b, pairs + h)),
            pl.BlockSpec(mblk, lambda b, h: (b, 2 * pairs + h)),
            pl.BlockSpec((2, 2 * ROW_WIN, GRID_W, 2 * GRID_W), lambda b, h: (h, 0, 0, 0)),
        ],
        out_specs=[
            pl.BlockSpec(blk, lambda b, h: (b, h)),
            pl.BlockSpec(mblk, lambda b, h: (b, h)),
        ],
        out_shape=[
            jax.ShapeDtypeStruct((batch * seq, na_width), BF16),
            jax.ShapeDtypeStruct((batch * N_META, na_width), BF16),
        ],
        scratch_shapes=[pltpu.VMEM((3, 2 * NA_QROWS * GRID_W, NA_KROWS * GRID_W), F32),
                        pltpu.VMEM((2, seq, N_META), F32),
                        pltpu.VMEM((seq, 2 * dh), F32)],
        compiler_params=_params("parallel", "parallel"),
        name="neighbourhood_attention",
    )(proj_main, proj_main, proj_main, proj_meta, proj_meta, proj_meta, bias)


def _chunk_ops_kernel(cp_ref, d_ref, shift_ref, pw_ref, bt_ref, me_ref, f_ref):
    gt, _, n, packed = cp_ref.shape
    c = bt_ref.shape[2]
    t, _, tc = shift_ref.shape
    w = pw_ref.shape[3]
    q = lax.broadcasted_iota(jnp.int32, (packed, tc), 1)
    p_lane = lax.broadcasted_iota(jnp.int32, (packed, tc), 0)
    k_row = lax.broadcasted_iota(jnp.int32, (c, tc), 0)

    def spread(parts, k, by_position):
        mat = (p_lane - k * c == (q // c if by_position else q % c)).astype(BF16)
        return sum(jnp.dot(x, mat, preferred_element_type=F32) for x in parts)

    lag_rows = [[None, None] for _ in range(gt)]
    for d in range(2):
        x = cp_ref[:, d].reshape(gt * n, packed)
        hi = x.astype(BF16)
        parts = (hi, (x - hi.astype(F32)).astype(BF16))
        cx_r, cx_i = spread(parts, 0, False), spread(parts, 1, False)
        e_r, e_i = spread(parts, 4, True), spread(parts, 5, True)
        f_ref[:, d * n:(d + 1) * n, :] = (cx_r * e_r - cx_i * e_i).reshape(gt, n, tc).astype(f_ref.dtype)
        f_ref[:, (2 + d) * n:(3 + d) * n, :] = (-(cx_r * e_i + cx_i * e_r)).reshape(gt, n, tc).astype(f_ref.dtype)
        e_r, e_i = spread(parts, 2, True), spread(parts, 3, True)
        z_r, z_i = cx_r * e_r - cx_i * e_i, cx_r * e_i + cx_i * e_r
        for g in range(gt):
            zr_g, zi_g = z_r[g * n:(g + 1) * n], z_i[g * n:(g + 1) * n]
            x_g = cp_ref[g, d]
            rows = jnp.zeros((c, tc), F32)
            for ci in range(c):
                b_r, b_i = x_g[:, 6 * c + ci:6 * c + ci + 1], x_g[:, 7 * c + ci:7 * c + ci + 1]
                row = jnp.sum(b_r * zr_g - b_i * zi_g, axis=0, keepdims=True)
                rows = jnp.where(k_row == ci, row, rows)
            lag_rows[g][d] = rows
    lag = jnp.concatenate([jnp.concatenate(lag_rows[g], axis=1) for g in range(gt)], axis=0).astype(BF16)

    lane = lax.broadcasted_iota(jnp.int32, (gt * c, tc), 1)
    c_in = lax.broadcasted_iota(jnp.int32, (gt * c, tc), 0) % c
    skip_gain = d_ref[...].reshape(gt * c, tc)
    b_r, b_i = bt_ref[:, 0], bt_ref[:, 1]
    for s in range(t):
        rows = slice(s * c, (s + 1) * c)
        blk = jnp.dot(lag, shift_ref[s], preferred_element_type=F32)
        blk = blk + jnp.where(lane == s * c + c_in, skip_gain, 0.0)
        me_ref[:, rows, 0:tc] = blk.reshape(gt, c, tc).astype(me_ref.dtype)
        e_r, e_i = pw_ref[:, 0, s:s + 1, :], pw_ref[:, 1, s:s + 1, :]
        me_ref[:, rows, tc:tc + w] = (e_r * b_r - e_i * b_i).astype(me_ref.dtype)
        me_ref[:, rows, tc + w:tc + 2 * w] = (e_r * b_i + e_i * b_r).astype(me_ref.dtype)


def _chunk_operators(cp, d_rows, pw, bt):
    g, _, n, packed = cp.shape
    t, c = pw.shape[2], bt.shape[2]
    tc = t * c
    w = pw.shape[-1]
    assert t == c and packed == 8 * c
    gt = _pick(g, (8, 4, 2, 1))
    p = jnp.arange(2 * tc)[None, :, None]
    q = jnp.arange(tc)[None, None, :]
    s = jnp.arange(t)[:, None, None]
    shift = jnp.where(p < tc, p == q - s * c, p - tc == q + (t - 1 - s) * c).astype(BF16)

    def per_group(shape):
        return pl.BlockSpec((gt,) + tuple(shape), lambda i: (i,) + (0,) * len(shape))

    return pl.pallas_call(
        _chunk_ops_kernel,
        grid=(g // gt,),
        in_specs=[
            per_group(cp.shape[1:]), per_group(d_rows.shape[1:]),
            pl.BlockSpec(shift.shape, lambda i: (0, 0, 0)),
            per_group(pw.shape[1:]), per_group(bt.shape[1:]),
        ],
        out_specs=[per_group((tc, tc + 2 * w)), per_group((4 * n, tc))],
        out_shape=[jax.ShapeDtypeStruct((g, tc, tc + 2 * w), BF16), jax.ShapeDtypeStruct((g, 4 * n, tc), BF16)],
        compiler_params=_params("parallel"),
        name="s5_chunk_operators",
    )(cp, d_rows, shift, pw, bt)


def _ssm_operators(lam_re, lam_im, log_step, b_re, b_im, c_re, c_im, d_skip):
    t = SSM_CHUNK
    lr, li = lam_re.astype(F32), lam_im.astype(F32)
    step = jnp.exp(log_step.astype(F32))[..., None]
    dt_r, dt_i = lr * step, li * step
    g, n = lr.shape[1], lr.shape[2]
    c = b_re.shape[-1]

    def lam_bar_pow(k):
        mag = jnp.exp(dt_r[:, :, None] * k)
        return mag * jnp.cos(dt_i[:, :, None] * k), mag * jnp.sin(dt_i[:, :, None] * k)

    p_r, p_i = lam_bar_pow(jnp.arange(t + 1, dtype=F32)[:, None])
    x_r, x_i = p_r[:, :, 1] - 1.0, p_i[:, :, 1]
    den = lr * lr + li * li
    q_r, q_i = (x_r * lr + x_i * li) / den, (x_i * lr - x_r * li) / den
    bt_r, bt_i = jnp.swapaxes(b_re.astype(F32), 2, 3), jnp.swapaxes(b_im.astype(F32), 2, 3)
    bb_r = q_r[:, :, None] * bt_r - q_i[:, :, None] * bt_i
    bb_i = q_r[:, :, None] * bt_i + q_i[:, :, None] * bt_r
    bn_r = q_r[..., None] * b_re.astype(F32) - q_i[..., None] * b_im.astype(F32)
    bn_i = q_r[..., None] * b_im.astype(F32) + q_i[..., None] * b_re.astype(F32)
    ct_r, ct_i = jnp.swapaxes(c_re.astype(F32), 2, 3), jnp.swapaxes(c_im.astype(F32), 2, 3)

    def cols(x):
        return jnp.swapaxes(x, 1, 2)

    factors = []
    for d, lag_sel, out_sel in ((0, lambda x: x[:, :t], lambda x: x[:, 1:]),
                                (1, lambda x: x[:, :t][:, ::-1], lambda x: x[:, ::-1][:, :t])):
        factors.append(jnp.concatenate([ct_r[d], ct_i[d], cols(lag_sel(p_r[d])), cols(lag_sel(p_i[d])),
                                        cols(out_sel(p_r[d])), cols(out_sel(p_i[d])), bn_r[d], bn_i[d]], axis=-1))
    cp = jnp.stack(factors, axis=1)
    d_rows = jnp.broadcast_to(d_skip.astype(F32).reshape(g, c, 1), (g, c, t * c))

    pw = jnp.stack([jnp.concatenate([p_r[0][:, ::-1][:, 1:], p_r[1][:, :t]], axis=-1),
                    jnp.concatenate([p_i[0][:, ::-1][:, 1:], p_i[1][:, :t]], axis=-1)], axis=1)
    bt = jnp.stack([jnp.concatenate([bb_r[0], bb_r[1]], axis=-1),
                    jnp.concatenate([bb_i[0], bb_i[1]], axis=-1)], axis=1)
    me, f = _chunk_operators(cp, d_rows, pw, bt)

    sub = F32_SUBLANES
    row = jnp.arange(sub)

    d_r, d_i = lam_bar_pow((t * (row + 1)).astype(F32)[:, None])
    d_r = jnp.concatenate([d_r[0], d_r[1]], axis=-1)
    d_i = jnp.concatenate([d_i[0], d_i[1]], axis=-1)

    def fixed(x, k, keep):
        return jnp.where(keep[None, :, None], x[:, k - 1:k], 0.0)

    slots = []
    for k in (1, 2, 4):
        slots += [fixed(d_r, k, row >= k), fixed(d_i, k, row >= k)]
    slots += [d_r, d_i]
    for k in (1, 2, 4):
        slots += [fixed(d_r, k, row < sub - k), fixed(d_i, k, row < sub - k)]
    slots += [d_r[:, ::-1], d_i[:, ::-1]]
    a = jnp.concatenate(slots, axis=1).reshape(g, len(slots), sub, 2 * n)

    toks = LANES // c
    grp = LANES // c
    src = jnp.transpose(jnp.arange(toks * grp * c).reshape(toks, grp, c), (1, 0, 2)).reshape(-1)
    perm = (jnp.arange(toks * grp * c)[:, None] == src[None, :]).astype(BF16)
    return perm, me.astype(BF16), f.astype(BF16), a.astype(F32)


def _ssm_kernel(*refs, batch, nchunks, nstate, gsub):
    t = SSM_CHUNK
    (u_ref, um_ref, p_ref, me_ref, f_ref, a_ref, y_ref, ym_ref,
     yi_ref, ee_ref, pf_ref, pr_ref, ycat_ref, ymcat_ref) = refs
    gtile = me_ref.shape[0]
    tc = f_ref.shape[2]
    toks = p_ref.shape[0] // LANES
    parts = t // toks
    sub = F32_SUBLANES
    w = 2 * nstate
    tiles = nchunks // sub
    perm = p_ref[...]
    row = lax.broadcasted_iota(jnp.int32, (sub, w), 0)
    lane = lax.broadcasted_iota(jnp.int32, (1, w), 1)
    rev_lane = lane >= nstate

    def group_major(token_refs):
        out = []
        for part in range(parts):
            x = jnp.concatenate([token_refs[part * toks + i] for i in range(toks)], axis=1)
            out.append(jnp.dot(x, perm, preferred_element_type=F32).astype(BF16))
        return out

    def group_chunk(regrouped, g):
        return jnp.concatenate([x[:, g * LANES:(g + 1) * LANES] for x in regrouped], axis=1)

    v_main = group_major(u_ref)
    v_meta = group_major(um_ref)

    def cmul_add(xr, xi, ar, ai, yr, yi):
        return xr + ar * yr - ai * yi, xi + ar * yi + ai * yr

    def bcast_row(x, r):
        return jnp.broadcast_to(x[r:r + 1, :], (sub, w))

    for sb in range(gtile // gsub):
        gs = [sb * gsub + gi for gi in range(gsub)]
        meta = []
        for gi, g in enumerate(gs):
            me = jnp.dot(group_chunk(v_main, g), me_ref[g], preferred_element_type=F32)
            yi_ref[gi] = me[:, :tc]
            ee_ref[gi] = me[:, tc:]
            meta.append(jnp.dot(group_chunk(v_meta, g), me_ref[g], preferred_element_type=F32))

        def body(j, carry, gs=gs):
            new = []
            for gi, g in enumerate(gs):
                for b in range(batch):
                    lr, li, fr, fi = carry[gi * batch + b]
                    fs = pl.ds(pl.multiple_of(b * nchunks + j * sub, sub), sub)
                    xr, xi = ee_ref[gi, fs, 0:w], ee_ref[gi, fs, w:2 * w]
                    for step, k in enumerate((1, 2, 4)):
                        xr, xi = cmul_add(xr, xi, a_ref[g, 2 * step], a_ref[g, 2 * step + 1],
                                          pltpu.roll(xr, k, 0), pltpu.roll(xi, k, 0))
                    sr, si = cmul_add(xr, xi, a_ref[g, 6], a_ref[g, 7], lr, li)
                    pf_ref[gi, fs, 0:w] = jnp.where(row == 0, lr, pltpu.roll(sr, 1, 0))
                    pf_ref[gi, fs, w:2 * w] = jnp.where(row == 0, li, pltpu.roll(si, 1, 0))
                    rs = pl.ds(pl.multiple_of(b * nchunks + (tiles - 1 - j) * sub, sub), sub)
                    xr, xi = ee_ref[gi, rs, 0:w], ee_ref[gi, rs, w:2 * w]
                    for step, k in enumerate((1, 2, 4)):
                        xr, xi = cmul_add(xr, xi, a_ref[g, 8 + 2 * step], a_ref[g, 9 + 2 * step],
                                          pltpu.roll(xr, sub - k, 0), pltpu.roll(xi, sub - k, 0))
                    rr, ri = cmul_add(xr, xi, a_ref[g, 14], a_ref[g, 15], fr, fi)
                    pr_ref[gi, rs, 0:w] = jnp.where(row == sub - 1, fr, pltpu.roll(rr, sub - 1, 0))
                    pr_ref[gi, rs, w:2 * w] = jnp.where(row == sub - 1, fi, pltpu.roll(ri, sub - 1, 0))
                    new.append((bcast_row(sr, sub - 1), bcast_row(si, sub - 1), bcast_row(rr, 0), bcast_row(ri, 0)))
            return tuple(new)

        zero = jnp.zeros((sub, w), F32)
        init = tuple((bcast_row(meta[gi][:, tc:tc + w], b), bcast_row(meta[gi][:, tc + w:tc + 2 * w], b), zero, zero)
                     for gi in range(gsub) for b in range(batch))
        final = lax.fori_loop(0, tiles, body, init, unroll=SSM_SCAN_UNROLL)

        for gi, g in enumerate(gs):
            prev = jnp.concatenate([jnp.where(rev_lane, pr_ref[gi, :, k * w:(k + 1) * w], pf_ref[gi, :, k * w:(k + 1) * w])
                                    for k in range(2)], axis=1).astype(BF16)
            y = (yi_ref[gi] + jnp.dot(prev, f_ref[g], preferred_element_type=F32)).astype(BF16)
            pm = []
            for k in (2, 3):
                x = zero
                for b in range(batch):
                    x = jnp.where((row == b) & rev_lane, final[gi * batch + b][k], x)
                pm.append(jnp.concatenate([x, jnp.zeros((SSM_META_ROWS - sub, w), F32)], axis=0))
            prev_m = jnp.concatenate(pm, axis=1).astype(BF16)
            y_m = (meta[gi][:, :tc] + jnp.dot(prev_m, f_ref[g], preferred_element_type=F32)).astype(BF16)
            for part in range(parts):
                ycat_ref[part, :, g * LANES:(g + 1) * LANES] = y[:, part * LANES:(part + 1) * LANES]
                ymcat_ref[part, :, g * LANES:(g + 1) * LANES] = y_m[:, part * LANES:(part + 1) * LANES]

    for part in range(parts):
        o = jnp.dot(ycat_ref[part], perm, preferred_element_type=F32).astype(y_ref.dtype)
        o_m = jnp.dot(ymcat_ref[part], perm, preferred_element_type=F32).astype(ym_ref.dtype)
        for i in range(toks):
            y_ref[part * toks + i] = o[:, i * LANES:(i + 1) * LANES]
            ym_ref[part * toks + i] = o_m[:, i * LANES:(i + 1) * LANES]


def s5_scan(u_main, u_meta_rows, ops, batch, seq):
    perm, me, f, a = ops
    g = me.shape[0]
    t = SSM_CHUNK
    tc = f.shape[2]
    c = tc // t
    width = g * c
    nstate = f.shape[1] // 4
    gtile = LANES // c
    nchunks = seq // t
    rows = batch * nchunks
    assert N_META == t and seq % (t * F32_SUBLANES) == 0 and batch <= F32_SUBLANES
    assert g % gtile == 0 and gtile % SSM_SUB == 0 and u_main.shape == (t, rows, width)

    u_meta = jnp.transpose(u_meta_rows.reshape(batch, t, width), (1, 0, 2))
    u_meta = jnp.pad(u_meta, ((0, 0), (0, SSM_META_ROWS - batch), (0, 0)))
    kernel = functools.partial(_ssm_kernel, batch=batch, nchunks=nchunks, nstate=nstate, gsub=SSM_SUB)
    y, y_m = pl.pallas_call(
        kernel,
        grid=(g // gtile,),
        in_specs=[
            pl.BlockSpec((t, rows, LANES), lambda i: (0, 0, i)),
            pl.BlockSpec((t, SSM_META_ROWS, LANES), lambda i: (0, 0, i)),
            pl.BlockSpec(perm.shape, lambda i: (0, 0)),
            pl.BlockSpec((gtile,) + me.shape[1:], lambda i: (i, 0, 0)),
            pl.BlockSpec((gtile,) + f.shape[1:], lambda i: (i, 0, 0)),
            pl.BlockSpec((gtile,) + a.shape[1:], lambda i: (i, 0, 0, 0)),
        ],
        out_specs=[
            pl.BlockSpec((t, rows, LANES), lambda i: (0, 0, i)),
            pl.BlockSpec((t, SSM_META_ROWS, LANES), lambda i: (0, 0, i)),
        ],
        out_shape=[
            jax.ShapeDtypeStruct((t, rows, width), BF16),
            jax.ShapeDtypeStruct((t, SSM_META_ROWS, width), BF16),
        ],
        scratch_shapes=[
            pltpu.VMEM((SSM_SUB, rows, tc), F32),
            pltpu.VMEM((SSM_SUB, rows, 4 * nstate), F32),
            pltpu.VMEM((SSM_SUB, rows, 4 * nstate), F32),
            pltpu.VMEM((SSM_SUB, rows, 4 * nstate), F32),
            pltpu.VMEM((t // (perm.shape[0] // LANES), rows, gtile * LANES), BF16),
            pltpu.VMEM((t // (perm.shape[0] // LANES), SSM_META_ROWS, gtile * LANES), BF16),
        ],
        compiler_params=_params("parallel"),
        name="s5_scan",
    )(u_main, u_meta, perm, me, f, a)

    y_meta = jnp.transpose(y_m[:, :batch], (1, 0, 2)).reshape(batch * N_META, width)
    return y, y_meta


def _mix_kernel(o_ref, y_ref, gna_ref, gssm_ref, wna_ref, wglu_ref, wssm_ref, *rest, chunk_major):
    out_ref = rest[-1]
    y_na = jnp.dot(o_ref[...], wna_ref[...], preferred_element_type=F32)
    if chunk_major:
        rowperm_ref = rest[0]
        t, nch, width = y_ref.shape
        y = jnp.dot(rowperm_ref[...], y_ref[...].reshape(t * nch, width), preferred_element_type=F32)
    else:
        y = y_ref[...].astype(F32)
    gl = _gelu(y)
    z = jnp.dot(gl.astype(BF16), wglu_ref[...], preferred_element_type=F32)
    t = (gl * _sigmoid(z)).astype(BF16)
    y_ssm = jnp.dot(t, wssm_ref[...], preferred_element_type=F32)
    mixed = _sigmoid(gna_ref[...].astype(F32)) * y_na + _sigmoid(gssm_ref[...].astype(F32)) * y_ssm
    out_ref[...] = mixed.astype(out_ref.dtype)


def mix_branches(o_na, y, proj, w_na, w_glu, w_ssm, gate_col):
    rows, na_width = o_na.shape
    ssm_width = y.shape[-1]
    d = w_na.shape[1]
    tm = _pick(rows, MID_ROW_TILES)
    assert gate_col % d == 0
    gblk = gate_col // d
    resident = dict(pipeline_mode=pl.Buffered(1))
    chunk_major = y.ndim == 3
    extra, extra_specs = [], []
    if chunk_major:
        t = y.shape[0]
        y_spec = pl.BlockSpec((t, tm // t, ssm_width), lambda i: (0, i, 0))
        tok = jnp.arange(tm)
        src = (tok % t) * (tm // t) + tok // t
        extra = [(src[:, None] == jnp.arange(tm)[None, :]).astype(BF16)]
        extra_specs = [pl.BlockSpec((tm, tm), lambda i: (0, 0), **resident)]
    else:
        y_spec = pl.BlockSpec((tm, ssm_width), lambda i: (i, 0))
    return pl.pallas_call(
        functools.partial(_mix_kernel, chunk_major=chunk_major),
        grid=(rows // tm,),
        in_specs=[
            pl.BlockSpec((tm, na_width), lambda i: (i, 0)),
            y_spec,
            pl.BlockSpec((tm, d), lambda i: (i, gblk)),
            pl.BlockSpec((tm, d), lambda i: (i, gblk + 1)),
            pl.BlockSpec(w_na.shape, lambda i: (0, 0), **resident),
            pl.BlockSpec(w_glu.shape, lambda i: (0, 0), **resident),
            pl.BlockSpec(w_ssm.shape, lambda i: (0, 0), **resident),
        ] + extra_specs,
        out_specs=pl.BlockSpec((tm, d), lambda i: (i, 0)),
        out_shape=jax.ShapeDtypeStruct((rows, d), BF16),
        compiler_params=_params("parallel"),
        name="mix_branches",
    )(o_na, y, proj, proj, w_na, w_glu, w_ssm, *extra)


def _residual_matmul_kernel(h_ref, a_ref, w_ref, g_ref, o_ref, n_ref):
    h1 = h_ref[...] + jnp.dot(a_ref[...], w_ref[...], preferred_element_type=F32)
    o_ref[...] = h1
    n_ref[...] = _rms(h1, g_ref[...]).astype(n_ref.dtype)


def residual_matmul(h, a, w, g):
    rows, d = h.shape
    tm = _pick(rows, MID_ROW_TILES)
    row_spec = pl.BlockSpec((tm, d), lambda i: (i, 0))
    return pl.pallas_call(
        _residual_matmul_kernel,
        grid=(rows // tm,),
        in_specs=[
            row_spec,
            pl.BlockSpec((tm, a.shape[1]), lambda i: (i, 0)),
            pl.BlockSpec(w.shape, lambda i: (0, 0), pipeline_mode=pl.Buffered(1)),
            pl.BlockSpec((1, d), lambda i: (0, 0)),
        ],
        out_specs=[row_spec, row_spec],
        out_shape=[jax.ShapeDtypeStruct((rows, d), F32), jax.ShapeDtypeStruct((rows, d), BF16)],
        compiler_params=_params("parallel"),
        name="residual_out_proj",
    )(h, a, w, g.reshape(1, d))


HALO = BF16_SUBLANES


def _ffn_up_kernel(h_ref, prev_ref, next_ref, wa_ref, wg_ref, cw_ref, cb_ref, o_ref, hn_ref, *, tm):
    @pl.when(pl.program_id(1) == 0)
    def _():
        hn_ref[0:HALO, :] = prev_ref[...]
        hn_ref[HALO:HALO + tm, :] = h_ref[...]
        hn_ref[HALO + tm:, :] = next_ref[...]

    ext = tm + 2 * HALO
    a = jnp.dot(hn_ref[...], wa_ref[...].astype(BF16), preferred_element_type=F32)
    gate = jnp.dot(hn_ref[HALO:HALO + tm, :], wg_ref[...].astype(BF16), preferred_element_type=F32)
    a_prev = pltpu.roll(a, 1, 0)[HALO:HALO + tm]
    a_next = pltpu.roll(a, ext - 1, 0)[HALO:HALO + tm]
    conv = a_prev * cw_ref[0:1, :] + a[HALO:HALO + tm] * cw_ref[1:2, :] + a_next * cw_ref[2:3, :] + cb_ref[...]
    o_ref[...] = (_gelu(conv) * gate).astype(o_ref.dtype)


def ffn_up(h, halo_prev, halo_next, w_up, conv_w, conv_b, tm):
    rows, d = h.shape
    dff = conv_b.shape[0]
    tn = _pick(dff, COL_TILES)
    nj = dff // tn
    kernel = functools.partial(_ffn_up_kernel, tm=tm)
    return pl.pallas_call(
        kernel,
        grid=(rows // tm, nj),
        in_specs=[
            pl.BlockSpec((tm, d), lambda i, j: (i, 0)),
            pl.BlockSpec((HALO, d), lambda i, j: (i, 0)),
            pl.BlockSpec((HALO, d), lambda i, j: (i, 0)),
            pl.BlockSpec((d, tn), lambda i, j: (0, j)),
            pl.BlockSpec((d, tn), lambda i, j: (0, nj + j)),
            pl.BlockSpec((conv_w.shape[0], tn), lambda i, j: (0, j)),
            pl.BlockSpec((1, tn), lambda i, j: (0, j)),
        ],
        out_specs=pl.BlockSpec((tm, tn), lambda i, j: (i, j)),
        out_shape=jax.ShapeDtypeStruct((rows, dff), BF16),
        scratch_shapes=[pltpu.VMEM((tm + 2 * HALO, d), BF16)],
        compiler_params=_params("parallel", "arbitrary"),
        name="ffn_up_conv_gate",
    )(h, halo_prev, halo_next, w_up, w_up, conv_w, conv_b.reshape(1, dff))


def _ffn_down_kernel(h_ref, a_ref, w_ref, g_ref, o_ref):
    k = pl.program_id(1)
    last = pl.num_programs(1) - 1

    def part():
        return jnp.dot(a_ref[...], w_ref[...], preferred_element_type=F32)

    @pl.when(k == 0)
    def _():
        o_ref[...] = h_ref[...] + part()

    @pl.when((k > 0) & (k < last))
    def _():
        o_ref[...] += part()

    @pl.when(k == last)
    def _():
        o_ref[...] = _rms(o_ref[...] + part(), g_ref[...])


def ffn_down(h, act, w_down, g):
    rows, d = h.shape
    dff = act.shape[1]
    tm = _pick(rows, ROW_TILES)
    tk = _pick(dff, COL_TILES)
    assert dff // tk >= 2
    return pl.pallas_call(
        _ffn_down_kernel,
        grid=(rows // tm, dff // tk),
        in_specs=[
            pl.BlockSpec((tm, d), lambda i, k: (i, 0)),
            pl.BlockSpec((tm, tk), lambda i, k: (i, k)),
            pl.BlockSpec((tk, d), lambda i, k: (k, 0)),
            pl.BlockSpec((1, d), lambda i, k: (0, 0)),
        ],
        out_specs=pl.BlockSpec((tm, d), lambda i, k: (i, 0)),
        out_shape=jax.ShapeDtypeStruct((rows, d), F32),
        compiler_params=_params("parallel", "arbitrary"),
        name="ffn_down_final_norm",
    )(h, act, w_down, g.reshape(1, d))


def _conv_halos(hn_main, hn_meta, batch, seq, tm):
    d = hn_main.shape[1]
    per_seq = seq // tm
    hm = hn_main.reshape(batch, per_seq, tm, d)
    meta_tail = hn_meta.reshape(batch, 1, N_META, d)[:, :, N_META - HALO:]
    prev = jnp.concatenate([meta_tail, hm[:, :-1, tm - HALO:]], axis=1)
    nxt = jnp.concatenate([hm[:, 1:, :HALO], jnp.zeros((batch, 1, HALO, d), hn_main.dtype)], axis=1)
    return prev.reshape(batch * per_seq * HALO, d), nxt.reshape(batch * per_seq * HALO, d)


def kernel(x, meta_tokens, norm1_g, w_in, na_rpb, ssm_lam_re, ssm_lam_im, ssm_log_step, ssm_b_re, ssm_b_im,
           ssm_c_re, ssm_c_im, ssm_d, w_glu, w_proj_na, w_proj_ssm, w_out, norm2_g, w_up, conv_w, conv_b,
           w_down, final_g):
    batch, seq, d = x.shape
    depth = w_in.shape[0]
    na_width = w_proj_na.shape[1]
    ssm_width = w_proj_ssm.shape[1]
    assert depth == 1 and N_META >= HALO
    l = 0
    h_main = x.reshape(batch * seq, d)
    h_meta = jnp.broadcast_to(meta_tokens.astype(x.dtype)[None], (batch, N_META, d)).reshape(batch * N_META, d)

    w_in_b = w_in[l].astype(BF16)
    u_col = 3 * na_width
    proj_main, u_main = norm_matmul(h_main, norm1_g[l], w_in_b, chunk_major=(u_col, ssm_width))
    proj_meta = norm_matmul(h_meta, norm1_g[l], w_in_b)

    o_main, o_meta = neighbourhood_attention(proj_main, proj_meta, na_rpb[l], batch, seq, na_width)

    ops = _ssm_operators(ssm_lam_re[l], ssm_lam_im[l], ssm_log_step[l], ssm_b_re[l], ssm_b_im[l],
                         ssm_c_re[l], ssm_c_im[l], ssm_d[l])
    y_main, y_meta = s5_scan(u_main, proj_meta[:, u_col:u_col + ssm_width], ops, batch, seq)

    w_na_b, w_glu_b, w_ssm_b = w_proj_na[l].astype(BF16), w_glu[l].astype(BF16), w_proj_ssm[l].astype(BF16)
    w_out_b = w_out[l].astype(BF16)
    gate_col = u_col + ssm_width
    h1 = []
    for h, o, y, proj in ((h_main, o_main, y_main, proj_main), (h_meta, o_meta, y_meta, proj_meta)):
        mixed = mix_branches(o, y, proj, w_na_b, w_glu_b, w_ssm_b, gate_col)
        h1.append(residual_matmul(h, mixed, w_out_b, norm2_g[l]))
    (h1_main, hn_main), (_, hn_meta) = h1

    tm = _pick(seq, ROW_TILES)
    halo_prev, halo_next = _conv_halos(hn_main, hn_meta, batch, seq, tm)
    act = ffn_up(hn_main, halo_prev, halo_next, w_up[l], conv_w[l], conv_b[l], tm)
    out = ffn_down(h1_main, act, w_down[l].astype(BF16), final_g)
    return out.reshape(batch, seq, d)
```
